```python
import math
import jax, jax.numpy as jnp
from jax import lax
import numpy as np

D_MODEL = 1024
BATCH = 2
SEQ = 16384
DEPTH = 1
DEC_BATCH = 32
DEC_SEQ = 16
PAST_LEN = 2048

CHUNK = 64
Q_BLOCK = 128
EPS = 1e-6

MLA_HEADS = 8
QK_NOPE = 64
QK_ROPE = 32
V_HEAD = 64
Q_LORA = 384
KV_LORA = 256
ROPE_THETA = 10000.0
MLA_WIDTH = MLA_HEADS * V_HEAD
MLA_SCALE = 1.0 / math.sqrt(QK_NOPE + QK_ROPE)

HG_HEADS = 4
HG_EXPAND = 128
HG_VDIM = 128
HG_WIDTH = HG_HEADS * HG_EXPAND
HG_SCALE = 1.0 / math.sqrt(HG_EXPAND)

D_FF = ((8 * D_MODEL // 3 + 255) // 256) * 256

IN_SIZES = (Q_LORA, KV_LORA, QK_ROPE, HG_WIDTH, HG_WIDTH, HG_WIDTH, HG_WIDTH, D_MODEL, D_MODEL)
IN_DIM = sum(IN_SIZES)
IN_OFFSETS = tuple(int(v) for v in np.cumsum(IN_SIZES)[:-1])

kernel_name = "hybrid_mla_hgrn2_streaming_step"


def rms_norm(x, w):
    xf = x.astype(jnp.float32)
    y = xf * lax.rsqrt(jnp.mean(xf * xf, axis=-1, keepdims=True) + EPS)
    return (y * w.astype(jnp.float32)).astype(x.dtype)


def rope(x, pos):
    half = x.shape[-1] // 2
    inv = ROPE_THETA ** (-jnp.arange(half, dtype=jnp.float32) / half)
    ang = pos.astype(jnp.float32)[:, None] * inv[None, :]
    shp = (1, pos.shape[0]) + (1,) * (x.ndim - 3) + (half,)
    cos = jnp.cos(ang).reshape(shp)
    sin = jnp.sin(ang).reshape(shp)
    xf = x.astype(jnp.float32)
    x1, x2 = xf[..., :half], xf[..., half:]
    return jnp.concatenate([x1 * cos - x2 * sin, x1 * sin + x2 * cos], axis=-1).astype(x.dtype)


def chunk_attention(q, k, v, q_chunk, k_chunk):
    s = jnp.einsum('bqhd,bkhd->bhqk', q, k).astype(jnp.float32) * MLA_SCALE
    mask = k_chunk[None, :] <= q_chunk[:, None]
    s = jnp.where(mask[None, None], s, -jnp.inf)
    p = jax.nn.softmax(s, axis=-1).astype(v.dtype)
    return jnp.einsum('bhqk,bkhd->bqhd', p, v)


def hgrn2_recurrence(q, k, v, g, s0, block):
    B, T, H, K = q.shape
    V = v.shape[-1]
    nb = T // block
    f32 = jnp.float32

    def to_blocks(a):
        return a.astype(f32).reshape(B, nb, block, H, a.shape[-1]).transpose(1, 0, 3, 2, 4)

    tri = jnp.tril(jnp.ones((block, block), dtype=bool))[None, None, :, :, None]

    def step(S, blk):
        qb, kb, vb, gb = blk
        bc = jnp.cumsum(gb, axis=2)
        diff = bc[:, :, :, None, :] - bc[:, :, None, :, :]
        dec = jnp.exp(jnp.where(tri, diff, -jnp.inf))
        att = jnp.einsum('bhtk,bhsk,bhtsk->bhts', qb, kb, dec)
        o = (jnp.einsum('bhts,bhsv->bhtv', att, vb)
             + jnp.einsum('bhtk,bhkv->bhtv', qb * jnp.exp(bc), S))
        last = bc[:, :, -1:, :]
        S = (jnp.exp(last[:, :, 0, :])[..., None] * S
             + jnp.einsum('bhsk,bhsv->bhkv', kb * jnp.exp(last - bc), vb))
        return S, o

    S, o = lax.scan(step, s0.astype(f32), (to_blocks(q), to_blocks(k), to_blocks(v), to_blocks(g)))
    o = o.transpose(1, 0, 3, 2, 4).reshape(B, T, H, V)
    return o, S


def token_mixers(h, pos, p, l, past_ckv, past_krope, s0, hg_block):
    B, T, _ = h.shape
    z = h @ p["w_in"][l]
    q_lat, kv_lat, kr_raw, hq, hf, hi, hgate, ga, gb = jnp.split(z, IN_OFFSETS, axis=-1)

    c_q = rms_norm(q_lat, p["q_norm"][l])
    q = (c_q @ p["w_uq"][l]).reshape(B, T, MLA_HEADS, QK_NOPE + QK_ROPE)
    q = jnp.concatenate([q[..., :QK_NOPE], rope(q[..., QK_NOPE:], pos)], axis=-1)
    c_kv = rms_norm(kv_lat, p["kv_norm"][l])
    k_rope = rope(kr_raw, pos)
    if past_ckv is None:
        all_ckv, all_kr, k_pos = c_kv, k_rope, pos
    else:
        all_ckv = jnp.concatenate([past_ckv.astype(c_kv.dtype), c_kv], axis=1)
        all_kr = jnp.concatenate([past_krope.astype(k_rope.dtype), k_rope], axis=1)
        k_pos = jnp.arange(all_ckv.shape[1], dtype=jnp.int32)
    Tk = all_ckv.shape[1]
    kv = (all_ckv @ p["w_ukv"][l]).reshape(B, Tk, MLA_HEADS, QK_NOPE + V_HEAD)
    k = jnp.concatenate([kv[..., :QK_NOPE],
                         jnp.broadcast_to(all_kr[:, :, None, :], (B, Tk, MLA_HEADS, QK_ROPE))], axis=-1)
    v = kv[..., QK_NOPE:]
    q_chunk = pos // CHUNK
    k_chunk = k_pos // CHUNK
    if T <= Q_BLOCK:
        attn = chunk_attention(q, k, v, q_chunk, k_chunk)
    else:
        nqb = T // Q_BLOCK
        qb = q.reshape(B, nqb, Q_BLOCK, MLA_HEADS, QK_NOPE + QK_ROPE).transpose(1, 0, 2, 3, 4)
        qc = q_chunk.reshape(nqb, Q_BLOCK)
        attn = lax.map(lambda a: chunk_attention(a[0], k, v, a[1], k_chunk), (qb, qc))
        attn = attn.transpose(1, 0, 2, 3, 4).reshape(B, T, MLA_HEADS, V_HEAD)

    lb = jnp.cumsum(jax.nn.softmax(p["lb_param"].astype(jnp.float32), axis=0), axis=0)[l]
    hf32 = hf.astype(jnp.float32)
    log_f = jnp.log(lb + (1.0 - lb) * jax.nn.sigmoid(hf32))
    k_in = (1.0 - lb) * jax.nn.sigmoid(-hf32)
    qh = jax.nn.silu(hq.astype(jnp.float32)) * HG_SCALE
    rs = lambda a, d: a.reshape(B, T, HG_HEADS, d)
    o, s_new = hgrn2_recurrence(rs(qh, HG_EXPAND), rs(k_in, HG_EXPAND), rs(hi, HG_VDIM),
                                rs(log_f, HG_EXPAND), s0, hg_block)
    o = rms_norm(o.astype(h.dtype), p["hg_norm"][l]) * jax.nn.silu(rs(hgate, HG_VDIM))

    y_a = attn.reshape(B, T, MLA_WIDTH) @ p["w_pa"][l]
    y_b = o.reshape(B, T, HG_HEADS * HG_VDIM) @ p["w_pb"][l]
    mixed = jax.nn.sigmoid(ga) * y_a + jax.nn.sigmoid(gb) * y_b
    return mixed @ p["w_out"][l], c_kv, k_rope, s_new


def trunk(x, c, pos, p, caches_ckv, caches_kr, states, hg_block):
    B = x.shape[0]
    new_ckv, new_kr, new_s = [], [], []
    for l in range(DEPTH):
        mod = jax.nn.silu(c) @ p["w_ada"][l] + p["b_ada"][l]
        sh1, sc1, g1, sh2, sc2, g2 = [t[:, None, :] for t in jnp.split(mod, 6, axis=-1)]
        h = rms_norm(x, p["norm1"][l]) * (1.0 + sc1) + sh1
        if caches_ckv is None:
            past_ckv, past_kr = None, None
            s0 = jnp.zeros((B, HG_HEADS, HG_EXPAND, HG_VDIM), dtype=jnp.float32)
        else:
            past_ckv, past_kr, s0 = caches_ckv[l], caches_kr[l], states[l]
        m, ckv, kr, s = token_mixers(h, pos, p, l, past_ckv, past_kr, s0, hg_block)
        x = x + g1 * m
        h = rms_norm(x, p["norm2"][l]) * (1.0 + sc2) + sh2
        gate, up = jnp.split(h @ p["w_gu"][l], 2, axis=-1)
        x = x + g2 * ((jax.nn.silu(gate) * up) @ p["w_down"][l])
        new_ckv.append(ckv)
        new_kr.append(kr)
        new_s.append(s.astype(x.dtype))
    y = rms_norm(x, p["final_norm"])
    return y, jnp.stack(new_ckv), jnp.stack(new_kr), jnp.stack(new_s)


def setup_inputs(seed: int = 0) -> dict:
    key = jax.random.key(seed)
    ks = jax.random.split(key, 32)
    f32 = jnp.float32
    nrm = lambda k, shape, s: jax.random.normal(k, shape, f32) * s
    gain = lambda k, shape: 1.0 + 0.02 * jax.random.normal(k, shape, f32)
    return {
        "x_prompt": nrm(ks[0], (BATCH, SEQ, D_MODEL), 1.0),
        "x_sample": nrm(ks[1], (DEC_BATCH, DEC_SEQ, D_MODEL), 1.0),
        "cache_ckv": nrm(ks[2], (DEPTH, DEC_BATCH, PAST_LEN, KV_LORA), 1.0),
        "cache_krope": nrm(ks[3], (DEPTH, DEC_BATCH, PAST_LEN, QK_ROPE), 1.0),
        "state_hgrn": nrm(ks[4], (DEPTH, DEC_BATCH, HG_HEADS, HG_EXPAND, HG_VDIM), 0.3),
        "c_prompt": nrm(ks[5], (BATCH, D_MODEL), 1.0),
        "c_sample": nrm(ks[6], (DEC_BATCH, D_MODEL), 1.0),
        "w_in": nrm(ks[7], (DEPTH, D_MODEL, IN_DIM), D_MODEL ** -0.5),
        "q_norm": gain(ks[8], (DEPTH, Q_LORA)),
        "w_uq": nrm(ks[9], (DEPTH, Q_LORA, MLA_HEADS * (QK_NOPE + QK_ROPE)), Q_LORA ** -0.5),
        "kv_norm": gain(ks[10], (DEPTH, KV_LORA)),
        "w_ukv": nrm(ks[11], (DEPTH, KV_LORA, MLA_HEADS * (QK_NOPE + V_HEAD)), KV_LORA ** -0.5),
        "lb_param": nrm(ks[12], (DEPTH + 1, HG_WIDTH), 0.5),
        "hg_norm": gain(ks[13], (DEPTH, HG_VDIM)),
        "w_pa": nrm(ks[14], (DEPTH, MLA_WIDTH, D_MODEL), MLA_WIDTH ** -0.5),
        "w_pb": nrm(ks[15], (DEPTH, HG_HEADS * HG_VDIM, D_MODEL), (HG_HEADS * HG_VDIM) ** -0.5),
        "w_out": nrm(ks[16], (DEPTH, D_MODEL, D_MODEL), D_MODEL ** -0.5),
        "norm1": gain(ks[17], (DEPTH, D_MODEL)),
        "norm2": gain(ks[18], (DEPTH, D_MODEL)),
        "w_ada": nrm(ks[19], (DEPTH, D_MODEL, 6 * D_MODEL), 0.5 * D_MODEL ** -0.5),
        "b_ada": nrm(ks[20], (DEPTH, 6 * D_MODEL), 0.02),
        "w_gu": nrm(ks[21], (DEPTH, D_MODEL, 2 * D_FF), D_MODEL ** -0.5),
        "w_down": nrm(ks[22], (DEPTH, D_FF, D_MODEL), D_FF ** -0.5),
        "final_norm": gain(ks[23], (D_MODEL,)),
    }


def reference(x_prompt, x_sample, cache_ckv, cache_krope, state_hgrn, c_prompt, c_sample,
              w_in, q_norm, w_uq, kv_norm, w_ukv, lb_param, hg_norm, w_pa, w_pb, w_out,
              norm1, norm2, w_ada, b_ada, w_gu, w_down, final_norm):
    p = {"w_in": w_in, "q_norm": q_norm, "w_uq": w_uq, "kv_norm": kv_norm, "w_ukv": w_ukv,
         "lb_param": lb_param, "hg_norm": hg_norm, "w_pa": w_pa, "w_pb": w_pb, "w_out": w_out,
         "norm1": norm1, "norm2": norm2, "w_ada": w_ada, "b_ada": b_ada, "w_gu": w_gu,
         "w_down": w_down, "final_norm": final_norm}
    t_prompt = x_prompt.shape[1]
    t_sample = x_sample.shape[1]
    pos_prompt = jnp.arange(t_prompt, dtype=jnp.int32)
    pos_sample = PAST_LEN + jnp.arange(t_sample, dtype=jnp.int32)
    y_prompt, ckv_p, kr_p, s_p = trunk(x_prompt, c_prompt, pos_prompt, p, None, None, None, CHUNK)
    y_sample, ckv_s, kr_s, s_s = trunk(x_sample, c_sample, pos_sample, p, cache_ckv, cache_krope,
                                       state_hgrn, t_sample)
    return (y_prompt, y_sample, ckv_p, kr_p, s_p, ckv_s, kr_s, s_s)
```

```python
import functools
import math

import numpy as np
import jax
import jax.numpy as jnp
from jax import lax
from jax.experimental import pallas as pl
from jax.experimental.pallas import tpu as pltpu

F32 = jnp.float32
BF16 = jnp.bfloat16

D_MODEL = 1024
CHUNK = 64
EPS = 1e-6
MLA_HEADS = 8
QK_NOPE = 64
QK_ROPE = 32
V_HEAD = 64
Q_LORA = 384
KV_LORA = 256
ROPE_THETA = 10000.0
MLA_WIDTH = MLA_HEADS * V_HEAD
MLA_SCALE = 1.0 / math.sqrt(QK_NOPE + QK_ROPE)
HG_HEADS = 4
HG_EXPAND = 128
HG_VDIM = 128
HG_WIDTH = HG_HEADS * HG_EXPAND
HG_SCALE = 1.0 / math.sqrt(HG_EXPAND)
D_FF = 2816

LANES = 128
HEAD_PAD = LANES
ROPE_LANE0 = QK_NOPE
VMEM_LIMIT = 56 * 1024 * 1024

ROW_TILE = 512
ATT_BLOCK = 1024
ATT_SUB = 256
HG_SUB = 16
HG_TIME_TILE = 512
FF_CHUNK = 1408

_C_QLAT = 0
_C_KVLAT = _C_QLAT + Q_LORA
_C_HG = _C_KVLAT + KV_LORA
_C_GATE = _C_HG + 4 * HG_WIDTH
_C_KRA = _C_GATE + 2 * D_MODEL
_C_KRB = _C_KRA + LANES
_C_END = _C_KRB + LANES


def _const_spec(shape):
    nd = len(shape)
    return pl.BlockSpec(shape, lambda *_: (0,) * nd, pipeline_mode=pl.Buffered(1))


def _nt(a, b):
    return lax.dot_general(a, b, (((1,), (1,)), ((), ())), preferred_element_type=F32)


def _tn(a, b):
    return lax.dot_general(a, b, (((0,), (0,)), ((), ())), preferred_element_type=F32)


def _dot(a, b):
    return jnp.dot(a, b, preferred_element_type=F32)


def _rms(x, w):
    return x * lax.rsqrt(jnp.mean(x * x, axis=-1, keepdims=True) + EPS) * w


def _sigmoid(x):
    return 1.0 / (1.0 + jnp.exp(-x))


def _ada_kernel(c_ref, w_ref, b_ref, o_ref):
    c = c_ref[...]
    a = (c * _sigmoid(c)).astype(BF16)
    o_ref[...] = _dot(a, w_ref[...].astype(BF16)) + b_ref[...]


def _ada(c, w_ada, b_ada):
    n = c.shape[0]
    cols = w_ada.shape[1]
    blk = D_MODEL
    return pl.pallas_call(
        _ada_kernel,
        grid=(cols // blk,),
        in_specs=[pl.BlockSpec((n, D_MODEL), lambda j: (0, 0)),
                  pl.BlockSpec((D_MODEL, blk), lambda j: (0, j)),
                  pl.BlockSpec((1, blk), lambda j: (0, j))],
        out_specs=pl.BlockSpec((n, blk), lambda j: (0, j)),
        out_shape=jax.ShapeDtypeStruct((n, cols), F32),
        compiler_params=pltpu.CompilerParams(dimension_semantics=("arbitrary",)),
        name="ada",
    )(c, w_ada, b_ada.reshape(1, cols))


def _pre_kernel(prompt, x_ref, sh_ref, sc_ref, n1_ref, win_ref, qn_ref, wqa_ref, wqb_ref, kvn_ref,
                cq_ref, sq_ref, ck_ref, sk_ref, *rest):
    if prompt:
        wuk_ref, wuvt_ref, qt_ref, k_ref, vt_ref, ckv_ref, kr_ref, hh_ref, gg_ref = rest
    else:
        q_ref, ckv_ref, kr_ref, hh_ref, gg_ref = rest
    x = x_ref[0]
    h = _rms(x, n1_ref[...]) * (1.0 + sc_ref[0]) + sh_ref[0]
    hb = h.astype(BF16)

    def proj(a, b):
        return _dot(hb, win_ref[:, a:b])

    hh_ref[0] = proj(_C_HG, _C_GATE)
    gg_ref[0] = proj(_C_GATE, _C_KRA)
    kr_full = proj(_C_KRA, _C_KRB) * ck_ref[...] + proj(_C_KRB, _C_END) * sk_ref[...]
    kr_ref[0] = kr_full[:, ROPE_LANE0:ROPE_LANE0 + QK_ROPE]
    c_kv = _rms(proj(_C_KVLAT, _C_HG), kvn_ref[...])
    ckv_ref[0] = c_kv
    cqb = _rms(proj(_C_QLAT, _C_KVLAT), qn_ref[...]).astype(BF16)
    if prompt:
        ckvb = c_kv.astype(BF16)
        qa = _nt(wqa_ref[...], cqb)
        qb = _nt(wqb_ref[...], cqb)
        cq = cq_ref[...]
        sq = sq_ref[...]
        for hd in range(MLA_HEADS):
            sl = slice(hd * HEAD_PAD, (hd + 1) * HEAD_PAD)
            qt_ref[0, hd] = ((qa[sl] * cq + qb[sl] * sq) * MLA_SCALE).astype(BF16)
        kall = _dot(ckvb, wuk_ref[...])
        for hd in range(MLA_HEADS):
            sl = slice(hd * HEAD_PAD, (hd + 1) * HEAD_PAD)
            k_ref[0, hd] = (kall[:, sl] + kr_full).astype(BF16)
        vt = _nt(wuvt_ref[...], ckvb)
        for hd in range(MLA_HEADS):
            vt_ref[0, hd] = vt[hd * V_HEAD:(hd + 1) * V_HEAD].astype(BF16)
    else:
        qa = _dot(cqb, wqa_ref[...])
        qb = _dot(cqb, wqb_ref[...])
        cq = cq_ref[...]
        sq = sq_ref[...]
        for hd in range(MLA_HEADS):
            sl = slice(hd * HEAD_PAD, (hd + 1) * HEAD_PAD)
            q_ref[0, :, sl] = ((qa[:, sl] * cq + qb[:, sl] * sq) * MLA_SCALE).astype(BF16)


def _pre(prompt, x, sh1, sc1, wts, tabs):
    bp, tp, _ = x.shape
    r = min(ROW_TILE, tp)
    nt = tp // r
    mod_rows = sh1.shape[1]
    mod_blk = (1, 1, D_MODEL) if mod_rows == 1 else (1, r, D_MODEL)
    mod_map = (lambda b, t: (b, 0, 0)) if mod_rows == 1 else (lambda b, t: (b, t, 0))
    row3 = lambda w: pl.BlockSpec((1, r, w), lambda b, t: (b, t, 0))
    in_specs = [row3(D_MODEL), pl.BlockSpec(mod_blk, mod_map), pl.BlockSpec(mod_blk, mod_map),
                _const_spec((1, D_MODEL)), _const_spec(wts["w_in"].shape), _const_spec((1, Q_LORA)),
                _const_spec(wts["wqa"].shape), _const_spec(wts["wqb"].shape), _const_spec((1, KV_LORA))]
    if prompt:
        in_specs += [pl.BlockSpec((HEAD_PAD, r), lambda b, t: (0, t))] * 2
    else:
        in_specs += [pl.BlockSpec((r, HEAD_PAD), lambda b, t: (t, 0))] * 2
    in_specs += [pl.BlockSpec((r, HEAD_PAD), lambda b, t: (t, 0))] * 2
    args = [x, sh1, sc1, wts["norm1"], wts["w_in"], wts["q_norm"], wts["wqa"], wts["wqb"], wts["kv_norm"],
            tabs["cq"], tabs["sq"], tabs["ck"], tabs["sk"]]
    common_shapes = [jax.ShapeDtypeStruct((bp, tp, KV_LORA), F32),
                     jax.ShapeDtypeStruct((bp, tp, QK_ROPE), F32),
                     jax.ShapeDtypeStruct((bp, tp, 4 * HG_WIDTH), F32),
                     jax.ShapeDtypeStruct((bp, tp, 2 * D_MODEL), F32)]
    common_specs = [row3(KV_LORA), row3(QK_ROPE), row3(4 * HG_WIDTH), row3(2 * D_MODEL)]
    if prompt:
        in_specs += [_const_spec(wts["wuk"].shape), _const_spec(wts["wuvt"].shape)]
        args += [wts["wuk"], wts["wuvt"]]
        out_shape = [jax.ShapeDtypeStruct((bp, MLA_HEADS, HEAD_PAD, tp), BF16),
                     jax.ShapeDtypeStruct((bp, MLA_HEADS, tp, HEAD_PAD), BF16),
                     jax.ShapeDtypeStruct((bp, MLA_HEADS, V_HEAD, tp), BF16)] + common_shapes
        out_specs = [pl.BlockSpec((1, MLA_HEADS, HEAD_PAD, r), lambda b, t: (b, 0, 0, t)),
                     pl.BlockSpec((1, MLA_HEADS, r, HEAD_PAD), lambda b, t: (b, 0, t, 0)),
                     pl.BlockSpec((1, MLA_HEADS, V_HEAD, r), lambda b, t: (b, 0, 0, t))] + common_specs
    else:
        out_shape = [jax.ShapeDtypeStruct((bp, tp, MLA_HEADS * HEAD_PAD), BF16)] + common_shapes
        out_specs = [row3(MLA_HEADS * HEAD_PAD)] + common_specs
    return pl.pallas_call(
        functools.partial(_pre_kernel, prompt),
        grid=(bp, nt),
        in_specs=in_specs,
        out_specs=out_specs,
        out_shape=out_shape,
        compiler_params=pltpu.CompilerParams(dimension_semantics=("arbitrary", "arbitrary"),
                                             vmem_limit_bytes=VMEM_LIMIT),
        name="pre_prompt" if prompt else "pre_sample",
    )(*args)


def _attn_kernel(qi_ref, ki_ref, qt_ref, k_ref, vt_ref, o_ref, m_sc, l_sc, acc_sc, *, blk, sub):
    p = pl.program_id(1)
    qi = qi_ref[p]
    ki = ki_ref[p]
    ns = blk // sub

    @pl.when(ki == 0)
    def _():
        m_sc[...] = jnp.full(m_sc.shape, -jnp.inf, F32)
        l_sc[...] = jnp.zeros(l_sc.shape, F32)
        acc_sc[...] = jnp.zeros(acc_sc.shape, F32)

    def run(diag):
        if diag:
            kc = lax.broadcasted_iota(jnp.int32, (sub, sub), 0) // CHUNK
            qc = lax.broadcasted_iota(jnp.int32, (sub, sub), 1) // CHUNK
            visible = kc <= qc

        def head_body(hd, carry):
            for qs in range(ns):
                ql = slice(qs * sub, (qs + 1) * sub)
                q_t = qt_ref[0, hd, :, ql]
                m = m_sc[hd, :, ql]
                l = l_sc[hd, :, ql]
                acc = acc_sc[hd, :, ql]
                for ks in range(ns):
                    if diag and ks > qs:
                        continue
                    kl = slice(ks * sub, (ks + 1) * sub)
                    s = _dot(k_ref[0, hd, kl, :], q_t)
                    if diag and ks == qs:
                        s = jnp.where(visible, s, -jnp.inf)
                    m_new = jnp.maximum(m, jnp.max(s, axis=0, keepdims=True))
                    alpha = jnp.exp(m - m_new)
                    pm = jnp.exp(s - m_new)
                    l = alpha * l + jnp.sum(pm, axis=0, keepdims=True)
                    acc = alpha * acc + _dot(vt_ref[0, hd, :, kl], pm.astype(BF16))
                    m = m_new
                m_sc[hd, :, ql] = m
                l_sc[hd, :, ql] = l
                acc_sc[hd, :, ql] = acc
            return carry

        lax.fori_loop(0, MLA_HEADS, head_body, 0)

    @pl.when(ki < qi)
    def _():
        run(False)

    @pl.when(ki == qi)
    def _():
        run(True)
        o = acc_sc[...] / l_sc[...]
        o_ref[0] = o.reshape(MLA_WIDTH, blk).T.astype(BF16)


def _attn_prompt(qt, k, vt):
    bp, _, _, tp = qt.shape
    blk = min(ATT_BLOCK, tp)
    sub = min(ATT_SUB, blk)
    nb = tp // blk
    qi_of = np.array([q for q in range(nb) for _ in range(q + 1)], np.int32)
    ki_of = np.array([kk for q in range(nb) for kk in range(q + 1)], np.int32)
    grid_spec = pltpu.PrefetchScalarGridSpec(
        num_scalar_prefetch=2,
        grid=(bp, len(qi_of)),
        in_specs=[pl.BlockSpec((1, MLA_HEADS, HEAD_PAD, blk), lambda b, p, qi, ki: (b, 0, 0, qi[p])),
                  pl.BlockSpec((1, MLA_HEADS, blk, HEAD_PAD), lambda b, p, qi, ki: (b, 0, ki[p], 0)),
                  pl.BlockSpec((1, MLA_HEADS, V_HEAD, blk), lambda b, p, qi, ki: (b, 0, 0, ki[p]))],
        out_specs=pl.BlockSpec((1, blk, MLA_WIDTH), lambda b, p, qi, ki: (b, qi[p], 0)),
        scratch_shapes=[pltpu.VMEM((MLA_HEADS, 1, blk), F32),
                        pltpu.VMEM((MLA_HEADS, 1, blk), F32),
                        pltpu.VMEM((MLA_HEADS, V_HEAD, blk), F32)],
    )
    return pl.pallas_call(
        functools.partial(_attn_kernel, blk=blk, sub=sub),
        grid_spec=grid_spec,
        out_shape=jax.ShapeDtypeStruct((bp, tp, MLA_WIDTH), BF16),
        compiler_params=pltpu.CompilerParams(dimension_semantics=("arbitrary", "arbitrary"),
                                             vmem_limit_bytes=VMEM_LIMIT),
        name="attn_prompt",
    )(jnp.asarray(qi_of), jnp.asarray(ki_of), qt, k, vt)


def _attn_sample_kernel(q_ref, cc_ref, ckr_ref, nc_ref, nkr_ref, wuk_ref, wuvp_ref, place_ref, o_ref):
    t = q_ref.shape[1]
    q = q_ref[0]
    qh = [q[:, hd * HEAD_PAD:(hd + 1) * HEAD_PAD] for hd in range(MLA_HEADS)]
    q_stack = jnp.concatenate(qh, axis=0)
    q_abs = jnp.concatenate(
        [_nt(qh[hd][:, :QK_NOPE], wuk_ref[hd]) for hd in range(MLA_HEADS)], axis=0).astype(BF16)
    place = place_ref[...]
    cc = cc_ref[0, 0].astype(BF16)
    nc = nc_ref[0].astype(BF16)
    ckr = _dot(ckr_ref[0, 0].astype(BF16), place).astype(BF16)
    nkr = _dot(nkr_ref[0].astype(BF16), place).astype(BF16)
    s_c = _nt(q_abs, cc) + _nt(q_stack, ckr)
    s_n = _nt(q_abs, nc) + _nt(q_stack, nkr)
    m = jnp.maximum(jnp.max(s_c, axis=-1, keepdims=True), jnp.max(s_n, axis=-1, keepdims=True))
    p_c = jnp.exp(s_c - m)
    p_n = jnp.exp(s_n - m)
    l = jnp.sum(p_c, axis=-1, keepdims=True) + jnp.sum(p_n, axis=-1, keepdims=True)
    o_lat = (_dot(p_c.astype(BF16), cc) + _dot(p_n.astype(BF16), nc)) / l
    o_lat = o_lat.astype(BF16)
    out = jnp.zeros((t, MLA_WIDTH), F32)
    for hd in range(MLA_HEADS):
        out = out + _dot(o_lat[hd * t:(hd + 1) * t], wuvp_ref[hd])
    o_ref[0] = out.astype(BF16)


def _attn_sample(q, cache_ckv, cache_kr, new_ckv, new_kr, wts):
    nb, t, _ = q.shape
    past = cache_ckv.shape[2]
    place = np.zeros((QK_ROPE, HEAD_PAD), np.float32)
    place[np.arange(QK_ROPE), ROPE_LANE0 + np.arange(QK_ROPE)] = 1.0
    return pl.pallas_call(
        _attn_sample_kernel,
        grid=(nb,),
        in_specs=[pl.BlockSpec((1, t, MLA_HEADS * HEAD_PAD), lambda b: (b, 0, 0)),
                  pl.BlockSpec((1, 1, past, KV_LORA), lambda b: (0, b, 0, 0)),
                  pl.BlockSpec((1, 1, past, QK_ROPE), lambda b: (0, b, 0, 0)),
                  pl.BlockSpec((1, t, KV_LORA), lambda b: (b, 0, 0)),
                  pl.BlockSpec((1, t, QK_ROPE), lambda b: (b, 0, 0)),
                  _const_spec(wts["wuk3"].shape), _const_spec(wts["wuvp"].shape),
                  _const_spec((QK_ROPE, HEAD_PAD))],
        out_specs=pl.BlockSpec((1, t, MLA_WIDTH), lambda b: (b, 0, 0)),
        out_shape=jax.ShapeDtypeStruct((nb, t, MLA_WIDTH), BF16),
        compiler_params=pltpu.CompilerParams(dimension_semantics=("arbitrary",),
                                             vmem_limit_bytes=VMEM_LIMIT),
        name="attn_sample",
    )(q, cache_ckv, cache_kr, new_ckv, new_kr, wts["wuk3"], wts["wuvp"], jnp.asarray(place, BF16))


def _split3(x):
    hi = x.astype(BF16)
    r1 = x - hi.astype(F32)
    mid = r1.astype(BF16)
    lo = (r1 - mid.astype(F32)).astype(BF16)
    return hi, mid, lo


def _hgrn_kernel(hh_ref, lb_ref, gn_ref, s0_ref, o_ref, sout_ref, st_sc, *, L, nblk):
    tstep = pl.program_id(1)

    @pl.when(tstep == 0)
    def _():
        for hd in range(HG_HEADS):
            st_sc[hd] = s0_ref[0, hd].T

    row = lax.broadcasted_iota(jnp.int32, (L, 1), 0)
    rmod = row % HG_SUB
    tril = (lax.broadcasted_iota(jnp.int32, (L, L), 0) >= lax.broadcasted_iota(jnp.int32, (L, L), 1)
            ).astype(BF16)
    nsub = L // HG_SUB
    neg_inf = -jnp.inf

    def blk_body(j, carry):
        r0 = pl.multiple_of(j * L, L)
        rows = pl.ds(r0, L)
        for hd in range(HG_HEADS):
            c0 = hd * HG_EXPAND
            hq = hh_ref[0, rows, c0:c0 + HG_EXPAND]
            hf = hh_ref[0, rows, HG_WIDTH + c0:HG_WIDTH + c0 + HG_EXPAND]
            v = hh_ref[0, rows, 2 * HG_WIDTH + c0:2 * HG_WIDTH + c0 + HG_VDIM]
            hg = hh_ref[0, rows, 3 * HG_WIDTH + c0:3 * HG_WIDTH + c0 + HG_VDIM]
            lb = lb_ref[:, c0:c0 + HG_EXPAND]
            g = jnp.log(lb + (1.0 - lb) * _sigmoid(hf))
            kk = (1.0 - lb) * _sigmoid(-hf)
            qq = hq * _sigmoid(hq) * HG_SCALE
            g_hi, g_mid, g_lo = _split3(g)
            bc = _dot(tril, g_hi) + _dot(tril, g_mid) + _dot(tril, g_lo)
            vb = v.astype(BF16)
            st = st_sc[hd]
            o = _nt((qq * jnp.exp(bc)).astype(BF16), st.astype(BF16))
            if nsub > 1:
                qs_, ks_ = [], []
                for i in range(1, nsub):
                    b_i = bc[HG_SUB * i - 1:HG_SUB * i, :]
                    in_i = (row >= HG_SUB * i) & (row < HG_SUB * (i + 1))
                    qs_.append(qq * jnp.exp(jnp.where(in_i, bc - b_i, neg_inf)))
                    ks_.append(kk * jnp.exp(jnp.where(row < HG_SUB * i, b_i - bc, neg_inf)))
                att = _nt(jnp.concatenate(qs_, axis=1).astype(BF16),
                          jnp.concatenate(ks_, axis=1).astype(BF16))
                o = o + _dot(att.astype(BF16), vb)
            od = jnp.sum(qq * kk, axis=1, keepdims=True) * v
            for d in range(1, HG_SUB):
                kd = pltpu.roll(kk, d, 0)
                bd = pltpu.roll(bc, d, 0)
                vd = pltpu.roll(v, d, 0)
                e = jnp.exp(jnp.where(rmod >= d, bc - bd, neg_inf))
                od = od + jnp.sum(qq * kd * e, axis=1, keepdims=True) * vd
            o = o + od
            last = bc[L - 1:L, :]
            kdec = (kk * jnp.exp(last - bc)).astype(BF16)
            st_sc[hd] = st * jnp.exp(last) + _tn(vb, kdec)
            o = _rms(o, gn_ref[...]) * (hg * _sigmoid(hg))
            o_ref[0, rows, c0:c0 + HG_VDIM] = o
        return carry

    lax.fori_loop(0, nblk, blk_body, 0)

    @pl.when(tstep == pl.num_programs(1) - 1)
    def _():
        for hd in range(HG_HEADS):
            sout_ref[0, hd] = st_sc[hd].T


def _hgrn(hh, lb, hg_norm, s0, L):
    bp, tp, _ = hh.shape
    tt = min(HG_TIME_TILE, tp)
    return pl.pallas_call(
        functools.partial(_hgrn_kernel, L=L, nblk=tt // L),
        grid=(bp, tp // tt),
        in_specs=[pl.BlockSpec((1, tt, 4 * HG_WIDTH), lambda b, t: (b, t, 0)),
                  _const_spec((1, HG_WIDTH)), _const_spec((1, HG_VDIM)),
                  pl.BlockSpec((1, HG_HEADS, HG_EXPAND, HG_VDIM), lambda b, t: (b, 0, 0, 0))],
        out_specs=[pl.BlockSpec((1, tt, HG_WIDTH), lambda b, t: (b, t, 0)),
                   pl.BlockSpec((1, HG_HEADS, HG_EXPAND, HG_VDIM), lambda b, t: (b, 0, 0, 0))],
        out_shape=[jax.ShapeDtypeStruct((bp, tp, HG_WIDTH), F32),
                   jax.ShapeDtypeStruct((bp, HG_HEADS, HG_EXPAND, HG_VDIM), F32)],
        scratch_shapes=[pltpu.VMEM((HG_HEADS, HG_VDIM, HG_EXPAND), F32)],
        compiler_params=pltpu.CompilerParams(dimension_semantics=("arbitrary", "arbitrary"),
                                             vmem_limit_bytes=VMEM_LIMIT),
        name="hgrn",
    )(hh, lb, hg_norm, s0)


def _post_kernel(x_ref, at_ref, ho_ref, gg_ref, g1_ref, sh2_ref, sc2_ref, g2_ref,
                 wpa_ref, wpb_ref, wout_ref, n2_ref, wgu_ref, wdown_ref, fn_ref, y_ref):
    x = x_ref[0]
    ya = _dot(at_ref[0], wpa_ref[...])
    yb = _dot(ho_ref[0].astype(BF16), wpb_ref[...])
    mixed = _sigmoid(gg_ref[0, :, :D_MODEL]) * ya + _sigmoid(gg_ref[0, :, D_MODEL:]) * yb
    x1 = x + g1_ref[0] * _dot(mixed.astype(BF16), wout_ref[...])
    hb = (_rms(x1, n2_ref[...]) * (1.0 + sc2_ref[0]) + sh2_ref[0]).astype(BF16)
    ff = jnp.zeros(x.shape, F32)
    for c in range(D_FF // FF_CHUNK):
        gate = _dot(hb, wgu_ref[:, c * FF_CHUNK:(c + 1) * FF_CHUNK])
        up = _dot(hb, wgu_ref[:, D_FF + c * FF_CHUNK:D_FF + (c + 1) * FF_CHUNK])
        act = (gate * _sigmoid(gate) * up).astype(BF16)
        ff = ff + _dot(act, wdown_ref[c * FF_CHUNK:(c + 1) * FF_CHUNK, :])
    x2 = x1 + g2_ref[0] * ff
    y_ref[0] = _rms(x2, fn_ref[...])


def _post(x, attn, ho, gg, g1, sh2, sc2, g2, wts):
    bp, tp, _ = x.shape
    r = min(ROW_TILE, tp)
    mod_rows = g1.shape[1]
    mod_blk = (1, 1, D_MODEL) if mod_rows == 1 else (1, r, D_MODEL)
    mod_map = (lambda b, t: (b, 0, 0)) if mod_rows == 1 else (lambda b, t: (b, t, 0))
    row3 = lambda w: pl.BlockSpec((1, r, w), lambda b, t: (b, t, 0))
    mod = pl.BlockSpec(mod_blk, mod_map)
    names = ["wpa", "wpb", "wout", "norm2", "wgu", "wdown", "final_norm"]
    return pl.pallas_call(
        _post_kernel,
        grid=(bp, tp // r),
        in_specs=[row3(D_MODEL), row3(MLA_WIDTH), row3(HG_WIDTH), row3(2 * D_MODEL), mod, mod, mod, mod]
                 + [_const_spec(wts[n].shape) for n in names],
        out_specs=row3(D_MODEL),
        out_shape=jax.ShapeDtypeStruct((bp, tp, D_MODEL), F32),
        compiler_params=pltpu.CompilerParams(dimension_semantics=("arbitrary", "arbitrary"),
                                             vmem_limit_bytes=VMEM_LIMIT),
        name="post",
    )(x, attn, ho, gg, g1, sh2, sc2, g2, *[wts[n] for n in names])


def _prep_weights(w_in, q_norm, w_uq, kv_norm, w_ukv, lb_param, hg_norm, w_pa, w_pb, w_out,
                  norm1, norm2, w_gu, w_down, final_norm):
    half = QK_ROPE // 2
    offs = np.cumsum([0, Q_LORA, KV_LORA, QK_ROPE, HG_WIDTH, HG_WIDTH, HG_WIDTH, HG_WIDTH, D_MODEL, D_MODEL])
    w = w_in[0]
    w_kr = w[:, offs[2]:offs[3]]
    zeros = lambda n: jnp.zeros((D_MODEL, n), w.dtype)
    kr_a = jnp.concatenate([zeros(ROPE_LANE0), w_kr, zeros(LANES - ROPE_LANE0 - QK_ROPE)], axis=1)
    kr_b = jnp.concatenate([zeros(ROPE_LANE0), w_kr[:, half:], w_kr[:, :half],
                            zeros(LANES - ROPE_LANE0 - QK_ROPE)], axis=1)
    w_in_p = jnp.concatenate([w[:, offs[0]:offs[2]], w[:, offs[3]:offs[9]], kr_a, kr_b], axis=1).astype(BF16)

    uq = w_uq[0].reshape(Q_LORA, MLA_HEADS, QK_NOPE + QK_ROPE)
    pad = jnp.zeros((Q_LORA, MLA_HEADS, HEAD_PAD - QK_NOPE - QK_ROPE), uq.dtype)
    wqa = jnp.concatenate([uq, pad], axis=2).reshape(Q_LORA, MLA_HEADS * HEAD_PAD).astype(BF16)
    wqb = jnp.concatenate([jnp.zeros((Q_LORA, MLA_HEADS, QK_NOPE), uq.dtype),
                           uq[:, :, QK_NOPE + half:], uq[:, :, QK_NOPE:QK_NOPE + half], pad],
                          axis=2).reshape(Q_LORA, MLA_HEADS * HEAD_PAD).astype(BF16)

    ukv = w_ukv[0].reshape(KV_LORA, MLA_HEADS, QK_NOPE + V_HEAD)
    uk = ukv[:, :, :QK_NOPE]
    uv = ukv[:, :, QK_NOPE:]
    wuk = jnp.concatenate([uk, jnp.zeros((KV_LORA, MLA_HEADS, HEAD_PAD - QK_NOPE), uk.dtype)],
                          axis=2).reshape(KV_LORA, MLA_HEADS * HEAD_PAD).astype(BF16)
    wuvt = uv.reshape(KV_LORA, MLA_WIDTH).T.astype(BF16)
    wuk3 = jnp.transpose(uk, (1, 0, 2)).astype(BF16)
    eye = jnp.eye(MLA_HEADS, dtype=uv.dtype)
    wuvp = (jnp.transpose(uv, (1, 0, 2))[:, :, None, :] * eye[:, None, :, None]
            ).reshape(MLA_HEADS, KV_LORA, MLA_WIDTH).astype(BF16)

    lb = jnp.cumsum(jax.nn.softmax(lb_param.astype(F32), axis=0), axis=0)[0].reshape(1, HG_WIDTH)
    return {
        "w_in": w_in_p, "wqa": wqa, "wqb": wqb, "wqa_t": wqa.T, "wqb_t": wqb.T,
        "wuk": wuk, "wuvt": wuvt, "wuk3": wuk3, "wuvp": wuvp, "lb": lb,
        "q_norm": q_norm[0].reshape(1, Q_LORA), "kv_norm": kv_norm[0].reshape(1, KV_LORA),
        "hg_norm": hg_norm[0].reshape(1, HG_VDIM),
        "norm1": norm1[0].reshape(1, D_MODEL), "norm2": norm2[0].reshape(1, D_MODEL),
        "final_norm": final_norm.reshape(1, D_MODEL),
        "wpa": w_pa[0].astype(BF16), "wpb": w_pb[0].astype(BF16), "wout": w_out[0].astype(BF16),
        "wgu": w_gu[0].astype(BF16), "wdown": w_down[0].astype(BF16),
    }


def _rope_tables(pos):
    half = QK_ROPE // 2
    inv = ROPE_THETA ** (-np.arange(half, dtype=np.float64) / half)
    ang = np.asarray(pos, np.float64)[:, None] * inv[None, :]
    cos, sin = np.cos(ang), np.sin(ang)
    t = len(pos)
    ck = np.zeros((t, HEAD_PAD))
    sk = np.zeros((t, HEAD_PAD))
    ck[:, ROPE_LANE0:ROPE_LANE0 + QK_ROPE] = np.concatenate([cos, cos], axis=1)
    sk[:, ROPE_LANE0:ROPE_LANE0 + QK_ROPE] = np.concatenate([-sin, sin], axis=1)
    cq = ck.copy()
    cq[:, :QK_NOPE] = 1.0
    f = lambda a: jnp.asarray(a.astype(np.float32))
    return {"ck": f(ck), "sk": f(sk), "cq": f(cq), "sq": f(sk)}


def _trunk(prompt, x, mod, wts, tabs, hg_block, s0, cache=None):
    sh1, sc1, g1, sh2, sc2, g2 = mod
    if prompt:
        w = dict(wts, wqa=wts["wqa_t"], wqb=wts["wqb_t"])
        t = dict(tabs, cq=tabs["cq"].T, sq=tabs["sq"].T)
        qt, k, vt, ckv, kr, hh, gg = _pre(True, x, sh1, sc1, w, t)
        attn = _attn_prompt(qt, k, vt)
    else:
        q, ckv, kr, hh, gg = _pre(False, x, sh1, sc1, wts, tabs)
        nb, t_new = cache[0].shape[1], x.shape[1] // cache[0].shape[1]
        attn = _attn_sample(q.reshape(nb, t_new, -1), cache[0], cache[1],
                            ckv.reshape(nb, t_new, KV_LORA), kr.reshape(nb, t_new, QK_ROPE), wts)
        attn = attn.reshape(1, nb * t_new, MLA_WIDTH)
        hh = hh.reshape(nb, t_new, 4 * HG_WIDTH)
    ho, s_new = _hgrn(hh, wts["lb"], wts["hg_norm"], s0, hg_block)
    ho = ho.reshape(x.shape[0], x.shape[1], HG_WIDTH)
    y = _post(x, attn, ho, gg, g1, sh2, sc2, g2, wts)
    return y, ckv, kr, s_new


def kernel(x_prompt, x_sample, cache_ckv, cache_krope, state_hgrn, c_prompt, c_sample, w_in, q_norm, w_uq,
           kv_norm, w_ukv, lb_param, hg_norm, w_pa, w_pb, w_out, norm1, norm2, w_ada, b_ada, w_gu, w_down,
           final_norm):
    bsz, t_p, _ = x_prompt.shape
    nb, t_s, _ = x_sample.shape
    past = cache_ckv.shape[2]
    assert past % CHUNK == 0 and t_s <= CHUNK
    wts = _prep_weights(w_in, q_norm, w_uq, kv_norm, w_ukv, lb_param, hg_norm, w_pa, w_pb, w_out,
                        norm1, norm2, w_gu, w_down, final_norm)

    n_c = bsz + nb
    n_pad = -n_c % 8
    c_all = jnp.concatenate([c_prompt, c_sample, jnp.zeros((n_pad, D_MODEL), F32)], axis=0)
    mod_all = _ada(c_all, w_ada[0], b_ada[0])
    mod_p = [m[:, None, :] for m in jnp.split(mod_all[:bsz], 6, axis=-1)]
    mod_s = [jnp.repeat(m, t_s, axis=0)[None] for m in jnp.split(mod_all[bsz:n_c], 6, axis=-1)]

    tabs_p = _rope_tables(np.arange(t_p))
    tabs_s = _rope_tables(np.tile(past + np.arange(t_s), nb))

    s0_p = jnp.zeros((bsz, HG_HEADS, HG_EXPAND, HG_VDIM), F32)
    y_p, ckv_p, kr_p, s_p = _trunk(True, x_prompt, mod_p, wts, tabs_p, CHUNK, s0_p)
    y_s, ckv_s, kr_s, s_s = _trunk(False, x_sample.reshape(1, nb * t_s, D_MODEL), mod_s, wts, tabs_s, t_s,
                                   state_hgrn[0], cache=(cache_ckv, cache_krope))
    return (y_p, y_s.reshape(nb, t_s, D_MODEL), ckv_p[None], kr_p[None], s_p[None],
            ckv_s.reshape(nb, t_s, KV_LORA)[None], kr_s.reshape(nb, t_s, QK_ROPE)[None], s_s[None])
```

```python
import functools
import math

import numpy as np
import jax
import jax.numpy as jnp
from jax import lax
from jax.experimental import pallas as pl
from jax.experimental.pallas import tpu as pltpu

F32 = jnp.float32
BF16 = jnp.bfloat16

D_MODEL = 1024
CHUNK = 64
EPS = 1e-6
MLA_HEADS = 8
QK_NOPE = 64
QK_ROPE = 32
V_HEAD = 64
Q_LORA = 384
KV_LORA = 256
ROPE_THETA = 10000.0
MLA_WIDTH = MLA_HEADS * V_HEAD
MLA_SCALE = 1.0 / math.sqrt(QK_NOPE + QK_ROPE)
HG_HEADS = 4
HG_EXPAND = 128
HG_VDIM = 128
HG_WIDTH = HG_HEADS * HG_EXPAND
HG_SCALE = 1.0 / math.sqrt(HG_EXPAND)
D_FF = 2816
LOG2E = math.log2(math.e)

LANES = 128
HEAD_PAD = LANES
ROPE_LANE0 = QK_NOPE
VMEM_LIMIT = 56 * 1024 * 1024

ROW_TILE = 512
ATT_BLOCK = 1024
ATT_SUB = 256
HG_SUB = 16
HG_TIME_TILE = 512
FF_CHUNK = 1408

_C_QLAT = 0
_C_KVLAT = _C_QLAT + Q_LORA
_C_HG = _C_KVLAT + KV_LORA
_C_GATE = _C_HG + 4 * HG_WIDTH
_C_KRA = _C_GATE + 2 * D_MODEL
_C_KRB = _C_KRA + LANES
_C_END = _C_KRB + LANES


def _const_spec(shape):
    nd = len(shape)
    return pl.BlockSpec(shape, lambda *_: (0,) * nd, pipeline_mode=pl.Buffered(1))


def _nt(a, b):
    return lax.dot_general(a, b, (((1,), (1,)), ((), ())), preferred_element_type=F32)


def _tn(a, b):
    return lax.dot_general(a, b, (((0,), (0,)), ((), ())), preferred_element_type=F32)


def _dot(a, b):
    return jnp.dot(a, b, preferred_element_type=F32)


def _rms(x, w):
    return x * lax.rsqrt(jnp.mean(x * x, axis=-1, keepdims=True) + EPS) * w


def _sigmoid(x):
    return 1.0 / (1.0 + jnp.exp(-x))


def _ada_kernel(c_ref, w_ref, b_ref, o_ref):
    c = c_ref[...]
    a = (c * _sigmoid(c)).astype(BF16)
    o_ref[...] = _dot(a, w_ref[...].astype(BF16)) + b_ref[...]


def _ada(c, w_ada, b_ada):
    n = c.shape[0]
    cols = w_ada.shape[1]
    blk = D_MODEL
    return pl.pallas_call(
        _ada_kernel,
        grid=(cols // blk,),
        in_specs=[pl.BlockSpec((n, D_MODEL), lambda j: (0, 0)),
                  pl.BlockSpec((D_MODEL, blk), lambda j: (0, j)),
                  pl.BlockSpec((1, blk), lambda j: (0, j))],
        out_specs=pl.BlockSpec((n, blk), lambda j: (0, j)),
        out_shape=jax.ShapeDtypeStruct((n, cols), F32),
        compiler_params=pltpu.CompilerParams(dimension_semantics=("arbitrary",)),
        name="ada",
    )(c, w_ada, b_ada.reshape(1, cols))


def _pre_kernel(prompt, x_ref, sh_ref, sc_ref, n1_ref, win_ref, qn_ref, wqa_ref, wqb_ref, kvn_ref,
                cq_ref, sq_ref, ck_ref, sk_ref, *rest):
    if prompt:
        wuk_ref, wuvt_ref, qt_ref, k_ref, vt_ref, ckv_ref, kr_ref, hh_ref, gg_ref = rest
    else:
        q_ref, ckv_ref, kr_ref, hh_ref, gg_ref = rest
    x = x_ref[0]
    h = _rms(x, n1_ref[...]) * (1.0 + sc_ref[0]) + sh_ref[0]
    hb = h.astype(BF16)

    def proj(a, b):
        return _dot(hb, win_ref[:, a:b])

    hh_ref[0] = proj(_C_HG, _C_GATE)
    gg_ref[0] = proj(_C_GATE, _C_KRA)
    kr_full = proj(_C_KRA, _C_KRB) * ck_ref[...] + proj(_C_KRB, _C_END) * sk_ref[...]
    kr_ref[0] = kr_full[:, ROPE_LANE0:ROPE_LANE0 + QK_ROPE]
    c_kv = _rms(proj(_C_KVLAT, _C_HG), kvn_ref[...])
    ckv_ref[0] = c_kv
    cqb = _rms(proj(_C_QLAT, _C_KVLAT), qn_ref[...]).astype(BF16)
    if prompt:
        ckvb = c_kv.astype(BF16)
        qa = _nt(wqa_ref[...], cqb)
        qb = _nt(wqb_ref[...], cqb)
        cq = cq_ref[...]
        sq = sq_ref[...]
        for hd in range(MLA_HEADS):
            sl = slice(hd * HEAD_PAD, (hd + 1) * HEAD_PAD)
            qt_ref[0, hd] = ((qa[sl] * cq + qb[sl] * sq) * (MLA_SCALE * LOG2E)).astype(BF16)
        kall = _dot(ckvb, wuk_ref[...])
        for hd in range(MLA_HEADS):
            sl = slice(hd * HEAD_PAD, (hd + 1) * HEAD_PAD)
            k_ref[0, hd] = (kall[:, sl] + kr_full).astype(BF16)
        vt = _nt(wuvt_ref[...], ckvb)
        for hd in range(MLA_HEADS):
            vt_ref[0, hd] = vt[hd * V_HEAD:(hd + 1) * V_HEAD].astype(BF16)
    else:
        qa = _dot(cqb, wqa_ref[...])
        qb = _dot(cqb, wqb_ref[...])
        cq = cq_ref[...]
        sq = sq_ref[...]
        for hd in range(MLA_HEADS):
            sl = slice(hd * HEAD_PAD, (hd + 1) * HEAD_PAD)
            q_ref[0, :, sl] = ((qa[:, sl] * cq + qb[:, sl] * sq) * MLA_SCALE).astype(BF16)


def _pre(prompt, x, sh1, sc1, wts, tabs):
    bp, tp, _ = x.shape
    r = min(ROW_TILE, tp)
    nt = tp // r
    mod_rows = sh1.shape[1]
    mod_blk = (1, 1, D_MODEL) if mod_rows == 1 else (1, r, D_MODEL)
    mod_map = (lambda b, t: (b, 0, 0)) if mod_rows == 1 else (lambda b, t: (b, t, 0))
    row3 = lambda w: pl.BlockSpec((1, r, w), lambda b, t: (b, t, 0))
    in_specs = [row3(D_MODEL), pl.BlockSpec(mod_blk, mod_map), pl.BlockSpec(mod_blk, mod_map),
                _const_spec((1, D_MODEL)), _const_spec(wts["w_in"].shape), _const_spec((1, Q_LORA)),
                _const_spec(wts["wqa"].shape), _const_spec(wts["wqb"].shape), _const_spec((1, KV_LORA))]
    if prompt:
        in_specs += [pl.BlockSpec((HEAD_PAD, r), lambda b, t: (0, t))] * 2
    else:
        in_specs += [pl.BlockSpec((r, HEAD_PAD), lambda b, t: (t, 0))] * 2
    in_specs += [pl.BlockSpec((r, HEAD_PAD), lambda b, t: (t, 0))] * 2
    args = [x, sh1, sc1, wts["norm1"], wts["w_in"], wts["q_norm"], wts["wqa"], wts["wqb"], wts["kv_norm"],
            tabs["cq"], tabs["sq"], tabs["ck"], tabs["sk"]]
    common_shapes = [jax.ShapeDtypeStruct((bp, tp, KV_LORA), F32),
                     jax.ShapeDtypeStruct((bp, tp, QK_ROPE), F32),
                     jax.ShapeDtypeStruct((bp, tp, 4 * HG_WIDTH), F32),
                     jax.ShapeDtypeStruct((bp, tp, 2 * D_MODEL), F32)]
    common_specs = [row3(KV_LORA), row3(QK_ROPE), row3(4 * HG_WIDTH), row3(2 * D_MODEL)]
    if prompt:
        in_specs += [_const_spec(wts["wuk"].shape), _const_spec(wts["wuvt"].shape)]
        args += [wts["wuk"], wts["wuvt"]]
        out_shape = [jax.ShapeDtypeStruct((bp, MLA_HEADS, HEAD_PAD, tp), BF16),
                     jax.ShapeDtypeStruct((bp, MLA_HEADS, tp, HEAD_PAD), BF16),
                     jax.ShapeDtypeStruct((bp, MLA_HEADS, V_HEAD, tp), BF16)] + common_shapes
        out_specs = [pl.BlockSpec((1, MLA_HEADS, HEAD_PAD, r), lambda b, t: (b, 0, 0, t)),
                     pl.BlockSpec((1, MLA_HEADS, r, HEAD_PAD), lambda b, t: (b, 0, t, 0)),
                     pl.BlockSpec((1, MLA_HEADS, V_HEAD, r), lambda b, t: (b, 0, 0, t))] + common_specs
    else:
        out_shape = [jax.ShapeDtypeStruct((bp, tp, MLA_HEADS * HEAD_PAD), BF16)] + common_shapes
        out_specs = [row3(MLA_HEADS * HEAD_PAD)] + common_specs
    return pl.pallas_call(
        functools.partial(_pre_kernel, prompt),
        grid=(bp, nt),
        in_specs=in_specs,
        out_specs=out_specs,
        out_shape=out_shape,
        compiler_params=pltpu.CompilerParams(dimension_semantics=("arbitrary", "arbitrary"),
                                             vmem_limit_bytes=VMEM_LIMIT),
        name="pre_prompt" if prompt else "pre_sample",
    )(*args)


def _attn_kernel(qi_ref, ki_ref, qt_ref, k_ref, vt_ref, o_ref, m_sc, l_sc, acc_sc, sa_sc, sb_sc, sta_sc, stb_sc,
                 *, blk, sub):
    p = pl.program_id(1)
    qi = qi_ref[p]
    ki = ki_ref[p]
    ns = blk // sub

    @pl.when(ki == 0)
    def _():
        m_sc[...] = jnp.full(m_sc.shape, -jnp.inf, F32)
        l_sc[...] = jnp.zeros(l_sc.shape, F32)
        acc_sc[...] = jnp.zeros(acc_sc.shape, F32)

    def run(diag):
        if diag:
            kc = lax.broadcasted_iota(jnp.int32, (sub, sub), 0) // CHUNK
            qc = lax.broadcasted_iota(jnp.int32, (sub, sub), 1) // CHUNK
            visible = kc <= qc

        def rows(qs):
            return (qs + 1) * sub if diag else blk

        def scores(hd, s_buf, st_buf):
            for qs in range(ns):
                ql = slice(qs * sub, (qs + 1) * sub)
                nrow = rows(qs)
                s = _dot(k_ref[0, hd, :nrow, :], qt_ref[0, hd, :, ql])
                if diag:
                    lo = qs * sub
                    dg = jnp.where(visible, s[lo:], -jnp.inf)
                    s_buf[lo:nrow, ql] = dg
                    mx = jnp.max(dg, axis=0, keepdims=True)
                    if qs > 0:
                        s_buf[:lo, ql] = s[:lo]
                        mx = jnp.maximum(mx, jnp.max(s[:lo], axis=0, keepdims=True))
                else:
                    s_buf[:, ql] = s
                    mx = jnp.max(s, axis=0, keepdims=True)
                m_old = m_sc[hd, :, ql]
                m_new = jnp.maximum(m_old, mx)
                m_sc[hd, :, ql] = m_new
                st_buf[0, :, ql] = m_new
                st_buf[1, :, ql] = jnp.exp2(m_old - m_new)

        def weighted(hd, s_buf, st_buf):
            for qs in range(ns):
                ql = slice(qs * sub, (qs + 1) * sub)
                nrow = rows(qs)
                alpha = st_buf[1, :, ql]
                pm = jnp.exp2(s_buf[:nrow, ql] - st_buf[0, :, ql])
                l_sc[hd, :, ql] = alpha * l_sc[hd, :, ql] + pm.reshape(nrow // 8, 8, sub).sum(axis=0)
                acc_sc[hd, :, ql] = alpha * acc_sc[hd, :, ql] + _dot(vt_ref[0, hd, :, :nrow], pm.astype(BF16))

        scores(0, sa_sc, sta_sc)

        def pair_body(j, carry):
            h0 = 2 * j
            scores(h0 + 1, sb_sc, stb_sc)
            weighted(h0, sa_sc, sta_sc)
            scores(h0 + 2, sa_sc, sta_sc)
            weighted(h0 + 1, sb_sc, stb_sc)
            return carry

        lax.fori_loop(0, MLA_HEADS // 2 - 1, pair_body, 0)
        scores(MLA_HEADS - 1, sb_sc, stb_sc)
        weighted(MLA_HEADS - 2, sa_sc, sta_sc)
        weighted(MLA_HEADS - 1, sb_sc, stb_sc)

    @pl.when(ki < qi)
    def _():
        run(False)

    @pl.when(ki == qi)
    def _():
        run(True)
        o = acc_sc[...] / jnp.sum(l_sc[...], axis=1, keepdims=True)
        o_ref[0] = o.reshape(MLA_WIDTH, blk).T.astype(BF16)


def _attn_prompt(qt, k, vt):
    bp, _, _, tp = qt.shape
    blk = min(ATT_BLOCK, tp)
    sub = min(ATT_SUB, blk)
    nb = tp // blk
    qi_of = np.array([q for q in range(nb) for _ in range(q + 1)], np.int32)
    ki_of = np.array([kk for q in range(nb) for kk in range(q + 1)], np.int32)
    grid_spec = pltpu.PrefetchScalarGridSpec(
        num_scalar_prefetch=2,
        grid=(bp, len(qi_of)),
        in_specs=[pl.BlockSpec((1, MLA_HEADS, HEAD_PAD, blk), lambda b, p, qi, ki: (b, 0, 0, qi[p])),
                  pl.BlockSpec((1, MLA_HEADS, blk, HEAD_PAD), lambda b, p, qi, ki: (b, 0, ki[p], 0)),
                  pl.BlockSpec((1, MLA_HEADS, V_HEAD, blk), lambda b, p, qi, ki: (b, 0, 0, ki[p]))],
        out_specs=pl.BlockSpec((1, blk, MLA_WIDTH), lambda b, p, qi, ki: (b, qi[p], 0)),
        scratch_shapes=[pltpu.VMEM((MLA_HEADS, 1, blk), F32),
                        pltpu.VMEM((MLA_HEADS, 8, blk), F32),
                        pltpu.VMEM((MLA_HEADS, V_HEAD, blk), F32),
                        pltpu.VMEM((blk, blk), F32),
                        pltpu.VMEM((blk, blk), F32),
                        pltpu.VMEM((2, 1, blk), F32),
                        pltpu.VMEM((2, 1, blk), F32)],
    )
    return pl.pallas_call(
        functools.partial(_attn_kernel, blk=blk, sub=sub),
        grid_spec=grid_spec,
        out_shape=jax.ShapeDtypeStruct((bp, tp, MLA_WIDTH), BF16),
        compiler_params=pltpu.CompilerParams(dimension_semantics=("arbitrary", "arbitrary"),
                                             vmem_limit_bytes=VMEM_LIMIT),
        name="attn_prompt",
    )(jnp.asarray(qi_of), jnp.asarray(ki_of), qt, k, vt)


def _attn_sample_kernel(q_ref, cc_ref, ckr_ref, nc_ref, nkr_ref, wuk_ref, wuvp_ref, place_ref, o_ref):
    t = q_ref.shape[1]
    q = q_ref[0]
    qh = [q[:, hd * HEAD_PAD:(hd + 1) * HEAD_PAD] for hd in range(MLA_HEADS)]
    q_stack = jnp.concatenate(qh, axis=0)
    q_abs = jnp.concatenate(
        [_nt(qh[hd][:, :QK_NOPE], wuk_ref[hd]) for hd in range(MLA_HEADS)], axis=0).astype(BF16)
    place = place_ref[...]
    cc = cc_ref[0, 0].astype(BF16)
    nc = nc_ref[0].astype(BF16)
    ckr = _dot(ckr_ref[0, 0].astype(BF16), place).astype(BF16)
    nkr = _dot(nkr_ref[0].astype(BF16), place).astype(BF16)
    s_c = _nt(q_abs, cc) + _nt(q_stack, ckr)
    s_n = _nt(q_abs, nc) + _nt(q_stack, nkr)
    m = jnp.maximum(jnp.max(s_c, axis=-1, keepdims=True), jnp.max(s_n, axis=-1, keepdims=True))
    p_c = jnp.exp(s_c - m)
    p_n = jnp.exp(s_n - m)
    l = jnp.sum(p_c, axis=-1, keepdims=True) + jnp.sum(p_n, axis=-1, keepdims=True)
    o_lat = (_dot(p_c.astype(BF16), cc) + _dot(p_n.astype(BF16), nc)) / l
    o_lat = o_lat.astype(BF16)
    out = jnp.zeros((t, MLA_WIDTH), F32)
    for hd in range(MLA_HEADS):
        out = out + _dot(o_lat[hd * t:(hd + 1) * t], wuvp_ref[hd])
    o_ref[0] = out.astype(BF16)


def _attn_sample(q, cache_ckv, cache_kr, new_ckv, new_kr, wts):
    nb, t, _ = q.shape
    past = cache_ckv.shape[2]
    place = np.zeros((QK_ROPE, HEAD_PAD), np.float32)
    place[np.arange(QK_ROPE), ROPE_LANE0 + np.arange(QK_ROPE)] = 1.0
    return pl.pallas_call(
        _attn_sample_kernel,
        grid=(nb,),
        in_specs=[pl.BlockSpec((1, t, MLA_HEADS * HEAD_PAD), lambda b: (b, 0, 0)),
                  pl.BlockSpec((1, 1, past, KV_LORA), lambda b: (0, b, 0, 0)),
                  pl.BlockSpec((1, 1, past, QK_ROPE), lambda b: (0, b, 0, 0)),
                  pl.BlockSpec((1, t, KV_LORA), lambda b: (b, 0, 0)),
                  pl.BlockSpec((1, t, QK_ROPE), lambda b: (b, 0, 0)),
                  _const_spec(wts["wuk3"].shape), _const_spec(wts["wuvp"].shape),
                  _const_spec((QK_ROPE, HEAD_PAD))],
        out_specs=pl.BlockSpec((1, t, MLA_WIDTH), lambda b: (b, 0, 0)),
        out_shape=jax.ShapeDtypeStruct((nb, t, MLA_WIDTH), BF16),
        compiler_params=pltpu.CompilerParams(dimension_semantics=("arbitrary",),
                                             vmem_limit_bytes=VMEM_LIMIT),
        name="attn_sample",
    )(q, cache_ckv, cache_kr, new_ckv, new_kr, wts["wuk3"], wts["wuvp"], jnp.asarray(place, BF16))


def _split3(x):
    hi = x.astype(BF16)
    r1 = x - hi.astype(F32)
    mid = r1.astype(BF16)
    lo = (r1 - mid.astype(F32)).astype(BF16)
    return hi, mid, lo


def _hgrn_kernel(hh_ref, lb_ref, gn_ref, s0_ref, o_ref, sout_ref, st_sc, *, L, nblk):
    tstep = pl.program_id(1)

    @pl.when(tstep == 0)
    def _():
        for hd in range(HG_HEADS):
            st_sc[hd] = s0_ref[0, hd].T

    row = lax.broadcasted_iota(jnp.int32, (L, 1), 0)
    rmod = row % HG_SUB
    tril = (lax.broadcasted_iota(jnp.int32, (L, L), 0) >= lax.broadcasted_iota(jnp.int32, (L, L), 1)
            ).astype(BF16)
    nsub = L // HG_SUB
    neg_inf = -jnp.inf

    def blk_body(j, carry):
        r0 = pl.multiple_of(j * L, L)
        rows = pl.ds(r0, L)
        for hd in range(HG_HEADS):
            c0 = hd * HG_EXPAND
            hq = hh_ref[0, rows, c0:c0 + HG_EXPAND]
            hf = hh_ref[0, rows, HG_WIDTH + c0:HG_WIDTH + c0 + HG_EXPAND]
            v = hh_ref[0, rows, 2 * HG_WIDTH + c0:2 * HG_WIDTH + c0 + HG_VDIM]
            hg = hh_ref[0, rows, 3 * HG_WIDTH + c0:3 * HG_WIDTH + c0 + HG_VDIM]
            lb = lb_ref[:, c0:c0 + HG_EXPAND]
            g = jnp.log(lb + (1.0 - lb) * _sigmoid(hf))
            kk = (1.0 - lb) * _sigmoid(-hf)
            qq = hq * _sigmoid(hq) * HG_SCALE
            g_hi, g_mid, g_lo = _split3(g)
            bc = _dot(tril, g_hi) + _dot(tril, g_mid) + _dot(tril, g_lo)
            vb = v.astype(BF16)
            st = st_sc[hd]
            o = _nt((qq * jnp.exp(bc)).astype(BF16), st.astype(BF16))
            if nsub > 1:
                qs_, ks_ = [], []
                for i in range(1, nsub):
                    b_i = bc[HG_SUB * i - 1:HG_SUB * i, :]
                    in_i = (row >= HG_SUB * i) & (row < HG_SUB * (i + 1))
                    qs_.append(qq * jnp.exp(jnp.where(in_i, bc - b_i, neg_inf)))
                    ks_.append(kk * jnp.exp(jnp.where(row < HG_SUB * i, b_i - bc, neg_inf)))
                att = _nt(jnp.concatenate(qs_, axis=1).astype(BF16),
                          jnp.concatenate(ks_, axis=1).astype(BF16))
                o = o + _dot(att.astype(BF16), vb)
            od = jnp.sum(qq * kk, axis=1, keepdims=True) * v
            for d in range(1, HG_SUB):
                kd = pltpu.roll(kk, d, 0)
                bd = pltpu.roll(bc, d, 0)
                vd = pltpu.roll(v, d, 0)
                e = jnp.exp(jnp.where(rmod >= d, bc - bd, neg_inf))
                od = od + jnp.sum(qq * kd * e, axis=1, keepdims=True) * vd
            o = o + od
            last = bc[L - 1:L, :]
            kdec = (kk * jnp.exp(last - bc)).astype(BF16)
            st_sc[hd] = st * jnp.exp(last) + _tn(vb, kdec)
            o = _rms(o, gn_ref[...]) * (hg * _sigmoid(hg))
            o_ref[0, rows, c0:c0 + HG_VDIM] = o
        return carry

    lax.fori_loop(0, nblk, blk_body, 0)

    @pl.when(tstep == pl.num_programs(1) - 1)
    def _():
        for hd in range(HG_HEADS):
            sout_ref[0, hd] = st_sc[hd].T


def _hgrn(hh, lb, hg_norm, s0, L):
    bp, tp, _ = hh.shape
    tt = min(HG_TIME_TILE, tp)
    return pl.pallas_call(
        functools.partial(_hgrn_kernel, L=L, nblk=tt // L),
        grid=(bp, tp // tt),
        in_specs=[pl.BlockSpec((1, tt, 4 * HG_WIDTH), lambda b, t: (b, t, 0)),
                  _const_spec((1, HG_WIDTH)), _const_spec((1, HG_VDIM)),
                  pl.BlockSpec((1, HG_HEADS, HG_EXPAND, HG_VDIM), lambda b, t: (b, 0, 0, 0))],
        out_specs=[pl.BlockSpec((1, tt, HG_WIDTH), lambda b, t: (b, t, 0)),
                   pl.BlockSpec((1, HG_HEADS, HG_EXPAND, HG_VDIM), lambda b, t: (b, 0, 0, 0))],
        out_shape=[jax.ShapeDtypeStruct((bp, tp, HG_WIDTH), F32),
                   jax.ShapeDtypeStruct((bp, HG_HEADS, HG_EXPAND, HG_VDIM), F32)],
        scratch_shapes=[pltpu.VMEM((HG_HEADS, HG_VDIM, HG_EXPAND), F32)],
        compiler_params=pltpu.CompilerParams(dimension_semantics=("arbitrary", "arbitrary"),
                                             vmem_limit_bytes=VMEM_LIMIT),
        name="hgrn",
    )(hh, lb, hg_norm, s0)


def _post_kernel(x_ref, at_ref, ho_ref, gg_ref, g1_ref, sh2_ref, sc2_ref, g2_ref,
                 wpa_ref, wpb_ref, wout_ref, n2_ref, wgu_ref, wdown_ref, fn_ref, y_ref):
    x = x_ref[0]
    ya = _dot(at_ref[0], wpa_ref[...])
    yb = _dot(ho_ref[0].astype(BF16), wpb_ref[...])
    mixed = _sigmoid(gg_ref[0, :, :D_MODEL]) * ya + _sigmoid(gg_ref[0, :, D_MODEL:]) * yb
    x1 = x + g1_ref[0] * _dot(mixed.astype(BF16), wout_ref[...])
    hb = (_rms(x1, n2_ref[...]) * (1.0 + sc2_ref[0]) + sh2_ref[0]).astype(BF16)
    ff = jnp.zeros(x.shape, F32)
    for c in range(D_FF // FF_CHUNK):
        gate = _dot(hb, wgu_ref[:, c * FF_CHUNK:(c + 1) * FF_CHUNK])
        up = _dot(hb, wgu_ref[:, D_FF + c * FF_CHUNK:D_FF + (c + 1) * FF_CHUNK])
        act = (gate * _sigmoid(gate) * up).astype(BF16)
        ff = ff + _dot(act, wdown_ref[c * FF_CHUNK:(c + 1) * FF_CHUNK, :])
    x2 = x1 + g2_ref[0] * ff
    y_ref[0] = _rms(x2, fn_ref[...])


def _post(x, attn, ho, gg, g1, sh2, sc2, g2, wts):
    bp, tp, _ = x.shape
    r = min(ROW_TILE, tp)
    mod_rows = g1.shape[1]
    mod_blk = (1, 1, D_MODEL) if mod_rows == 1 else (1, r, D_MODEL)
    mod_map = (lambda b, t: (b, 0, 0)) if mod_rows == 1 else (lambda b, t: (b, t, 0))
    row3 = lambda w: pl.BlockSpec((1, r, w), lambda b, t: (b, t, 0))
    mod = pl.BlockSpec(mod_blk, mod_map)
    names = ["wpa", "wpb", "wout", "norm2", "wgu", "wdown", "final_norm"]
    return pl.pallas_call(
        _post_kernel,
        grid=(bp, tp // r),
        in_specs=[row3(D_MODEL), row3(MLA_WIDTH), row3(HG_WIDTH), row3(2 * D_MODEL), mod, mod, mod, mod]
                 + [_const_spec(wts[n].shape) for n in names],
        out_specs=row3(D_MODEL),
        out_shape=jax.ShapeDtypeStruct((bp, tp, D_MODEL), F32),
        compiler_params=pltpu.CompilerParams(dimension_semantics=("arbitrary", "arbitrary"),
                                             vmem_limit_bytes=VMEM_LIMIT),
        name="post",
    )(x, attn, ho, gg, g1, sh2, sc2, g2, *[wts[n] for n in names])


def _prep_weights(w_in, q_norm, w_uq, kv_norm, w_ukv, lb_param, hg_norm, w_pa, w_pb, w_out,
                  norm1, norm2, w_gu, w_down, final_norm):
    half = QK_ROPE // 2
    offs = np.cumsum([0, Q_LORA, KV_LORA, QK_ROPE, HG_WIDTH, HG_WIDTH, HG_WIDTH, HG_WIDTH, D_MODEL, D_MODEL])
    w = w_in[0]
    w_kr = w[:, offs[2]:offs[3]]
    zeros = lambda n: jnp.zeros((D_MODEL, n), w.dtype)
    kr_a = jnp.concatenate([zeros(ROPE_LANE0), w_kr, zeros(LANES - ROPE_LANE0 - QK_ROPE)], axis=1)
    kr_b = jnp.concatenate([zeros(ROPE_LANE0), w_kr[:, half:], w_kr[:, :half],
                            zeros(LANES - ROPE_LANE0 - QK_ROPE)], axis=1)
    w_in_p = jnp.concatenate([w[:, offs[0]:offs[2]], w[:, offs[3]:offs[9]], kr_a, kr_b], axis=1).astype(BF16)

    uq = w_uq[0].reshape(Q_LORA, MLA_HEADS, QK_NOPE + QK_ROPE)
    pad = jnp.zeros((Q_LORA, MLA_HEADS, HEAD_PAD - QK_NOPE - QK_ROPE), uq.dtype)
    wqa = jnp.concatenate([uq, pad], axis=2).reshape(Q_LORA, MLA_HEADS * HEAD_PAD).astype(BF16)
    wqb = jnp.concatenate([jnp.zeros((Q_LORA, MLA_HEADS, QK_NOPE), uq.dtype),
                           uq[:, :, QK_NOPE + half:], uq[:, :, QK_NOPE:QK_NOPE + half], pad],
                          axis=2).reshape(Q_LORA, MLA_HEADS * HEAD_PAD).astype(BF16)

    ukv = w_ukv[0].reshape(KV_LORA, MLA_HEADS, QK_NOPE + V_HEAD)
    uk = ukv[:, :, :QK_NOPE]
    uv = ukv[:, :, QK_NOPE:]
    wuk = jnp.concatenate([uk, jnp.zeros((KV_LORA, MLA_HEADS, HEAD_PAD - QK_NOPE), uk.dtype)],
                          axis=2).reshape(KV_LORA, MLA_HEADS * HEAD_PAD).astype(BF16)
    wuvt = uv.reshape(KV_LORA, MLA_WIDTH).T.astype(BF16)
    wuk3 = jnp.transpose(uk, (1, 0, 2)).astype(BF16)
    eye = jnp.eye(MLA_HEADS, dtype=uv.dtype)
    wuvp = (jnp.transpose(uv, (1, 0, 2))[:, :, None, :] * eye[:, None, :, None]
            ).reshape(MLA_HEADS, KV_LORA, MLA_WIDTH).astype(BF16)

    lb = jnp.cumsum(jax.nn.softmax(lb_param.astype(F32), axis=0), axis=0)[0].reshape(1, HG_WIDTH)
    return {
        "w_in": w_in_p, "wqa": wqa, "wqb": wqb, "wqa_t": wqa.T, "wqb_t": wqb.T,
        "wuk": wuk, "wuvt": wuvt, "wuk3": wuk3, "wuvp": wuvp, "lb": lb,
        "q_norm": q_norm[0].reshape(1, Q_LORA), "kv_norm": kv_norm[0].reshape(1, KV_LORA),
        "hg_norm": hg_norm[0].reshape(1, HG_VDIM),
        "norm1": norm1[0].reshape(1, D_MODEL), "norm2": norm2[0].reshape(1, D_MODEL),
        "final_norm": final_norm.reshape(1, D_MODEL),
        "wpa": w_pa[0].astype(BF16), "wpb": w_pb[0].astype(BF16), "wout": w_out[0].astype(BF16),
        "wgu": w_gu[0].astype(BF16), "wdown": w_down[0].astype(BF16),
    }


def _rope_tables(pos):
    half = QK_ROPE // 2
    inv = ROPE_THETA ** (-np.arange(half, dtype=np.float64) / half)
    ang = np.asarray(pos, np.float64)[:, None] * inv[None, :]
    cos, sin = np.cos(ang), np.sin(ang)
    t = len(pos)
    ck = np.zeros((t, HEAD_PAD))
    sk = np.zeros((t, HEAD_PAD))
    ck[:, ROPE_LANE0:ROPE_LANE0 + QK_ROPE] = np.concatenate([cos, cos], axis=1)
    sk[:, ROPE_LANE0:ROPE_LANE0 + QK_ROPE] = np.concatenate([-sin, sin], axis=1)
    cq = ck.copy()
    cq[:, :QK_NOPE] = 1.0
    f = lambda a: jnp.asarray(a.astype(np.float32))
    return {"ck": f(ck), "sk": f(sk), "cq": f(cq), "sq": f(sk)}


def _trunk(prompt, x, mod, wts, tabs, hg_block, s0, cache=None):
    sh1, sc1, g1, sh2, sc2, g2 = mod
    if prompt:
        w = dict(wts, wqa=wts["wqa_t"], wqb=wts["wqb_t"])
        t = dict(tabs, cq=tabs["cq"].T, sq=tabs["sq"].T)
        qt, k, vt, ckv, kr, hh, gg = _pre(True, x, sh1, sc1, w, t)
        attn = _attn_prompt(qt, k, vt)
    else:
        q, ckv, kr, hh, gg = _pre(False, x, sh1, sc1, wts, tabs)
        nb, t_new = cache[0].shape[1], x.shape[1] // cache[0].shape[1]
        attn = _attn_sample(q.reshape(nb, t_new, -1), cache[0], cache[1],
                            ckv.reshape(nb, t_new, KV_LORA), kr.reshape(nb, t_new, QK_ROPE), wts)
        attn = attn.reshape(1, nb * t_new, MLA_WIDTH)
        hh = hh.reshape(nb, t_new, 4 * HG_WIDTH)
    ho, s_new = _hgrn(hh, wts["lb"], wts["hg_norm"], s0, hg_block)
    ho = ho.reshape(x.shape[0], x.shape[1], HG_WIDTH)
    y = _post(x, attn, ho, gg, g1, sh2, sc2, g2, wts)
    return y, ckv, kr, s_new


def kernel(x_prompt, x_sample, cache_ckv, cache_krope, state_hgrn, c_prompt, c_sample, w_in, q_norm, w_uq,
           kv_norm, w_ukv, lb_param, hg_norm, w_pa, w_pb, w_out, norm1, norm2, w_ada, b_ada, w_gu, w_down,
           final_norm):
    bsz, t_p, _ = x_prompt.shape
    nb, t_s, _ = x_sample.shape
    past = cache_ckv.shape[2]
    assert past % CHUNK == 0 and t_s <= CHUNK
    wts = _prep_weights(w_in, q_norm, w_uq, kv_norm, w_ukv, lb_param, hg_norm, w_pa, w_pb, w_out,
                        norm1, norm2, w_gu, w_down, final_norm)

    n_c = bsz + nb
    n_pad = -n_c % 8
    c_all = jnp.concatenate([c_prompt, c_sample, jnp.zeros((n_pad, D_MODEL), F32)], axis=0)
    mod_all = _ada(c_all, w_ada[0], b_ada[0])
    mod_p = [m[:, None, :] for m in jnp.split(mod_all[:bsz], 6, axis=-1)]
    mod_s = [jnp.repeat(m, t_s, axis=0)[None] for m in jnp.split(mod_all[bsz:n_c], 6, axis=-1)]

    tabs_p = _rope_tables(np.arange(t_p))
    tabs_s = _rope_tables(np.tile(past + np.arange(t_s), nb))

    s0_p = jnp.zeros((bsz, HG_HEADS, HG_EXPAND, HG_VDIM), F32)
    y_p, ckv_p, kr_p, s_p = _trunk(True, x_prompt, mod_p, wts, tabs_p, CHUNK, s0_p)
    y_s, ckv_s, kr_s, s_s = _trunk(False, x_sample.reshape(1, nb * t_s, D_MODEL), mod_s, wts, tabs_s, t_s,
                                   state_hgrn[0], cache=(cache_ckv, cache_krope))
    return (y_p, y_s.reshape(nb, t_s, D_MODEL), ckv_p[None], kr_p[None], s_p[None],
            ckv_s.reshape(nb, t_s, KV_LORA)[None], kr_s.reshape(nb, t_s, QK_ROPE)[None], s_s[None])
```

```python
import functools
import math

import numpy as np
import jax
import jax.numpy as jnp
from jax import lax
from jax.experimental import pallas as pl
from jax.experimental.pallas import tpu as pltpu

F32 = jnp.float32
BF16 = jnp.bfloat16

D_MODEL = 1024
CHUNK = 64
EPS = 1e-6
MLA_HEADS = 8
QK_NOPE = 64
QK_ROPE = 32
V_HEAD = 64
Q_LORA = 384
KV_LORA = 256
ROPE_THETA = 10000.0
MLA_WIDTH = MLA_HEADS * V_HEAD
MLA_SCALE = 1.0 / math.sqrt(QK_NOPE + QK_ROPE)
HG_HEADS = 4
HG_EXPAND = 128
HG_VDIM = 128
HG_WIDTH = HG_HEADS * HG_EXPAND
HG_SCALE = 1.0 / math.sqrt(HG_EXPAND)
D_FF = 2816
LOG2E = math.log2(math.e)

LANES = 128
HEAD_PAD = LANES
ROPE_LANE0 = QK_NOPE
VT_ROWS = V_HEAD + 16
VMEM_LIMIT = 56 * 1024 * 1024

ROW_TILE = 512
ATT_BLOCK = 1024
ATT_SUB = 256
HG_SUB = 8
HG_TIME_TILE = 512
FF_CHUNK = 1408

_C_QLAT = 0
_C_KVLAT = _C_QLAT + Q_LORA
_C_HG = _C_KVLAT + KV_LORA
_C_GATE = _C_HG + 4 * HG_WIDTH
_C_KRA = _C_GATE + 2 * D_MODEL
_C_KRB = _C_KRA + LANES
_C_END = _C_KRB + LANES


def _const_spec(shape):
    nd = len(shape)
    return pl.BlockSpec(shape, lambda *_: (0,) * nd, pipeline_mode=pl.Buffered(1))


def _nt(a, b):
    return lax.dot_general(a, b, (((1,), (1,)), ((), ())), preferred_element_type=F32)


def _tn(a, b):
    return lax.dot_general(a, b, (((0,), (0,)), ((), ())), preferred_element_type=F32)


def _dot(a, b):
    return jnp.dot(a, b, preferred_element_type=F32)


def _rms(x, w):
    return x * lax.rsqrt(jnp.mean(x * x, axis=-1, keepdims=True) + EPS) * w


def _sigmoid(x):
    return 1.0 / (1.0 + jnp.exp(-x))


def _ada_kernel(c_ref, w_ref, b_ref, o_ref):
    c = c_ref[...]
    a = (c * _sigmoid(c)).astype(BF16)
    o_ref[...] = _dot(a, w_ref[...].astype(BF16)) + b_ref[...]


def _ada(c, w_ada, b_ada):
    n = c.shape[0]
    cols = w_ada.shape[1]
    blk = D_MODEL
    return pl.pallas_call(
        _ada_kernel,
        grid=(cols // blk,),
        in_specs=[pl.BlockSpec((n, D_MODEL), lambda j: (0, 0)),
                  pl.BlockSpec((D_MODEL, blk), lambda j: (0, j)),
                  pl.BlockSpec((1, blk), lambda j: (0, j))],
        out_specs=pl.BlockSpec((n, blk), lambda j: (0, j)),
        out_shape=jax.ShapeDtypeStruct((n, cols), F32),
        compiler_params=pltpu.CompilerParams(dimension_semantics=("arbitrary",)),
        name="ada",
    )(c, w_ada, b_ada.reshape(1, cols))


def _pre_kernel(prompt, x_ref, sh_ref, sc_ref, n1_ref, win_ref, qn_ref, wqa_ref, wqb_ref, kvn_ref,
                cq_ref, sq_ref, ck_ref, sk_ref, *rest):
    if prompt:
        wuk_ref, wuvt_ref, qt_ref, k_ref, vt_ref, ckv_ref, kr_ref, hh_ref, gg_ref = rest
    else:
        q_ref, ckv_ref, kr_ref, hh_ref, gg_ref = rest
    x = x_ref[0]
    h = _rms(x, n1_ref[...]) * (1.0 + sc_ref[0]) + sh_ref[0]
    hb = h.astype(BF16)

    def proj(a, b):
        return _dot(hb, win_ref[:, a:b])

    hh_ref[0] = proj(_C_HG, _C_GATE)
    gg_ref[0] = proj(_C_GATE, _C_KRA)
    kr_full = proj(_C_KRA, _C_KRB) * ck_ref[...] + proj(_C_KRB, _C_END) * sk_ref[...]
    kr_ref[0] = kr_full[:, ROPE_LANE0:ROPE_LANE0 + QK_ROPE]
    c_kv = _rms(proj(_C_KVLAT, _C_HG), kvn_ref[...])
    ckv_ref[0] = c_kv
    cqb = _rms(proj(_C_QLAT, _C_KVLAT), qn_ref[...]).astype(BF16)
    if prompt:
        ckvb = c_kv.astype(BF16)
        qa = _nt(wqa_ref[...], cqb)
        qb = _nt(wqb_ref[...], cqb)
        cq = cq_ref[...]
        sq = sq_ref[...]
        for hd in range(MLA_HEADS):
            sl = slice(hd * HEAD_PAD, (hd + 1) * HEAD_PAD)
            qt_ref[0, hd] = ((qa[sl] * cq + qb[sl] * sq) * (MLA_SCALE * LOG2E)).astype(BF16)
        kall = _dot(ckvb, wuk_ref[...])
        for hd in range(MLA_HEADS):
            sl = slice(hd * HEAD_PAD, (hd + 1) * HEAD_PAD)
            k_ref[0, hd] = (kall[:, sl] + kr_full).astype(BF16)
        vt = _nt(wuvt_ref[...], ckvb)
        ones_rows = (lax.broadcasted_iota(jnp.int32, (VT_ROWS - V_HEAD, vt.shape[1]), 0) == 0).astype(BF16)
        for hd in range(MLA_HEADS):
            vt_ref[0, hd, :V_HEAD] = vt[hd * V_HEAD:(hd + 1) * V_HEAD].astype(BF16)
            vt_ref[0, hd, V_HEAD:] = ones_rows
    else:
        qa = _dot(cqb, wqa_ref[...])
        qb = _dot(cqb, wqb_ref[...])
        cq = cq_ref[...]
        sq = sq_ref[...]
        for hd in range(MLA_HEADS):
            sl = slice(hd * HEAD_PAD, (hd + 1) * HEAD_PAD)
            q_ref[0, :, sl] = ((qa[:, sl] * cq + qb[:, sl] * sq) * MLA_SCALE).astype(BF16)


def _pre(prompt, x, sh1, sc1, wts, tabs):
    bp, tp, _ = x.shape
    r = min(ROW_TILE, tp)
    nt = tp // r
    mod_rows = sh1.shape[1]
    mod_blk = (1, 1, D_MODEL) if mod_rows == 1 else (1, r, D_MODEL)
    mod_map = (lambda b, t: (b, 0, 0)) if mod_rows == 1 else (lambda b, t: (b, t, 0))
    row3 = lambda w: pl.BlockSpec((1, r, w), lambda b, t: (b, t, 0))
    in_specs = [row3(D_MODEL), pl.BlockSpec(mod_blk, mod_map), pl.BlockSpec(mod_blk, mod_map),
                _const_spec((1, D_MODEL)), _const_spec(wts["w_in"].shape), _const_spec((1, Q_LORA)),
                _const_spec(wts["wqa"].shape), _const_spec(wts["wqb"].shape), _const_spec((1, KV_LORA))]
    if prompt:
        in_specs += [pl.BlockSpec((HEAD_PAD, r), lambda b, t: (0, t))] * 2
    else:
        in_specs += [pl.BlockSpec((r, HEAD_PAD), lambda b, t: (t, 0))] * 2
    in_specs += [pl.BlockSpec((r, HEAD_PAD), lambda b, t: (t, 0))] * 2
    args = [x, sh1, sc1, wts["norm1"], wts["w_in"], wts["q_norm"], wts["wqa"], wts["wqb"], wts["kv_norm"],
            tabs["cq"], tabs["sq"], tabs["ck"], tabs["sk"]]
    common_shapes = [jax.ShapeDtypeStruct((bp, tp, KV_LORA), F32),
                     jax.ShapeDtypeStruct((bp, tp, QK_ROPE), F32),
                     jax.ShapeDtypeStruct((bp, tp, 4 * HG_WIDTH), F32),
                     jax.ShapeDtypeStruct((bp, tp, 2 * D_MODEL), F32)]
    common_specs = [row3(KV_LORA), row3(QK_ROPE), row3(4 * HG_WIDTH), row3(2 * D_MODEL)]
    if prompt:
        in_specs += [_const_spec(wts["wuk"].shape), _const_spec(wts["wuvt"].shape)]
        args += [wts["wuk"], wts["wuvt"]]
        out_shape = [jax.ShapeDtypeStruct((bp, MLA_HEADS, HEAD_PAD, tp), BF16),
                     jax.ShapeDtypeStruct((bp, MLA_HEADS, tp, HEAD_PAD), BF16),
                     jax.ShapeDtypeStruct((bp, MLA_HEADS, VT_ROWS, tp), BF16)] + common_shapes
        out_specs = [pl.BlockSpec((1, MLA_HEADS, HEAD_PAD, r), lambda b, t: (b, 0, 0, t)),
                     pl.BlockSpec((1, MLA_HEADS, r, HEAD_PAD), lambda b, t: (b, 0, t, 0)),
                     pl.BlockSpec((1, MLA_HEADS, VT_ROWS, r), lambda b, t: (b, 0, 0, t))] + common_specs
    else:
        out_shape = [jax.ShapeDtypeStruct((bp, tp, MLA_HEADS * HEAD_PAD), BF16)] + common_shapes
        out_specs = [row3(MLA_HEADS * HEAD_PAD)] + common_specs
    return pl.pallas_call(
        functools.partial(_pre_kernel, prompt),
        grid=(bp, nt),
        in_specs=in_specs,
        out_specs=out_specs,
        out_shape=out_shape,
        compiler_params=pltpu.CompilerParams(dimension_semantics=("arbitrary", "arbitrary"),
                                             vmem_limit_bytes=VMEM_LIMIT),
        name="pre_prompt" if prompt else "pre_sample",
    )(*args)


def _attn_kernel(qi_ref, ki_ref, qt_ref, k_ref, vt_ref, o_ref, m_sc, acc_sc, *slabs, blk, sub):
    p = pl.program_id(1)
    qi = qi_ref[p]
    ki = ki_ref[p]
    ns = blk // sub
    s_bufs, st_bufs = slabs[:ns], slabs[ns:]

    @pl.when(ki == 0)
    def _():
        m_sc[...] = jnp.full(m_sc.shape, -jnp.inf, F32)
        acc_sc[...] = jnp.zeros(acc_sc.shape, F32)

    def run(diag):
        if diag:
            kc = lax.broadcasted_iota(jnp.int32, (sub, sub), 0) // CHUNK
            qc = lax.broadcasted_iota(jnp.int32, (sub, sub), 1) // CHUNK
            visible = kc <= qc

        def rows(qs):
            return (qs + 1) * sub if diag else blk

        def scores(hd, qs):
            ql = slice(qs * sub, (qs + 1) * sub)
            nrow = rows(qs)
            s_buf, st_buf = s_bufs[qs], st_bufs[qs]
            q_t = qt_ref[0, hd, :, ql]
            half = max(sub, (nrow // 2) // sub * sub)
            mx = None
            for lo, hi in ((0, half), (half, nrow)):
                if lo == hi:
                    continue
                s = _dot(k_ref[0, hd, lo:hi, :], q_t)
                if diag and hi == nrow:
                    top = nrow - sub - lo
                    dg = jnp.where(visible, s[top:], -jnp.inf)
                    s_buf[nrow - sub:nrow, :] = dg
                    part = jnp.max(dg, axis=0, keepdims=True)
                    if top > 0:
                        s_buf[lo:nrow - sub, :] = s[:top]
                        part = jnp.maximum(part, jnp.max(s[:top], axis=0, keepdims=True))
                else:
                    s_buf[lo:hi, :] = s
                    part = jnp.max(s, axis=0, keepdims=True)
                mx = part if mx is None else jnp.maximum(mx, part)
            m_old = m_sc[hd, :, ql]
            m_new = jnp.maximum(m_old, mx)
            m_sc[hd, :, ql] = m_new
            st_buf[0] = m_new
            st_buf[1] = jnp.exp2(m_old - m_new)

        def weighted(hd, qs):
            ql = slice(qs * sub, (qs + 1) * sub)
            nrow = rows(qs)
            s_buf, st_buf = s_bufs[qs], st_bufs[qs]
            pm = jnp.exp2(s_buf[:nrow, :] - st_buf[0])
            acc_sc[hd, :, ql] = st_buf[1] * acc_sc[hd, :, ql] + _dot(vt_ref[0, hd, :, :nrow], pm.astype(BF16))

        scores(0, 0)

        for hd in range(MLA_HEADS):
            for qs in range(ns):
                if qs + 1 < ns:
                    scores(hd, qs + 1)
                elif hd + 1 < MLA_HEADS:
                    scores(hd + 1, 0)
                weighted(hd, qs)

    @pl.when(ki < qi)
    def _():
        run(False)

    @pl.when(ki == qi)
    def _():
        run(True)
        o = acc_sc[:, :V_HEAD, :] / acc_sc[:, V_HEAD:V_HEAD + 1, :]
        o_ref[0] = o.reshape(MLA_WIDTH, blk).T.astype(BF16)


def _attn_prompt(qt, k, vt):
    bp, _, _, tp = qt.shape
    blk = min(ATT_BLOCK, tp)
    sub = min(ATT_SUB, blk)
    ns = blk // sub
    nb = tp // blk
    qi_of = np.array([q for q in range(nb) for _ in range(q + 1)], np.int32)
    ki_of = np.array([kk for q in range(nb) for kk in range(q + 1)], np.int32)
    grid_spec = pltpu.PrefetchScalarGridSpec(
        num_scalar_prefetch=2,
        grid=(bp, len(qi_of)),
        in_specs=[pl.BlockSpec((1, MLA_HEADS, HEAD_PAD, blk), lambda b, p, qi, ki: (b, 0, 0, qi[p])),
                  pl.BlockSpec((1, MLA_HEADS, blk, HEAD_PAD), lambda b, p, qi, ki: (b, 0, ki[p], 0)),
                  pl.BlockSpec((1, MLA_HEADS, VT_ROWS, blk), lambda b, p, qi, ki: (b, 0, 0, ki[p]))],
        out_specs=pl.BlockSpec((1, blk, MLA_WIDTH), lambda b, p, qi, ki: (b, qi[p], 0)),
        scratch_shapes=[pltpu.VMEM((MLA_HEADS, 1, blk), F32),
                        pltpu.VMEM((MLA_HEADS, VT_ROWS, blk), F32)]
                       + [pltpu.VMEM((blk, sub), F32)] * ns
                       + [pltpu.VMEM((2, 1, sub), F32)] * ns,
    )
    return pl.pallas_call(
        functools.partial(_attn_kernel, blk=blk, sub=sub),
        grid_spec=grid_spec,
        out_shape=jax.ShapeDtypeStruct((bp, tp, MLA_WIDTH), BF16),
        compiler_params=pltpu.CompilerParams(dimension_semantics=("arbitrary", "arbitrary"),
                                             vmem_limit_bytes=VMEM_LIMIT),
        name="attn_prompt",
    )(jnp.asarray(qi_of), jnp.asarray(ki_of), qt, k, vt)


def _attn_sample_kernel(q_ref, cc_ref, ckr_ref, nc_ref, nkr_ref, wuk_ref, wuvp_ref, place_ref, o_ref):
    t = q_ref.shape[1]
    q = q_ref[0]
    qh = [q[:, hd * HEAD_PAD:(hd + 1) * HEAD_PAD] for hd in range(MLA_HEADS)]
    q_stack = jnp.concatenate(qh, axis=0)
    q_abs = jnp.concatenate(
        [_nt(qh[hd][:, :QK_NOPE], wuk_ref[hd]) for hd in range(MLA_HEADS)], axis=0).astype(BF16)
    place = place_ref[...]
    cc = cc_ref[0, 0].astype(BF16)
    nc = nc_ref[0].astype(BF16)
    ckr = _dot(ckr_ref[0, 0].astype(BF16), place).astype(BF16)
    nkr = _dot(nkr_ref[0].astype(BF16), place).astype(BF16)
    s_c = _nt(q_abs, cc) + _nt(q_stack, ckr)
    s_n = _nt(q_abs, nc) + _nt(q_stack, nkr)
    m = jnp.maximum(jnp.max(s_c, axis=-1, keepdims=True), jnp.max(s_n, axis=-1, keepdims=True))
    p_c = jnp.exp(s_c - m)
    p_n = jnp.exp(s_n - m)
    l = jnp.sum(p_c, axis=-1, keepdims=True) + jnp.sum(p_n, axis=-1, keepdims=True)
    o_lat = (_dot(p_c.astype(BF16), cc) + _dot(p_n.astype(BF16), nc)) / l
    o_lat = o_lat.astype(BF16)
    out = jnp.zeros((t, MLA_WIDTH), F32)
    for hd in range(MLA_HEADS):
        out = out + _dot(o_lat[hd * t:(hd + 1) * t], wuvp_ref[hd])
    o_ref[0] = out.astype(BF16)


def _attn_sample(q, cache_ckv, cache_kr, new_ckv, new_kr, wts):
    nb, t, _ = q.shape
    past = cache_ckv.shape[2]
    place = np.zeros((QK_ROPE, HEAD_PAD), np.float32)
    place[np.arange(QK_ROPE), ROPE_LANE0 + np.arange(QK_ROPE)] = 1.0
    return pl.pallas_call(
        _attn_sample_kernel,
        grid=(nb,),
        in_specs=[pl.BlockSpec((1, t, MLA_HEADS * HEAD_PAD), lambda b: (b, 0, 0)),
                  pl.BlockSpec((1, 1, past, KV_LORA), lambda b: (0, b, 0, 0)),
                  pl.BlockSpec((1, 1, past, QK_ROPE), lambda b: (0, b, 0, 0)),
                  pl.BlockSpec((1, t, KV_LORA), lambda b: (b, 0, 0)),
                  pl.BlockSpec((1, t, QK_ROPE), lambda b: (b, 0, 0)),
                  _const_spec(wts["wuk3"].shape), _const_spec(wts["wuvp"].shape),
                  _const_spec((QK_ROPE, HEAD_PAD))],
        out_specs=pl.BlockSpec((1, t, MLA_WIDTH), lambda b: (b, 0, 0)),
        out_shape=jax.ShapeDtypeStruct((nb, t, MLA_WIDTH), BF16),
        compiler_params=pltpu.CompilerParams(dimension_semantics=("arbitrary",),
                                             vmem_limit_bytes=VMEM_LIMIT),
        name="attn_sample",
    )(q, cache_ckv, cache_kr, new_ckv, new_kr, wts["wuk3"], wts["wuvp"], jnp.asarray(place, BF16))


def _split3(x):
    hi = x.astype(BF16)
    r1 = x - hi.astype(F32)
    mid = r1.astype(BF16)
    lo = (r1 - mid.astype(F32)).astype(BF16)
    return hi, mid, lo


def _hgrn_kernel(hh_ref, lb_ref, gn_ref, s0_ref, o_ref, sout_ref, st_sc, sh_sc, *, L, nblk):
    tstep = pl.program_id(1)

    @pl.when(tstep == 0)
    def _():
        for hd in range(HG_HEADS):
            st_sc[hd] = s0_ref[0, hd].T

    sh_sc[:, :, :HG_SUB, :] = jnp.zeros((HG_HEADS, 3, HG_SUB, HG_EXPAND), F32)

    row = lax.broadcasted_iota(jnp.int32, (L, 1), 0)
    rmod = row % HG_SUB
    tril = (lax.broadcasted_iota(jnp.int32, (L, L), 0) >= lax.broadcasted_iota(jnp.int32, (L, L), 1)
            ).astype(BF16)
    nsub = L // HG_SUB
    neg_inf = -jnp.inf

    def blk_body(j, carry):
        r0 = pl.multiple_of(j * L, L)
        rows = pl.ds(r0, L)
        for hd in range(HG_HEADS):
            c0 = hd * HG_EXPAND
            hq = hh_ref[0, rows, c0:c0 + HG_EXPAND]
            hf = hh_ref[0, rows, HG_WIDTH + c0:HG_WIDTH + c0 + HG_EXPAND]
            v = hh_ref[0, rows, 2 * HG_WIDTH + c0:2 * HG_WIDTH + c0 + HG_VDIM]
            hg = hh_ref[0, rows, 3 * HG_WIDTH + c0:3 * HG_WIDTH + c0 + HG_VDIM]
            lb = lb_ref[:, c0:c0 + HG_EXPAND]
            sg = _sigmoid(hf)
            g = jnp.log2(lb + (1.0 - lb) * sg)
            kk = (1.0 - lb) * (1.0 - sg)
            qq = hq * _sigmoid(hq) * HG_SCALE
            g_hi, g_mid, g_lo = _split3(g)
            bc = _dot(tril, g_hi) + _dot(tril, g_mid) + _dot(tril, g_lo)
            vb = v.astype(BF16)
            st = st_sc[hd]
            o = _nt((qq * jnp.exp2(bc)).astype(BF16), st.astype(BF16))
            if nsub > 1:
                qs_, ks_ = [], []
                for i in range(1, nsub):
                    b_i = bc[HG_SUB * i - 1:HG_SUB * i, :]
                    in_i = (row >= HG_SUB * i) & (row < HG_SUB * (i + 1))
                    qs_.append(qq * jnp.exp2(jnp.where(in_i, bc - b_i, neg_inf)))
                    ks_.append(kk * jnp.exp2(jnp.where(row < HG_SUB * i, b_i - bc, neg_inf)))
                att = _nt(jnp.concatenate(qs_, axis=1).astype(BF16),
                          jnp.concatenate(ks_, axis=1).astype(BF16))
                o = o + _dot(att.astype(BF16), vb)
            sh_sc[hd, 0, HG_SUB:, :] = kk
            sh_sc[hd, 1, HG_SUB:, :] = bc
            sh_sc[hd, 2, HG_SUB:, :] = v
            od = jnp.sum(qq * kk, axis=1, keepdims=True) * v
            for d in range(1, HG_SUB):
                shifted = pl.ds(HG_SUB - d, L)
                e = jnp.exp2(jnp.where(rmod >= d, bc - sh_sc[hd, 1, shifted, :], neg_inf))
                od = od + jnp.sum(qq * sh_sc[hd, 0, shifted, :] * e, axis=1, keepdims=True) * sh_sc[hd, 2, shifted, :]
            o = o + od
            last = bc[L - 1:L, :]
            kdec = (kk * jnp.exp2(last - bc)).astype(BF16)
            st_sc[hd] = st * jnp.exp2(last) + _tn(vb, kdec)
            o = _rms(o, gn_ref[...]) * (hg * _sigmoid(hg))
            o_ref[0, rows, c0:c0 + HG_VDIM] = o
        return carry

    lax.fori_loop(0, nblk, blk_body, 0, unroll=min(2, nblk))

    @pl.when(tstep == pl.num_programs(1) - 1)
    def _():
        for hd in range(HG_HEADS):
            sout_ref[0, hd] = st_sc[hd].T


def _hgrn(hh, lb, hg_norm, s0, L):
    bp, tp, _ = hh.shape
    tt = min(HG_TIME_TILE, tp)
    return pl.pallas_call(
        functools.partial(_hgrn_kernel, L=L, nblk=tt // L),
        grid=(bp, tp // tt),
        in_specs=[pl.BlockSpec((1, tt, 4 * HG_WIDTH), lambda b, t: (b, t, 0)),
                  _const_spec((1, HG_WIDTH)), _const_spec((1, HG_VDIM)),
                  pl.BlockSpec((1, HG_HEADS, HG_EXPAND, HG_VDIM), lambda b, t: (b, 0, 0, 0))],
        out_specs=[pl.BlockSpec((1, tt, HG_WIDTH), lambda b, t: (b, t, 0)),
                   pl.BlockSpec((1, HG_HEADS, HG_EXPAND, HG_VDIM), lambda b, t: (b, 0, 0, 0))],
        out_shape=[jax.ShapeDtypeStruct((bp, tp, HG_WIDTH), F32),
                   jax.ShapeDtypeStruct((bp, HG_HEADS, HG_EXPAND, HG_VDIM), F32)],
        scratch_shapes=[pltpu.VMEM((HG_HEADS, HG_VDIM, HG_EXPAND), F32),
                        pltpu.VMEM((HG_HEADS, 3, HG_SUB + L, HG_EXPAND), F32)],
        compiler_params=pltpu.CompilerParams(dimension_semantics=("arbitrary", "arbitrary"),
                                             vmem_limit_bytes=VMEM_LIMIT),
        name="hgrn",
    )(hh, lb, hg_norm, s0)


def _post_kernel(x_ref, at_ref, ho_ref, gg_ref, g1_ref, sh2_ref, sc2_ref, g2_ref,
                 wpa_ref, wpb_ref, wout_ref, n2_ref, wgu_ref, wdown_ref, fn_ref, y_ref):
    x = x_ref[0]
    ya = _dot(at_ref[0], wpa_ref[...])
    yb = _dot(ho_ref[0].astype(BF16), wpb_ref[...])
    mixed = _sigmoid(gg_ref[0, :, :D_MODEL]) * ya + _sigmoid(gg_ref[0, :, D_MODEL:]) * yb
    x1 = x + g1_ref[0] * _dot(mixed.astype(BF16), wout_ref[...])
    hb = (_rms(x1, n2_ref[...]) * (1.0 + sc2_ref[0]) + sh2_ref[0]).astype(BF16)
    ff = jnp.zeros(x.shape, F32)
    for c in range(D_FF // FF_CHUNK):
        gate = _dot(hb, wgu_ref[:, c * FF_CHUNK:(c + 1) * FF_CHUNK])
        up = _dot(hb, wgu_ref[:, D_FF + c * FF_CHUNK:D_FF + (c + 1) * FF_CHUNK])
        act = (gate * _sigmoid(gate) * up).astype(BF16)
        ff = ff + _dot(act, wdown_ref[c * FF_CHUNK:(c + 1) * FF_CHUNK, :])
    x2 = x1 + g2_ref[0] * ff
    y_ref[0] = _rms(x2, fn_ref[...])


def _post(x, attn, ho, gg, g1, sh2, sc2, g2, wts):
    bp, tp, _ = x.shape
    r = min(ROW_TILE, tp)
    mod_rows = g1.shape[1]
    mod_blk = (1, 1, D_MODEL) if mod_rows == 1 else (1, r, D_MODEL)
    mod_map = (lambda b, t: (b, 0, 0)) if mod_rows == 1 else (lambda b, t: (b, t, 0))
    row3 = lambda w: pl.BlockSpec((1, r, w), lambda b, t: (b, t, 0))
    mod = pl.BlockSpec(mod_blk, mod_map)
    names = ["wpa", "wpb", "wout", "norm2", "wgu", "wdown", "final_norm"]
    return pl.pallas_call(
        _post_kernel,
        grid=(bp, tp // r),
        in_specs=[row3(D_MODEL), row3(MLA_WIDTH), row3(HG_WIDTH), row3(2 * D_MODEL), mod, mod, mod, mod]
                 + [_const_spec(wts[n].shape) for n in names],
        out_specs=row3(D_MODEL),
        out_shape=jax.ShapeDtypeStruct((bp, tp, D_MODEL), F32),
        compiler_params=pltpu.CompilerParams(dimension_semantics=("arbitrary", "arbitrary"),
                                             vmem_limit_bytes=VMEM_LIMIT),
        name="post",
    )(x, attn, ho, gg, g1, sh2, sc2, g2, *[wts[n] for n in names])


def _prep_weights(w_in, q_norm, w_uq, kv_norm, w_ukv, lb_param, hg_norm, w_pa, w_pb, w_out,
                  norm1, norm2, w_gu, w_down, final_norm):
    half = QK_ROPE // 2
    offs = np.cumsum([0, Q_LORA, KV_LORA, QK_ROPE, HG_WIDTH, HG_WIDTH, HG_WIDTH, HG_WIDTH, D_MODEL, D_MODEL])
    w = w_in[0]
    w_kr = w[:, offs[2]:offs[3]]
    zeros = lambda n: jnp.zeros((D_MODEL, n), w.dtype)
    kr_a = jnp.concatenate([zeros(ROPE_LANE0), w_kr, zeros(LANES - ROPE_LANE0 - QK_ROPE)], axis=1)
    kr_b = jnp.concatenate([zeros(ROPE_LANE0), w_kr[:, half:], w_kr[:, :half],
                            zeros(LANES - ROPE_LANE0 - QK_ROPE)], axis=1)
    w_in_p = jnp.concatenate([w[:, offs[0]:offs[2]], w[:, offs[3]:offs[9]], kr_a, kr_b], axis=1).astype(BF16)

    uq = w_uq[0].reshape(Q_LORA, MLA_HEADS, QK_NOPE + QK_ROPE)
    pad = jnp.zeros((Q_LORA, MLA_HEADS, HEAD_PAD - QK_NOPE - QK_ROPE), uq.dtype)
    wqa = jnp.concatenate([uq, pad], axis=2).reshape(Q_LORA, MLA_HEADS * HEAD_PAD).astype(BF16)
    wqb = jnp.concatenate([jnp.zeros((Q_LORA, MLA_HEADS, QK_NOPE), uq.dtype),
                           uq[:, :, QK_NOPE + half:], uq[:, :, QK_NOPE:QK_NOPE + half], pad],
                          axis=2).reshape(Q_LORA, MLA_HEADS * HEAD_PAD).astype(BF16)

    ukv = w_ukv[0].reshape(KV_LORA, MLA_HEADS, QK_NOPE + V_HEAD)
    uk = ukv[:, :, :QK_NOPE]
    uv = ukv[:, :, QK_NOPE:]
    wuk = jnp.concatenate([uk, jnp.zeros((KV_LORA, MLA_HEADS, HEAD_PAD - QK_NOPE), uk.dtype)],
                          axis=2).reshape(KV_LORA, MLA_HEADS * HEAD_PAD).astype(BF16)
    wuvt = uv.reshape(KV_LORA, MLA_WIDTH).T.astype(BF16)
    wuk3 = jnp.transpose(uk, (1, 0, 2)).astype(BF16)
    eye = jnp.eye(MLA_HEADS, dtype=uv.dtype)
    wuvp = (jnp.transpose(uv, (1, 0, 2))[:, :, None, :] * eye[:, None, :, None]
            ).reshape(MLA_HEADS, KV_LORA, MLA_WIDTH).astype(BF16)

    lb = jnp.cumsum(jax.nn.softmax(lb_param.astype(F32), axis=0), axis=0)[0].reshape(1, HG_WIDTH)
    return {
        "w_in": w_in_p, "wqa": wqa, "wqb": wqb, "wqa_t": wqa.T, "wqb_t": wqb.T,
        "wuk": wuk, "wuvt": wuvt, "wuk3": wuk3, "wuvp": wuvp, "lb": lb,
        "q_norm": q_norm[0].reshape(1, Q_LORA), "kv_norm": kv_norm[0].reshape(1, KV_LORA),
        "hg_norm": hg_norm[0].reshape(1, HG_VDIM),
        "norm1": norm1[0].reshape(1, D_MODEL), "norm2": norm2[0].reshape(1, D_MODEL),
        "final_norm": final_norm.reshape(1, D_MODEL),
        "wpa": w_pa[0].astype(BF16), "wpb": w_pb[0].astype(BF16), "wout": w_out[0].astype(BF16),
        "wgu": w_gu[0].astype(BF16), "wdown": w_down[0].astype(BF16),
    }


def _rope_tables(pos):
    half = QK_ROPE // 2
    inv = ROPE_THETA ** (-np.arange(half, dtype=np.float64) / half)
    ang = np.asarray(pos, np.float64)[:, None] * inv[None, :]
    cos, sin = np.cos(ang), np.sin(ang)
    t = len(pos)
    ck = np.zeros((t, HEAD_PAD))
    sk = np.zeros((t, HEAD_PAD))
    ck[:, ROPE_LANE0:ROPE_LANE0 + QK_ROPE] = np.concatenate([cos, cos], axis=1)
    sk[:, ROPE_LANE0:ROPE_LANE0 + QK_ROPE] = np.concatenate([-sin, sin], axis=1)
    cq = ck.copy()
    cq[:, :QK_NOPE] = 1.0
    f = lambda a: jnp.asarray(a.astype(np.float32))
    return {"ck": f(ck), "sk": f(sk), "cq": f(cq), "sq": f(sk)}


def _trunk(prompt, x, mod, wts, tabs, hg_block, s0, cache=None):
    sh1, sc1, g1, sh2, sc2, g2 = mod
    if prompt:
        w = dict(wts, wqa=wts["wqa_t"], wqb=wts["wqb_t"])
        t = dict(tabs, cq=tabs["cq"].T, sq=tabs["sq"].T)
        qt, k, vt, ckv, kr, hh, gg = _pre(True, x, sh1, sc1, w, t)
        attn = _attn_prompt(qt, k, vt)
    else:
        q, ckv, kr, hh, gg = _pre(False, x, sh1, sc1, wts, tabs)
        nb, t_new = cache[0].shape[1], x.shape[1] // cache[0].shape[1]
        attn = _attn_sample(q.reshape(nb, t_new, -1), cache[0], cache[1],
                            ckv.reshape(nb, t_new, KV_LORA), kr.reshape(nb, t_new, QK_ROPE), wts)
        attn = attn.reshape(1, nb * t_new, MLA_WIDTH)
        hh = hh.reshape(nb, t_new, 4 * HG_WIDTH)
    ho, s_new = _hgrn(hh, wts["lb"], wts["hg_norm"], s0, hg_block)
    ho = ho.reshape(x.shape[0], x.shape[1], HG_WIDTH)
    y = _post(x, attn, ho, gg, g1, sh2, sc2, g2, wts)
    return y, ckv, kr, s_new


def kernel(x_prompt, x_sample, cache_ckv, cache_krope, state_hgrn, c_prompt, c_sample, w_in, q_norm, w_uq,
           kv_norm, w_ukv, lb_param, hg_norm, w_pa, w_pb, w_out, norm1, norm2, w_ada, b_ada, w_gu, w_down,
           final_norm):
    bsz, t_p, _ = x_prompt.shape
    nb, t_s, _ = x_sample.shape
    past = cache_ckv.shape[2]
    assert past % CHUNK == 0 and t_s <= CHUNK
    wts = _prep_weights(w_in, q_norm, w_uq, kv_norm, w_ukv, lb_param, hg_norm, w_pa, w_pb, w_out,
                        norm1, norm2, w_gu, w_down, final_norm)

    n_c = bsz + nb
    n_pad = -n_c % 8
    c_all = jnp.concatenate([c_prompt, c_sample, jnp.zeros((n_pad, D_MODEL), F32)], axis=0)
    mod_all = _ada(c_all, w_ada[0], b_ada[0])
    mod_p = [m[:, None, :] for m in jnp.split(mod_all[:bsz], 6, axis=-1)]
    mod_s = [jnp.repeat(m, t_s, axis=0)[None] for m in jnp.split(mod_all[bsz:n_c], 6, axis=-1)]

    tabs_p = _rope_tables(np.arange(t_p))
    tabs_s = _rope_tables(np.tile(past + np.arange(t_s), nb))

    s0_p = jnp.zeros((bsz, HG_HEADS, HG_EXPAND, HG_VDIM), F32)
    y_p, ckv_p, kr_p, s_p = _trunk(True, x_prompt, mod_p, wts, tabs_p, CHUNK, s0_p)
    y_s, ckv_s, kr_s, s_s = _trunk(False, x_sample.reshape(1, nb * t_s, D_MODEL), mod_s, wts, tabs_s, t_s,
                                   state_hgrn[0], cache=(cache_ckv, cache_krope))
    return (y_p, y_s.reshape(nb, t_s, D_MODEL), ckv_p[None], kr_p[None], s_p[None],
            ckv_s.reshape(nb, t_s, KV_LORA)[None], kr_s.reshape(nb, t_s, QK_ROPE)[None], s_s[None])
```

```python
import functools
import math

import numpy as np
import jax
import jax.numpy as jnp
from jax import lax
from jax.experimental import pallas as pl
from jax.experimental.pallas import tpu as pltpu

F32 = jnp.float32
BF16 = jnp.bfloat16

D_MODEL = 1024
CHUNK = 64
EPS = 1e-6
MLA_HEADS = 8
QK_NOPE = 64
QK_ROPE = 32
V_HEAD = 64
Q_LORA = 384
KV_LORA = 256
ROPE_THETA = 10000.0
MLA_WIDTH = MLA_HEADS * V_HEAD
MLA_SCALE = 1.0 / math.sqrt(QK_NOPE + QK_ROPE)
HG_HEADS = 4
HG_EXPAND = 128
HG_VDIM = 128
HG_WIDTH = HG_HEADS * HG_EXPAND
HG_SCALE = 1.0 / math.sqrt(HG_EXPAND)
D_FF = 2816
LOG2E = math.log2(math.e)

LANES = 128
HEAD_PAD = LANES
ROPE_LANE0 = QK_NOPE
VT_ROWS = V_HEAD + 16
VMEM_LIMIT = 56 * 1024 * 1024

ROW_TILE = 512
PRE_ROW_TILE = 256
ATT_BLOCK = 1024
ATT_SUB = 256
HG_SUB = 8
HG_TIME_TILE = 512
FF_CHUNK = 1408

_C_QLAT = 0
_C_KVLAT = _C_QLAT + Q_LORA
_C_HG = _C_KVLAT + KV_LORA
_C_GATE = _C_HG + 4 * HG_WIDTH
_C_KRA = _C_GATE + 2 * D_MODEL
_C_KRB = _C_KRA + LANES
_C_END = _C_KRB + LANES


def _const_spec(shape):
    nd = len(shape)
    return pl.BlockSpec(shape, lambda *_: (0,) * nd, pipeline_mode=pl.Buffered(1))


def _nt(a, b):
    return lax.dot_general(a, b, (((1,), (1,)), ((), ())), preferred_element_type=F32)


def _tn(a, b):
    return lax.dot_general(a, b, (((0,), (0,)), ((), ())), preferred_element_type=F32)


def _dot(a, b):
    return jnp.dot(a, b, preferred_element_type=F32)


def _rms(x, w):
    return x * lax.rsqrt(jnp.mean(x * x, axis=-1, keepdims=True) + EPS) * w


def _sigmoid(x):
    return 1.0 / (1.0 + jnp.exp(-x))


def _ada_kernel(c_ref, w_ref, b_ref, o_ref):
    c = c_ref[...]
    a = (c * _sigmoid(c)).astype(BF16)
    o_ref[...] = _dot(a, w_ref[...].astype(BF16)) + b_ref[...]


def _ada(c, w_ada, b_ada):
    n = c.shape[0]
    cols = w_ada.shape[1]
    blk = D_MODEL
    return pl.pallas_call(
        _ada_kernel,
        grid=(cols // blk,),
        in_specs=[pl.BlockSpec((n, D_MODEL), lambda j: (0, 0)),
                  pl.BlockSpec((D_MODEL, blk), lambda j: (0, j)),
                  pl.BlockSpec((1, blk), lambda j: (0, j))],
        out_specs=pl.BlockSpec((n, blk), lambda j: (0, j)),
        out_shape=jax.ShapeDtypeStruct((n, cols), F32),
        compiler_params=pltpu.CompilerParams(dimension_semantics=("arbitrary",)),
        name="ada",
    )(c, w_ada, b_ada.reshape(1, cols))


def _pre_kernel(prompt, x_ref, sh_ref, sc_ref, n1_ref, win_ref, qn_ref, wqa_ref, wqb_ref, kvn_ref,
                cq_ref, sq_ref, ck_ref, sk_ref, *rest):
    if prompt:
        (wuk_ref, wuvt_ref, lb_ref, gn_ref, qt_ref, k_ref, vt_ref, ckv_ref, kr_ref, ho_ref, sout_ref, gg_ref,
         hh_ref, st_sc, sh_sc) = rest
        tstep = pl.program_id(1)

        @pl.when(tstep == 0)
        def _():
            st_sc[...] = jnp.zeros(st_sc.shape, F32)
    else:
        q_ref, ckv_ref, kr_ref, hh_ref, gg_ref = rest
    x = x_ref[0]
    h = _rms(x, n1_ref[...]) * (1.0 + sc_ref[0]) + sh_ref[0]
    hb = h.astype(BF16)

    def proj(a, b):
        return _dot(hb, win_ref[:, a:b])

    def latents():
        kr_full = proj(_C_KRA, _C_KRB) * ck_ref[...] + proj(_C_KRB, _C_END) * sk_ref[...]
        kr_ref[0] = kr_full[:, ROPE_LANE0:ROPE_LANE0 + QK_ROPE]
        c_kv = _rms(proj(_C_KVLAT, _C_HG), kvn_ref[...])
        ckv_ref[0] = c_kv
        return kr_full, c_kv, _rms(proj(_C_QLAT, _C_KVLAT), qn_ref[...]).astype(BF16)

    if prompt:
        lat = {}

        def gate_cols(lo, hi):
            gg_ref[0, :, lo:hi] = proj(_C_GATE + lo, _C_GATE + hi)

        def do_latents():
            lat["kr"], c_kv, lat["cq"] = latents()
            lat["ckv"] = c_kv.astype(BF16)

        def do_q():
            qa = _nt(wqa_ref[...], lat["cq"])
            qb = _nt(wqb_ref[...], lat["cq"])
            cq = cq_ref[...]
            sq = sq_ref[...]
            for hd in range(MLA_HEADS):
                sl = slice(hd * HEAD_PAD, (hd + 1) * HEAD_PAD)
                qt_ref[0, hd] = ((qa[sl] * cq + qb[sl] * sq) * (MLA_SCALE * LOG2E)).astype(BF16)

        def do_kv():
            kall = _dot(lat["ckv"], wuk_ref[...])
            for hd in range(MLA_HEADS):
                sl = slice(hd * HEAD_PAD, (hd + 1) * HEAD_PAD)
                k_ref[0, hd] = (kall[:, sl] + lat["kr"]).astype(BF16)
            vt = _nt(wuvt_ref[...], lat["ckv"])
            ones_rows = (lax.broadcasted_iota(jnp.int32, (VT_ROWS - V_HEAD, vt.shape[1]), 0) == 0).astype(BF16)
            for hd in range(MLA_HEADS):
                vt_ref[0, hd, :V_HEAD] = vt[hd * V_HEAD:(hd + 1) * V_HEAD].astype(BF16)
                vt_ref[0, hd, V_HEAD:] = ones_rows

        gstep = D_MODEL // 2
        mla_work = [functools.partial(gate_cols, 0, gstep), functools.partial(gate_cols, gstep, 2 * gstep),
                    do_latents, functools.partial(gate_cols, 2 * gstep, 3 * gstep), do_q,
                    functools.partial(gate_cols, 3 * gstep, 4 * gstep), do_kv]
        hh_ref[...] = proj(_C_HG, _C_GATE)

        def load(part, j):
            return hh_ref[j * CHUNK:(j + 1) * CHUNK, part * HG_WIDTH:(part + 1) * HG_WIDTH]

        def store(j, o):
            ho_ref[0, j * CHUNK:(j + 1) * CHUNK, :] = o

        _hgrn_blocks(load, store, lb_ref, gn_ref, st_sc, sh_sc, CHUNK, hh_ref.shape[0] // CHUNK, mla_work)

        @pl.when(tstep == pl.num_programs(1) - 1)
        def _():
            for hd in range(HG_HEADS):
                sout_ref[0, hd] = st_sc[hd].T
    else:
        hh_ref[0] = proj(_C_HG, _C_GATE)
        gg_ref[0] = proj(_C_GATE, _C_KRA)
        _, _, cqb = latents()
        qa = _dot(cqb, wqa_ref[...])
        qb = _dot(cqb, wqb_ref[...])
        cq = cq_ref[...]
        sq = sq_ref[...]
        for hd in range(MLA_HEADS):
            sl = slice(hd * HEAD_PAD, (hd + 1) * HEAD_PAD)
            q_ref[0, :, sl] = ((qa[:, sl] * cq + qb[:, sl] * sq) * MLA_SCALE).astype(BF16)


def _pre(prompt, x, sh1, sc1, wts, tabs):
    bp, tp, _ = x.shape
    r = min(PRE_ROW_TILE if prompt else ROW_TILE, tp)
    nt = tp // r
    mod_rows = sh1.shape[1]
    mod_blk = (1, 1, D_MODEL) if mod_rows == 1 else (1, r, D_MODEL)
    mod_map = (lambda b, t: (b, 0, 0)) if mod_rows == 1 else (lambda b, t: (b, t, 0))
    row3 = lambda w: pl.BlockSpec((1, r, w), lambda b, t: (b, t, 0))
    in_specs = [row3(D_MODEL), pl.BlockSpec(mod_blk, mod_map), pl.BlockSpec(mod_blk, mod_map),
                _const_spec((1, D_MODEL)), _const_spec(wts["w_in"].shape), _const_spec((1, Q_LORA)),
                _const_spec(wts["wqa"].shape), _const_spec(wts["wqb"].shape), _const_spec((1, KV_LORA))]
    if prompt:
        in_specs += [pl.BlockSpec((HEAD_PAD, r), lambda b, t: (0, t))] * 2
    else:
        in_specs += [pl.BlockSpec((r, HEAD_PAD), lambda b, t: (t, 0))] * 2
    in_specs += [pl.BlockSpec((r, HEAD_PAD), lambda b, t: (t, 0))] * 2
    args = [x, sh1, sc1, wts["norm1"], wts["w_in"], wts["q_norm"], wts["wqa"], wts["wqb"], wts["kv_norm"],
            tabs["cq"], tabs["sq"], tabs["ck"], tabs["sk"]]
    f32_rows = lambda w: jax.ShapeDtypeStruct((bp, tp, w), F32)
    state_shape = (bp, HG_HEADS, HG_EXPAND, HG_VDIM)
    scratch = []
    if prompt:
        in_specs += [_const_spec(wts["wuk"].shape), _const_spec(wts["wuvt"].shape),
                     _const_spec((1, HG_WIDTH)), _const_spec((1, HG_VDIM))]
        args += [wts["wuk"], wts["wuvt"], wts["lb"], wts["hg_norm"]]
        out_shape = [jax.ShapeDtypeStruct((bp, MLA_HEADS, HEAD_PAD, tp), BF16),
                     jax.ShapeDtypeStruct((bp, MLA_HEADS, tp, HEAD_PAD), BF16),
                     jax.ShapeDtypeStruct((bp, MLA_HEADS, VT_ROWS, tp), BF16),
                     f32_rows(KV_LORA), f32_rows(QK_ROPE), f32_rows(HG_WIDTH),
                     jax.ShapeDtypeStruct(state_shape, F32), f32_rows(2 * D_MODEL)]
        out_specs = [pl.BlockSpec((1, MLA_HEADS, HEAD_PAD, r), lambda b, t: (b, 0, 0, t)),
                     pl.BlockSpec((1, MLA_HEADS, r, HEAD_PAD), lambda b, t: (b, 0, t, 0)),
                     pl.BlockSpec((1, MLA_HEADS, VT_ROWS, r), lambda b, t: (b, 0, 0, t)),
                     row3(KV_LORA), row3(QK_ROPE), row3(HG_WIDTH),
                     pl.BlockSpec((1,) + state_shape[1:], lambda b, t: (b, 0, 0, 0)), row3(2 * D_MODEL)]
        scratch = [pltpu.VMEM((r, 4 * HG_WIDTH), F32),
                   pltpu.VMEM((HG_HEADS, HG_VDIM, HG_EXPAND), F32),
                   pltpu.VMEM((r // CHUNK, HG_HEADS, 3, HG_SUB + CHUNK, HG_EXPAND), F32)]
    else:
        out_shape = [jax.ShapeDtypeStruct((bp, tp, MLA_HEADS * HEAD_PAD), BF16),
                     f32_rows(KV_LORA), f32_rows(QK_ROPE), f32_rows(4 * HG_WIDTH), f32_rows(2 * D_MODEL)]
        out_specs = [row3(MLA_HEADS * HEAD_PAD), row3(KV_LORA), row3(QK_ROPE), row3(4 * HG_WIDTH),
                     row3(2 * D_MODEL)]
    return pl.pallas_call(
        functools.partial(_pre_kernel, prompt),
        grid=(bp, nt),
        in_specs=in_specs,
        out_specs=out_specs,
        out_shape=out_shape,
        scratch_shapes=scratch,
        compiler_params=pltpu.CompilerParams(dimension_semantics=("arbitrary", "arbitrary"),
                                             vmem_limit_bytes=VMEM_LIMIT),
        name="pre_prompt" if prompt else "pre_sample",
    )(*args)


def _attn_kernel(qi_ref, ki_ref, qt_ref, k_ref, vt_ref, o_ref, m_sc, acc_sc, *slabs, blk, sub):
    p = pl.program_id(1)
    qi = qi_ref[p]
    ki = ki_ref[p]
    ns = blk // sub
    s_bufs, st_bufs = slabs[:ns], slabs[ns:]

    @pl.when(ki == 0)
    def _():
        m_sc[...] = jnp.full(m_sc.shape, -jnp.inf, F32)
        acc_sc[...] = jnp.zeros(acc_sc.shape, F32)

    def run(diag):
        if diag:
            kc = lax.broadcasted_iota(jnp.int32, (sub, sub), 0) // CHUNK
            qc = lax.broadcasted_iota(jnp.int32, (sub, sub), 1) // CHUNK
            visible = kc <= qc

        def rows(qs):
            return (qs + 1) * sub if diag else blk

        def scores(hd, qs):
            ql = slice(qs * sub, (qs + 1) * sub)
            nrow = rows(qs)
            s_buf, st_buf = s_bufs[qs], st_bufs[qs]
            q_t = qt_ref[0, hd, :, ql]
            half = max(sub, (nrow // 2) // sub * sub)
            mx = None
            for lo, hi in ((0, half), (half, nrow)):
                if lo == hi:
                    continue
                s = _dot(k_ref[0, hd, lo:hi, :], q_t)
                if diag and hi == nrow:
                    top = nrow - sub - lo
                    dg = jnp.where(visible, s[top:], -jnp.inf)
                    s_buf[nrow - sub:nrow, :] = dg
                    part = jnp.max(dg, axis=0, keepdims=True)
                    if top > 0:
                        s_buf[lo:nrow - sub, :] = s[:top]
                        part = jnp.maximum(part, jnp.max(s[:top], axis=0, keepdims=True))
                else:
                    s_buf[lo:hi, :] = s
                    part = jnp.max(s, axis=0, keepdims=True)
                mx = part if mx is None else jnp.maximum(mx, part)
            m_old = m_sc[hd, :, ql]
            m_new = jnp.maximum(m_old, mx)
            m_sc[hd, :, ql] = m_new
            st_buf[0] = m_new
            st_buf[1] = jnp.exp2(m_old - m_new)

        def weighted(hd, qs):
            ql = slice(qs * sub, (qs + 1) * sub)
            nrow = rows(qs)
            s_buf, st_buf = s_bufs[qs], st_bufs[qs]
            pm = jnp.exp2(s_buf[:nrow, :] - st_buf[0])
            acc_sc[hd, :, ql] = st_buf[1] * acc_sc[hd, :, ql] + _dot(vt_ref[0, hd, :, :nrow], pm.astype(BF16))

        scores(0, 0)

        for hd in range(MLA_HEADS):
            for qs in range(ns):
                if qs + 1 < ns:
                    scores(hd, qs + 1)
                elif hd + 1 < MLA_HEADS:
                    scores(hd + 1, 0)
                weighted(hd, qs)

    @pl.when(ki < qi)
    def _():
        run(False)

    @pl.when(ki == qi)
    def _():
        run(True)
        o = acc_sc[:, :V_HEAD, :] / acc_sc[:, V_HEAD:V_HEAD + 1, :]
        o_ref[0] = o.reshape(MLA_WIDTH, blk).T.astype(BF16)


def _attn_prompt(qt, k, vt):
    bp, _, _, tp = qt.shape
    blk = min(ATT_BLOCK, tp)
    sub = min(ATT_SUB, blk)
    ns = blk // sub
    nb = tp // blk
    qi_of = np.array([q for q in range(nb) for _ in range(q + 1)], np.int32)
    ki_of = np.array([kk for q in range(nb) for kk in range(q + 1)], np.int32)
    grid_spec = pltpu.PrefetchScalarGridSpec(
        num_scalar_prefetch=2,
        grid=(bp, len(qi_of)),
        in_specs=[pl.BlockSpec((1, MLA_HEADS, HEAD_PAD, blk), lambda b, p, qi, ki: (b, 0, 0, qi[p])),
                  pl.BlockSpec((1, MLA_HEADS, blk, HEAD_PAD), lambda b, p, qi, ki: (b, 0, ki[p], 0)),
                  pl.BlockSpec((1, MLA_HEADS, VT_ROWS, blk), lambda b, p, qi, ki: (b, 0, 0, ki[p]))],
        out_specs=pl.BlockSpec((1, blk, MLA_WIDTH), lambda b, p, qi, ki: (b, qi[p], 0)),
        scratch_shapes=[pltpu.VMEM((MLA_HEADS, 1, blk), F32),
                        pltpu.VMEM((MLA_HEADS, VT_ROWS, blk), F32)]
                       + [pltpu.VMEM((blk, sub), F32)] * ns
                       + [pltpu.VMEM((2, 1, sub), F32)] * ns,
    )
    return pl.pallas_call(
        functools.partial(_attn_kernel, blk=blk, sub=sub),
        grid_spec=grid_spec,
        out_shape=jax.ShapeDtypeStruct((bp, tp, MLA_WIDTH), BF16),
        compiler_params=pltpu.CompilerParams(dimension_semantics=("arbitrary", "arbitrary"),
                                             vmem_limit_bytes=VMEM_LIMIT),
        name="attn_prompt",
    )(jnp.asarray(qi_of), jnp.asarray(ki_of), qt, k, vt)


def _attn_sample_kernel(q_ref, cc_ref, ckr_ref, nc_ref, nkr_ref, wuk_ref, wuvp_ref, place_ref, o_ref):
    t = q_ref.shape[1]
    q = q_ref[0]
    qh = [q[:, hd * HEAD_PAD:(hd + 1) * HEAD_PAD] for hd in range(MLA_HEADS)]
    q_stack = jnp.concatenate(qh, axis=0)
    q_abs = jnp.concatenate(
        [_nt(qh[hd][:, :QK_NOPE], wuk_ref[hd]) for hd in range(MLA_HEADS)], axis=0).astype(BF16)
    place = place_ref[...]
    cc = cc_ref[0, 0].astype(BF16)
    nc = nc_ref[0].astype(BF16)
    ckr = _dot(ckr_ref[0, 0].astype(BF16), place).astype(BF16)
    nkr = _dot(nkr_ref[0].astype(BF16), place).astype(BF16)
    s_c = _nt(q_abs, cc) + _nt(q_stack, ckr)
    s_n = _nt(q_abs, nc) + _nt(q_stack, nkr)
    m = jnp.maximum(jnp.max(s_c, axis=-1, keepdims=True), jnp.max(s_n, axis=-1, keepdims=True))
    p_c = jnp.exp(s_c - m)
    p_n = jnp.exp(s_n - m)
    l = jnp.sum(p_c, axis=-1, keepdims=True) + jnp.sum(p_n, axis=-1, keepdims=True)
    o_lat = (_dot(p_c.astype(BF16), cc) + _dot(p_n.astype(BF16), nc)) / l
    o_lat = o_lat.astype(BF16)
    out = jnp.zeros((t, MLA_WIDTH), F32)
    for hd in range(MLA_HEADS):
        out = out + _dot(o_lat[hd * t:(hd + 1) * t], wuvp_ref[hd])
    o_ref[0] = out.astype(BF16)


def _attn_sample(q, cache_ckv, cache_kr, new_ckv, new_kr, wts):
    nb, t, _ = q.shape
    past = cache_ckv.shape[2]
    place = np.zeros((QK_ROPE, HEAD_PAD), np.float32)
    place[np.arange(QK_ROPE), ROPE_LANE0 + np.arange(QK_ROPE)] = 1.0
    return pl.pallas_call(
        _attn_sample_kernel,
        grid=(nb,),
        in_specs=[pl.BlockSpec((1, t, MLA_HEADS * HEAD_PAD), lambda b: (b, 0, 0)),
                  pl.BlockSpec((1, 1, past, KV_LORA), lambda b: (0, b, 0, 0)),
                  pl.BlockSpec((1, 1, past, QK_ROPE), lambda b: (0, b, 0, 0)),
                  pl.BlockSpec((1, t, KV_LORA), lambda b: (b, 0, 0)),
                  pl.BlockSpec((1, t, QK_ROPE), lambda b: (b, 0, 0)),
                  _const_spec(wts["wuk3"].shape), _const_spec(wts["wuvp"].shape),
                  _const_spec((QK_ROPE, HEAD_PAD))],
        out_specs=pl.BlockSpec((1, t, MLA_WIDTH), lambda b: (b, 0, 0)),
        out_shape=jax.ShapeDtypeStruct((nb, t, MLA_WIDTH), BF16),
        compiler_params=pltpu.CompilerParams(dimension_semantics=("arbitrary",),
                                             vmem_limit_bytes=VMEM_LIMIT),
        name="attn_sample",
    )(q, cache_ckv, cache_kr, new_ckv, new_kr, wts["wuk3"], wts["wuvp"], jnp.asarray(place, BF16))


def _split3(x):
    hi = x.astype(BF16)
    r1 = x - hi.astype(F32)
    mid = r1.astype(BF16)
    lo = (r1 - mid.astype(F32)).astype(BF16)
    return hi, mid, lo


def _hgrn_blocks(load, store, lb_ref, gn_ref, st_sc, sh_sc, L, nb, fillers=()):
    fillers = list(fillers)

    def fill(count):
        for _ in range(min(count, len(fillers))):
            fillers.pop(0)()

    heads = range(HG_HEADS)
    rmod = lax.broadcasted_iota(jnp.int32, (L, 1), 0) % HG_SUB
    tril = (lax.broadcasted_iota(jnp.int32, (L, L), 0) >= lax.broadcasted_iota(jnp.int32, (L, L), 1)
            ).astype(BF16)
    nsub = L // HG_SUB
    hsl = lambda a, hd: a[:, hd * HG_EXPAND:(hd + 1) * HG_EXPAND]
    tile = lambda a, i: a[HG_SUB * i:HG_SUB * (i + 1)]
    zero_tile = jnp.zeros((HG_SUB, HG_EXPAND), F32)
    lb = lb_ref[...]
    gain = gn_ref[...]
    sh_sc[:, :, :, :HG_SUB, :] = jnp.zeros((nb, HG_HEADS, 3, HG_SUB, HG_EXPAND), F32)

    blocks = []
    for j in range(nb):
        hq, hf, v, hg = (load(part, j) for part in range(4))
        sg = _sigmoid(hf)
        g = jnp.log2(lb + (1.0 - lb) * sg)
        blocks.append(dict(v=v, hg=hg,
                           kk=(1.0 - lb) * (1.0 - sg),
                           qq=hq * _sigmoid(hq) * HG_SCALE,
                           g3=_split3(g)))
    for blk in blocks:
        g_hi, g_mid, g_lo = blk.pop("g3")
        blk["bc"] = _dot(tril, g_hi) + _dot(tril, g_mid) + _dot(tril, g_lo)
    fill(2)
    for blk in blocks:
        bc, qq, kk = blk["bc"], blk["qq"], blk["kk"]
        last = bc[L - 1:L, :]
        blk["qdec"] = (qq * jnp.exp2(bc)).astype(BF16)
        blk["kdec"] = (kk * jnp.exp2(last - bc)).astype(BF16)
        blk["dec"] = jnp.exp2(last)
        blk["vb"] = blk["v"].astype(BF16)
        if nsub > 1:
            blk["q2"], blk["k2"] = [], []
            for hd in heads:
                qh, kh, bh = hsl(qq, hd), hsl(kk, hd), hsl(bc, hd)
                q_rows = [jnp.concatenate([zero_tile] * (nsub - 1), axis=1)]
                k_cols = []
                for i in range(1, nsub):
                    b_i = bh[HG_SUB * i - 1:HG_SUB * i, :]
                    q_i = tile(qh, i) * jnp.exp2(tile(bh, i) - b_i)
                    q_rows.append(jnp.concatenate([zero_tile] * (i - 1) + [q_i] + [zero_tile] * (nsub - 1 - i),
                                                  axis=1))
                    k_i = kh[:HG_SUB * i] * jnp.exp2(b_i - bh[:HG_SUB * i])
                    k_cols.append(jnp.concatenate([k_i] + [zero_tile] * (nsub - i), axis=0))
                blk["q2"].append(jnp.concatenate(q_rows, axis=0).astype(BF16))
                blk["k2"].append(jnp.concatenate(k_cols, axis=1).astype(BF16))
    for blk in blocks:
        if nsub > 1:
            blk["att"] = [_nt(q2, k2).astype(BF16) for q2, k2 in zip(blk.pop("q2"), blk.pop("k2"))]
        blk["upd"] = [_tn(hsl(blk["vb"], hd), hsl(blk["kdec"], hd)) for hd in heads]
    fill(2)
    for j, blk in enumerate(blocks):
        blk["od"] = []
        for hd in heads:
            bc, qq, kk, v = (hsl(blk[name], hd) for name in ("bc", "qq", "kk", "v"))
            sh_sc[j, hd, 0, HG_SUB:, :] = kk
            sh_sc[j, hd, 1, HG_SUB:, :] = bc
            sh_sc[j, hd, 2, HG_SUB:, :] = v
            od = jnp.sum(qq * kk, axis=1, keepdims=True) * v
            for d in range(1, HG_SUB):
                shifted = pl.ds(HG_SUB - d, L)
                e = jnp.exp2(jnp.where(rmod >= d, bc - sh_sc[j, hd, 1, shifted, :], -jnp.inf))
                od = od + (jnp.sum(qq * sh_sc[j, hd, 0, shifted, :] * e, axis=1, keepdims=True)
                           * sh_sc[j, hd, 2, shifted, :])
            blk["od"].append(od)
    for blk in blocks:
        blk["st"] = []
    for hd in heads:
        st = st_sc[hd]
        for blk in blocks:
            blk["st"].append(st.astype(BF16))
            st = st * hsl(blk["dec"], hd) + blk["upd"][hd]
        st_sc[hd] = st
    for blk in blocks:
        o = [_nt(hsl(blk["qdec"], hd), blk["st"][hd]) for hd in heads]
        if nsub > 1:
            o = [o[hd] + _dot(blk["att"][hd], hsl(blk["vb"], hd)) for hd in heads]
        blk["o"] = o
    fill(len(fillers))
    for j, blk in enumerate(blocks):
        hg = blk["hg"]
        gate = hg * _sigmoid(hg)
        store(j, jnp.concatenate([_rms(blk["o"][hd] + blk["od"][hd], gain) * hsl(gate, hd) for hd in heads],
                                 axis=1))


def _hgrn_kernel(hh_ref, lb_ref, gn_ref, s0_ref, o_ref, sout_ref, st_sc, sh_sc, *, L, nb):
    tstep = pl.program_id(1)

    @pl.when(tstep == 0)
    def _():
        for hd in range(HG_HEADS):
            st_sc[hd] = s0_ref[0, hd].T

    def load(part, j):
        return hh_ref[0, j * L:(j + 1) * L, part * HG_WIDTH:(part + 1) * HG_WIDTH]

    def store(j, o):
        o_ref[0, j * L:(j + 1) * L, :] = o

    _hgrn_blocks(load, store, lb_ref, gn_ref, st_sc, sh_sc, L, nb)

    @pl.when(tstep == pl.num_programs(1) - 1)
    def _():
        for hd in range(HG_HEADS):
            sout_ref[0, hd] = st_sc[hd].T


def _hgrn(hh, lb, hg_norm, s0, L):
    bp, tp, _ = hh.shape
    tt = min(HG_TIME_TILE, tp)
    nb = tt // L
    return pl.pallas_call(
        functools.partial(_hgrn_kernel, L=L, nb=nb),
        grid=(bp, tp // tt),
        in_specs=[pl.BlockSpec((1, tt, 4 * HG_WIDTH), lambda b, t: (b, t, 0)),
                  _const_spec((1, HG_WIDTH)), _const_spec((1, HG_VDIM)),
                  pl.BlockSpec((1, HG_HEADS, HG_EXPAND, HG_VDIM), lambda b, t: (b, 0, 0, 0))],
        out_specs=[pl.BlockSpec((1, tt, HG_WIDTH), lambda b, t: (b, t, 0)),
                   pl.BlockSpec((1, HG_HEADS, HG_EXPAND, HG_VDIM), lambda b, t: (b, 0, 0, 0))],
        out_shape=[jax.ShapeDtypeStruct((bp, tp, HG_WIDTH), F32),
                   jax.ShapeDtypeStruct((bp, HG_HEADS, HG_EXPAND, HG_VDIM), F32)],
        scratch_shapes=[pltpu.VMEM((HG_HEADS, HG_VDIM, HG_EXPAND), F32),
                        pltpu.VMEM((nb, HG_HEADS, 3, HG_SUB + L, HG_EXPAND), F32)],
        compiler_params=pltpu.CompilerParams(dimension_semantics=("arbitrary", "arbitrary"),
                                             vmem_limit_bytes=VMEM_LIMIT),
        name="hgrn",
    )(hh, lb, hg_norm, s0)


def _post_kernel(x_ref, at_ref, ho_ref, gg_ref, g1_ref, sh2_ref, sc2_ref, g2_ref,
                 wpa_ref, wpb_ref, wout_ref, n2_ref, wgu_ref, wdown_ref, fn_ref, y_ref):
    x = x_ref[0]
    ya = _dot(at_ref[0], wpa_ref[...])
    yb = _dot(ho_ref[0].astype(BF16), wpb_ref[...])
    mixed = _sigmoid(gg_ref[0, :, :D_MODEL]) * ya + _sigmoid(gg_ref[0, :, D_MODEL:]) * yb
    x1 = x + g1_ref[0] * _dot(mixed.astype(BF16), wout_ref[...])
    hb = (_rms(x1, n2_ref[...]) * (1.0 + sc2_ref[0]) + sh2_ref[0]).astype(BF16)
    ff = jnp.zeros(x.shape, F32)
    for c in range(D_FF // FF_CHUNK):
        gate = _dot(hb, wgu_ref[:, c * FF_CHUNK:(c + 1) * FF_CHUNK])
        up = _dot(hb, wgu_ref[:, D_FF + c * FF_CHUNK:D_FF + (c + 1) * FF_CHUNK])
        act = (gate * _sigmoid(gate) * up).astype(BF16)
        ff = ff + _dot(act, wdown_ref[c * FF_CHUNK:(c + 1) * FF_CHUNK, :])
    x2 = x1 + g2_ref[0] * ff
    y_ref[0] = _rms(x2, fn_ref[...])


def _post(x, attn, ho, gg, g1, sh2, sc2, g2, wts):
    bp, tp, _ = x.shape
    r = min(ROW_TILE, tp)
    mod_rows = g1.shape[1]
    mod_blk = (1, 1, D_MODEL) if mod_rows == 1 else (1, r, D_MODEL)
    mod_map = (lambda b, t: (b, 0, 0)) if mod_rows == 1 else (lambda b, t: (b, t, 0))
    row3 = lambda w: pl.BlockSpec((1, r, w), lambda b, t: (b, t, 0))
    mod = pl.BlockSpec(mod_blk, mod_map)
    names = ["wpa", "wpb", "wout", "norm2", "wgu", "wdown", "final_norm"]
    return pl.pallas_call(
        _post_kernel,
        grid=(bp, tp // r),
        in_specs=[row3(D_MODEL), row3(MLA_WIDTH), row3(HG_WIDTH), row3(2 * D_MODEL), mod, mod, mod, mod]
                 + [_const_spec(wts[n].shape) for n in names],
        out_specs=row3(D_MODEL),
        out_shape=jax.ShapeDtypeStruct((bp, tp, D_MODEL), F32),
        compiler_params=pltpu.CompilerParams(dimension_semantics=("arbitrary", "arbitrary"),
                                             vmem_limit_bytes=VMEM_LIMIT),
        name="post",
    )(x, attn, ho, gg, g1, sh2, sc2, g2, *[wts[n] for n in names])


def _prep_weights(w_in, q_norm, w_uq, kv_norm, w_ukv, lb_param, hg_norm, w_pa, w_pb, w_out,
                  norm1, norm2, w_gu, w_down, final_norm):
    half = QK_ROPE // 2
    offs = np.cumsum([0, Q_LORA, KV_LORA, QK_ROPE, HG_WIDTH, HG_WIDTH, HG_WIDTH, HG_WIDTH, D_MODEL, D_MODEL])
    w = w_in[0]
    w_kr = w[:, offs[2]:offs[3]]
    zeros = lambda n: jnp.zeros((D_MODEL, n), w.dtype)
    kr_a = jnp.concatenate([zeros(ROPE_LANE0), w_kr, zeros(LANES - ROPE_LANE0 - QK_ROPE)], axis=1)
    kr_b = jnp.concatenate([zeros(ROPE_LANE0), w_kr[:, half:], w_kr[:, :half],
                            zeros(LANES - ROPE_LANE0 - QK_ROPE)], axis=1)
    w_in_p = jnp.concatenate([w[:, offs[0]:offs[2]], w[:, offs[3]:offs[9]], kr_a, kr_b], axis=1).astype(BF16)

    uq = w_uq[0].reshape(Q_LORA, MLA_HEADS, QK_NOPE + QK_ROPE)
    pad = jnp.zeros((Q_LORA, MLA_HEADS, HEAD_PAD - QK_NOPE - QK_ROPE), uq.dtype)
    wqa = jnp.concatenate([uq, pad], axis=2).reshape(Q_LORA, MLA_HEADS * HEAD_PAD).astype(BF16)
    wqb = jnp.concatenate([jnp.zeros((Q_LORA, MLA_HEADS, QK_NOPE), uq.dtype),
                           uq[:, :, QK_NOPE + half:], uq[:, :, QK_NOPE:QK_NOPE + half], pad],
                          axis=2).reshape(Q_LORA, MLA_HEADS * HEAD_PAD).astype(BF16)

    ukv = w_ukv[0].reshape(KV_LORA, MLA_HEADS, QK_NOPE + V_HEAD)
    uk = ukv[:, :, :QK_NOPE]
    uv = ukv[:, :, QK_NOPE:]
    wuk = jnp.concatenate([uk, jnp.zeros((KV_LORA, MLA_HEADS, HEAD_PAD - QK_NOPE), uk.dtype)],
                          axis=2).reshape(KV_LORA, MLA_HEADS * HEAD_PAD).astype(BF16)
    wuvt = uv.reshape(KV_LORA, MLA_WIDTH).T.astype(BF16)
    wuk3 = jnp.transpose(uk, (1, 0, 2)).astype(BF16)
    eye = jnp.eye(MLA_HEADS, dtype=uv.dtype)
    wuvp = (jnp.transpose(uv, (1, 0, 2))[:, :, None, :] * eye[:, None, :, None]
            ).reshape(MLA_HEADS, KV_LORA, MLA_WIDTH).astype(BF16)

    lb = jnp.cumsum(jax.nn.softmax(lb_param.astype(F32), axis=0), axis=0)[0].reshape(1, HG_WIDTH)
    return {
        "w_in": w_in_p, "wqa": wqa, "wqb": wqb, "wqa_t": wqa.T, "wqb_t": wqb.T,
        "wuk": wuk, "wuvt": wuvt, "wuk3": wuk3, "wuvp": wuvp, "lb": lb,
        "q_norm": q_norm[0].reshape(1, Q_LORA), "kv_norm": kv_norm[0].reshape(1, KV_LORA),
        "hg_norm": hg_norm[0].reshape(1, HG_VDIM),
        "norm1": norm1[0].reshape(1, D_MODEL), "norm2": norm2[0].reshape(1, D_MODEL),
        "final_norm": final_norm.reshape(1, D_MODEL),
        "wpa": w_pa[0].astype(BF16), "wpb": w_pb[0].astype(BF16), "wout": w_out[0].astype(BF16),
        "wgu": w_gu[0].astype(BF16), "wdown": w_down[0].astype(BF16),
    }


def _rope_tables(pos):
    half = QK_ROPE // 2
    inv = ROPE_THETA ** (-np.arange(half, dtype=np.float64) / half)
    ang = np.asarray(pos, np.float64)[:, None] * inv[None, :]
    cos, sin = np.cos(ang), np.sin(ang)
    t = len(pos)
    ck = np.zeros((t, HEAD_PAD))
    sk = np.zeros((t, HEAD_PAD))
    ck[:, ROPE_LANE0:ROPE_LANE0 + QK_ROPE] = np.concatenate([cos, cos], axis=1)
    sk[:, ROPE_LANE0:ROPE_LANE0 + QK_ROPE] = np.concatenate([-sin, sin], axis=1)
    cq = ck.copy()
    cq[:, :QK_NOPE] = 1.0
    f = lambda a: jnp.asarray(a.astype(np.float32))
    return {"ck": f(ck), "sk": f(sk), "cq": f(cq), "sq": f(sk)}


def _trunk(prompt, x, mod, wts, tabs, s0=None, cache=None):
    sh1, sc1, g1, sh2, sc2, g2 = mod
    if prompt:
        w = dict(wts, wqa=wts["wqa_t"], wqb=wts["wqb_t"])
        t = dict(tabs, cq=tabs["cq"].T, sq=tabs["sq"].T)
        qt, k, vt, ckv, kr, ho, s_new, gg = _pre(True, x, sh1, sc1, w, t)
        attn = _attn_prompt(qt, k, vt)
    else:
        q, ckv, kr, hh, gg = _pre(False, x, sh1, sc1, wts, tabs)
        nb, t_new = cache[0].shape[1], x.shape[1] // cache[0].shape[1]
        attn = _attn_sample(q.reshape(nb, t_new, -1), cache[0], cache[1],
                            ckv.reshape(nb, t_new, KV_LORA), kr.reshape(nb, t_new, QK_ROPE), wts)
        attn = attn.reshape(1, nb * t_new, MLA_WIDTH)
        ho, s_new = _hgrn(hh.reshape(nb, t_new, 4 * HG_WIDTH), wts["lb"], wts["hg_norm"], s0, t_new)
        ho = ho.reshape(x.shape[0], x.shape[1], HG_WIDTH)
    y = _post(x, attn, ho, gg, g1, sh2, sc2, g2, wts)
    return y, ckv, kr, s_new


def kernel(x_prompt, x_sample, cache_ckv, cache_krope, state_hgrn, c_prompt, c_sample, w_in, q_norm, w_uq,
           kv_norm, w_ukv, lb_param, hg_norm, w_pa, w_pb, w_out, norm1, norm2, w_ada, b_ada, w_gu, w_down,
           final_norm):
    bsz, t_p, _ = x_prompt.shape
    nb, t_s, _ = x_sample.shape
    past = cache_ckv.shape[2]
    assert past % CHUNK == 0 and t_s <= CHUNK
    wts = _prep_weights(w_in, q_norm, w_uq, kv_norm, w_ukv, lb_param, hg_norm, w_pa, w_pb, w_out,
                        norm1, norm2, w_gu, w_down, final_norm)

    n_c = bsz + nb
    n_pad = -n_c % 8
    c_all = jnp.concatenate([c_prompt, c_sample, jnp.zeros((n_pad, D_MODEL), F32)], axis=0)
    mod_all = _ada(c_all, w_ada[0], b_ada[0])
    mod_p = [m[:, None, :] for m in jnp.split(mod_all[:bsz], 6, axis=-1)]
    mod_s = [jnp.repeat(m, t_s, axis=0)[None] for m in jnp.split(mod_all[bsz:n_c], 6, axis=-1)]

    tabs_p = _rope_tables(np.arange(t_p))
    tabs_s = _rope_tables(np.tile(past + np.arange(t_s), nb))

    y_p, ckv_p, kr_p, s_p = _trunk(True, x_prompt, mod_p, wts, tabs_p)
    y_s, ckv_s, kr_s, s_s = _trunk(False, x_sample.reshape(1, nb * t_s, D_MODEL), mod_s, wts, tabs_s,
                                   s0=state_hgrn[0], cache=(cache_ckv, cache_krope))
    return (y_p, y_s.reshape(nb, t_s, D_MODEL), ckv_p[None], kr_p[None], s_p[None],
            ckv_s.reshape(nb, t_s, KV_LORA)[None], kr_s.reshape(nb, t_s, QK_ROPE)[None], s_s[None])
```

```python
import functools
import math

import numpy as np
import jax
import jax.numpy as jnp
from jax import lax
from jax.experimental import pallas as pl
from jax.experimental.pallas import tpu as pltpu

F32 = jnp.float32
BF16 = jnp.bfloat16

D_MODEL = 1024
CHUNK = 64
EPS = 1e-6
MLA_HEADS = 8
QK_NOPE = 64
QK_ROPE = 32
V_HEAD = 64
Q_LORA = 384
KV_LORA = 256
ROPE_THETA = 10000.0
MLA_WIDTH = MLA_HEADS * V_HEAD
MLA_SCALE = 1.0 / math.sqrt(QK_NOPE + QK_ROPE)
HG_HEADS = 4
HG_EXPAND = 128
HG_VDIM = 128
HG_WIDTH = HG_HEADS * HG_EXPAND
HG_SCALE = 1.0 / math.sqrt(HG_EXPAND)
D_FF = 2816
LOG2E = math.log2(math.e)

LANES = 128
HEAD_PAD = LANES
ROPE_LANE0 = QK_NOPE
VT_ROWS = V_HEAD + 16
VMEM_LIMIT = 56 * 1024 * 1024

ROW_TILE = 512
PRE_ROW_TILE = 256
ATT_BLOCK = 1024
ATT_KV_BLOCK = 2048
ATT_SUB = 256
HG_SUB = 8
HG_TIME_TILE = 512
FF_CHUNK = 1408

_C_QLAT = 0
_C_KVLAT = _C_QLAT + Q_LORA
_C_HG = _C_KVLAT + KV_LORA
_C_GATE = _C_HG + 4 * HG_WIDTH
_C_KRA = _C_GATE + 2 * D_MODEL
_C_KRB = _C_KRA + LANES
_C_END = _C_KRB + LANES


def _const_spec(shape):
    nd = len(shape)
    return pl.BlockSpec(shape, lambda *_: (0,) * nd, pipeline_mode=pl.Buffered(1))


def _nt(a, b):
    return lax.dot_general(a, b, (((1,), (1,)), ((), ())), preferred_element_type=F32)


def _tn(a, b):
    return lax.dot_general(a, b, (((0,), (0,)), ((), ())), preferred_element_type=F32)


def _dot(a, b):
    return jnp.dot(a, b, preferred_element_type=F32)


def _rms(x, w):
    return x * lax.rsqrt(jnp.mean(x * x, axis=-1, keepdims=True) + EPS) * w


def _sigmoid(x):
    return 1.0 / (1.0 + jnp.exp(-x))


def _ada_kernel(c_ref, w_ref, b_ref, o_ref):
    c = c_ref[...]
    a = (c * _sigmoid(c)).astype(BF16)
    o_ref[...] = _dot(a, w_ref[...].astype(BF16)) + b_ref[...]


def _ada(c, w_ada, b_ada):
    n = c.shape[0]
    cols = w_ada.shape[1]
    blk = D_MODEL
    return pl.pallas_call(
        _ada_kernel,
        grid=(cols // blk,),
        in_specs=[pl.BlockSpec((n, D_MODEL), lambda j: (0, 0)),
                  pl.BlockSpec((D_MODEL, blk), lambda j: (0, j)),
                  pl.BlockSpec((1, blk), lambda j: (0, j))],
        out_specs=pl.BlockSpec((n, blk), lambda j: (0, j)),
        out_shape=jax.ShapeDtypeStruct((n, cols), F32),
        compiler_params=pltpu.CompilerParams(dimension_semantics=("arbitrary",)),
        name="ada",
    )(c, w_ada, b_ada.reshape(1, cols))


def _pre_kernel(prompt, x_ref, sh_ref, sc_ref, n1_ref, win_ref, qn_ref, wqa_ref, wqb_ref, kvn_ref,
                cq_ref, sq_ref, ck_ref, sk_ref, *rest):
    if prompt:
        (wuk_ref, wuvt_ref, lb_ref, gn_ref, qt_ref, k_ref, vt_ref, ckv_ref, kr_ref, ho_ref, sout_ref, gg_ref,
         hh_ref, st_sc, sh_sc) = rest
        tstep = pl.program_id(1)

        @pl.when(tstep == 0)
        def _():
            st_sc[...] = jnp.zeros(st_sc.shape, F32)
    else:
        q_ref, ckv_ref, kr_ref, hh_ref, gg_ref = rest
    x = x_ref[0]
    h = _rms(x, n1_ref[...]) * (1.0 + sc_ref[0]) + sh_ref[0]
    hb = h.astype(BF16)

    def proj(a, b):
        return _dot(hb, win_ref[:, a:b])

    def latents():
        kr_full = proj(_C_KRA, _C_KRB) * ck_ref[...] + proj(_C_KRB, _C_END) * sk_ref[...]
        kr_ref[0] = kr_full[:, ROPE_LANE0:ROPE_LANE0 + QK_ROPE]
        c_kv = _rms(proj(_C_KVLAT, _C_HG), kvn_ref[...])
        ckv_ref[0] = c_kv
        return kr_full, c_kv, _rms(proj(_C_QLAT, _C_KVLAT), qn_ref[...]).astype(BF16)

    if prompt:
        lat = {}

        def gate_cols(lo, hi):
            gg_ref[0, :, lo:hi] = proj(_C_GATE + lo, _C_GATE + hi)

        def do_latents():
            lat["kr"], c_kv, lat["cq"] = latents()
            lat["ckv"] = c_kv.astype(BF16)

        def do_q():
            qa = _nt(wqa_ref[...], lat["cq"])
            cq = cq_ref[...]
            sq = sq_ref[...]
            half = QK_ROPE // 2
            r0, r1, r2 = ROPE_LANE0, ROPE_LANE0 + half, ROPE_LANE0 + QK_ROPE
            for hd in range(MLA_HEADS):
                qh = qa[hd * HEAD_PAD:(hd + 1) * HEAD_PAD]
                qsw = jnp.concatenate([qh[:r0], qh[r1:r2], qh[r0:r1], qh[r2:]], axis=0)
                qt_ref[0, hd] = ((qh * cq + qsw * sq) * (MLA_SCALE * LOG2E)).astype(BF16)

        def do_kv():
            kall = _dot(lat["ckv"], wuk_ref[...])
            for hd in range(MLA_HEADS):
                sl = slice(hd * HEAD_PAD, (hd + 1) * HEAD_PAD)
                k_ref[0, hd] = (kall[:, sl] + lat["kr"]).astype(BF16)
            vt = _nt(wuvt_ref[...], lat["ckv"])
            ones_rows = (lax.broadcasted_iota(jnp.int32, (VT_ROWS - V_HEAD, vt.shape[1]), 0) == 0).astype(BF16)
            for hd in range(MLA_HEADS):
                vt_ref[0, hd, :V_HEAD] = vt[hd * V_HEAD:(hd + 1) * V_HEAD].astype(BF16)
                vt_ref[0, hd, V_HEAD:] = ones_rows

        gstep = D_MODEL // 2
        mla_work = [functools.partial(gate_cols, 0, gstep), functools.partial(gate_cols, gstep, 2 * gstep),
                    do_latents, functools.partial(gate_cols, 2 * gstep, 3 * gstep), do_q,
                    functools.partial(gate_cols, 3 * gstep, 4 * gstep), do_kv]
        hh_ref[...] = proj(_C_HG, _C_GATE)

        def load(part, j):
            return hh_ref[j * CHUNK:(j + 1) * CHUNK, part * HG_WIDTH:(part + 1) * HG_WIDTH]

        def store(j, o):
            ho_ref[0, j * CHUNK:(j + 1) * CHUNK, :] = o

        _hgrn_blocks(load, store, lb_ref, gn_ref, st_sc, sh_sc, CHUNK, hh_ref.shape[0] // CHUNK, mla_work)

        @pl.when(tstep == pl.num_programs(1) - 1)
        def _():
            for hd in range(HG_HEADS):
                sout_ref[0, hd] = st_sc[hd].T
    else:
        hh_ref[0] = proj(_C_HG, _C_GATE)
        gg_ref[0] = proj(_C_GATE, _C_KRA)
        _, _, cqb = latents()
        qa = _dot(cqb, wqa_ref[...])
        qb = _dot(cqb, wqb_ref[...])
        cq = cq_ref[...]
        sq = sq_ref[...]
        for hd in range(MLA_HEADS):
            sl = slice(hd * HEAD_PAD, (hd + 1) * HEAD_PAD)
            q_ref[0, :, sl] = ((qa[:, sl] * cq + qb[:, sl] * sq) * MLA_SCALE).astype(BF16)


def _pre(prompt, x, sh1, sc1, wts, tabs):
    bp, tp, _ = x.shape
    r = min(PRE_ROW_TILE if prompt else ROW_TILE, tp)
    nt = tp // r
    mod_rows = sh1.shape[1]
    mod_blk = (1, 1, D_MODEL) if mod_rows == 1 else (1, r, D_MODEL)
    mod_map = (lambda b, t: (b, 0, 0)) if mod_rows == 1 else (lambda b, t: (b, t, 0))
    row3 = lambda w: pl.BlockSpec((1, r, w), lambda b, t: (b, t, 0))
    in_specs = [row3(D_MODEL), pl.BlockSpec(mod_blk, mod_map), pl.BlockSpec(mod_blk, mod_map),
                _const_spec((1, D_MODEL)), _const_spec(wts["w_in"].shape), _const_spec((1, Q_LORA)),
                _const_spec(wts["wqa"].shape), _const_spec(wts["wqb"].shape), _const_spec((1, KV_LORA))]
    if prompt:
        in_specs += [pl.BlockSpec((HEAD_PAD, r), lambda b, t: (0, t))] * 2
    else:
        in_specs += [pl.BlockSpec((r, HEAD_PAD), lambda b, t: (t, 0))] * 2
    in_specs += [pl.BlockSpec((r, HEAD_PAD), lambda b, t: (t, 0))] * 2
    args = [x, sh1, sc1, wts["norm1"], wts["w_in"], wts["q_norm"], wts["wqa"], wts["wqb"], wts["kv_norm"],
            tabs["cq"], tabs["sq"], tabs["ck"], tabs["sk"]]
    f32_rows = lambda w: jax.ShapeDtypeStruct((bp, tp, w), F32)
    state_shape = (bp, HG_HEADS, HG_EXPAND, HG_VDIM)
    scratch = []
    if prompt:
        in_specs += [_const_spec(wts["wuk"].shape), _const_spec(wts["wuvt"].shape),
                     _const_spec((1, HG_WIDTH)), _const_spec((1, HG_VDIM))]
        args += [wts["wuk"], wts["wuvt"], wts["lb"], wts["hg_norm"]]
        out_shape = [jax.ShapeDtypeStruct((bp, MLA_HEADS, HEAD_PAD, tp), BF16),
                     jax.ShapeDtypeStruct((bp, MLA_HEADS, tp, HEAD_PAD), BF16),
                     jax.ShapeDtypeStruct((bp, MLA_HEADS, VT_ROWS, tp), BF16),
                     f32_rows(KV_LORA), f32_rows(QK_ROPE), f32_rows(HG_WIDTH),
                     jax.ShapeDtypeStruct(state_shape, F32), f32_rows(2 * D_MODEL)]
        out_specs = [pl.BlockSpec((1, MLA_HEADS, HEAD_PAD, r), lambda b, t: (b, 0, 0, t)),
                     pl.BlockSpec((1, MLA_HEADS, r, HEAD_PAD), lambda b, t: (b, 0, t, 0)),
                     pl.BlockSpec((1, MLA_HEADS, VT_ROWS, r), lambda b, t: (b, 0, 0, t)),
                     row3(KV_LORA), row3(QK_ROPE), row3(HG_WIDTH),
                     pl.BlockSpec((1,) + state_shape[1:], lambda b, t: (b, 0, 0, 0)), row3(2 * D_MODEL)]
        scratch = [pltpu.VMEM((r, 4 * HG_WIDTH), F32),
                   pltpu.VMEM((HG_HEADS, HG_VDIM, HG_EXPAND), F32),
                   pltpu.VMEM((r // CHUNK, HG_HEADS, 3, HG_SUB + CHUNK, HG_EXPAND), F32)]
    else:
        out_shape = [jax.ShapeDtypeStruct((bp, tp, MLA_HEADS * HEAD_PAD), BF16),
                     f32_rows(KV_LORA), f32_rows(QK_ROPE), f32_rows(4 * HG_WIDTH), f32_rows(2 * D_MODEL)]
        out_specs = [row3(MLA_HEADS * HEAD_PAD), row3(KV_LORA), row3(QK_ROPE), row3(4 * HG_WIDTH),
                     row3(2 * D_MODEL)]
    return pl.pallas_call(
        functools.partial(_pre_kernel, prompt),
        grid=(bp, nt),
        in_specs=in_specs,
        out_specs=out_specs,
        out_shape=out_shape,
        scratch_shapes=scratch,
        compiler_params=pltpu.CompilerParams(dimension_semantics=("arbitrary", "arbitrary"),
                                             vmem_limit_bytes=VMEM_LIMIT),
        name="pre_prompt" if prompt else "pre_sample",
    )(*args)


def _attn_kernel(qi_ref, kj_ref, kind_ref, qt_ref, k_ref, vt_ref, o_ref, m_sc, acc_sc, *slabs, blk, kblk, sub):
    p = pl.program_id(1)
    kj = kj_ref[p]
    kind = kind_ref[p]
    ns = blk // sub
    s_bufs, st_bufs = slabs[:ns], slabs[ns:]

    @pl.when(kj == 0)
    def _():
        m_sc[...] = jnp.full(m_sc.shape, -jnp.inf, F32)
        acc_sc[...] = jnp.zeros(acc_sc.shape, F32)

    def run(part):
        diag = part is not None
        if diag:
            kc = lax.broadcasted_iota(jnp.int32, (sub, sub), 0) // CHUNK
            qc = lax.broadcasted_iota(jnp.int32, (sub, sub), 1) // CHUNK
            visible = kc <= qc

        def rows(qs):
            return part * blk + (qs + 1) * sub if diag else kblk

        def scores(hd, qs):
            ql = slice(qs * sub, (qs + 1) * sub)
            nrow = rows(qs)
            s_buf, st_buf = s_bufs[qs], st_bufs[qs]
            q_t = qt_ref[0, hd, :, ql]
            half = max(sub, (nrow // 2) // sub * sub)
            mx = None
            for lo, hi in ((0, half), (half, nrow)):
                if lo == hi:
                    continue
                s = _dot(k_ref[0, hd, lo:hi, :], q_t)
                if diag and hi == nrow:
                    top = nrow - sub - lo
                    dg = jnp.where(visible, s[top:], -jnp.inf)
                    s_buf[nrow - sub:nrow, :] = dg
                    part = jnp.max(dg, axis=0, keepdims=True)
                    if top > 0:
                        s_buf[lo:nrow - sub, :] = s[:top]
                        part = jnp.maximum(part, jnp.max(s[:top], axis=0, keepdims=True))
                else:
                    s_buf[lo:hi, :] = s
                    part = jnp.max(s, axis=0, keepdims=True)
                mx = part if mx is None else jnp.maximum(mx, part)
            m_old = m_sc[hd, :, ql]
            m_new = jnp.maximum(m_old, mx)
            m_sc[hd, :, ql] = m_new
            st_buf[0] = m_new
            st_buf[1] = jnp.exp2(m_old - m_new)

        def weighted(hd, qs):
            ql = slice(qs * sub, (qs + 1) * sub)
            nrow = rows(qs)
            s_buf, st_buf = s_bufs[qs], st_bufs[qs]
            pm = jnp.exp2(s_buf[:nrow, :] - st_buf[0])
            acc_sc[hd, :, ql] = st_buf[1] * acc_sc[hd, :, ql] + _dot(vt_ref[0, hd, :, :nrow], pm.astype(BF16))

        scores(0, 0)

        for hd in range(MLA_HEADS):
            for qs in range(ns):
                if qs + 1 < ns:
                    scores(hd, qs + 1)
                elif hd + 1 < MLA_HEADS:
                    scores(hd + 1, 0)
                weighted(hd, qs)

    @pl.when(kind == 0)
    def _():
        run(None)

    for part in range(kblk // blk):
        @pl.when(kind == 1 + part)
        def _(part=part):
            run(part)
            o = acc_sc[:, :V_HEAD, :] / acc_sc[:, V_HEAD:V_HEAD + 1, :]
            o_ref[0] = o.reshape(MLA_WIDTH, blk).T.astype(BF16)


def _attn_prompt(qt, k, vt):
    bp, _, _, tp = qt.shape
    blk = min(ATT_BLOCK, tp)
    kblk = min(ATT_KV_BLOCK, tp)
    sub = min(ATT_SUB, blk)
    ns = blk // sub
    ratio = kblk // blk
    pairs = [(q, j) for q in range(tp // blk) for j in range(q // ratio + 1)]
    qi_of = np.array([q for q, _ in pairs], np.int32)
    kj_of = np.array([j for _, j in pairs], np.int32)
    kind_of = np.array([0 if j < q // ratio else 1 + q % ratio for q, j in pairs], np.int32)
    grid_spec = pltpu.PrefetchScalarGridSpec(
        num_scalar_prefetch=3,
        grid=(bp, len(pairs)),
        in_specs=[pl.BlockSpec((1, MLA_HEADS, HEAD_PAD, blk), lambda b, p, qi, kj, kd: (b, 0, 0, qi[p])),
                  pl.BlockSpec((1, MLA_HEADS, kblk, HEAD_PAD), lambda b, p, qi, kj, kd: (b, 0, kj[p], 0)),
                  pl.BlockSpec((1, MLA_HEADS, VT_ROWS, kblk), lambda b, p, qi, kj, kd: (b, 0, 0, kj[p]))],
        out_specs=pl.BlockSpec((1, blk, MLA_WIDTH), lambda b, p, qi, kj, kd: (b, qi[p], 0)),
        scratch_shapes=[pltpu.VMEM((MLA_HEADS, 1, blk), F32),
                        pltpu.VMEM((MLA_HEADS, VT_ROWS, blk), F32)]
                       + [pltpu.VMEM((kblk, sub), F32)] * ns
                       + [pltpu.VMEM((2, 1, sub), F32)] * ns,
    )
    return pl.pallas_call(
        functools.partial(_attn_kernel, blk=blk, kblk=kblk, sub=sub),
        grid_spec=grid_spec,
        out_shape=jax.ShapeDtypeStruct((bp, tp, MLA_WIDTH), BF16),
        compiler_params=pltpu.CompilerParams(dimension_semantics=("arbitrary", "arbitrary"),
                                             vmem_limit_bytes=VMEM_LIMIT),
        name="attn_prompt",
    )(jnp.asarray(qi_of), jnp.asarray(kj_of), jnp.asarray(kind_of), qt, k, vt)


def _attn_sample_kernel(q_ref, cc_ref, ckr_ref, nc_ref, nkr_ref, wuk_ref, wuvp_ref, place_ref, o_ref):
    t = q_ref.shape[1]
    q = q_ref[0]
    qh = [q[:, hd * HEAD_PAD:(hd + 1) * HEAD_PAD] for hd in range(MLA_HEADS)]
    q_stack = jnp.concatenate(qh, axis=0)
    q_abs = jnp.concatenate(
        [_nt(qh[hd][:, :QK_NOPE], wuk_ref[hd]) for hd in range(MLA_HEADS)], axis=0).astype(BF16)
    place = place_ref[...]
    cc = cc_ref[0, 0].astype(BF16)
    nc = nc_ref[0].astype(BF16)
    ckr = _dot(ckr_ref[0, 0].astype(BF16), place).astype(BF16)
    nkr = _dot(nkr_ref[0].astype(BF16), place).astype(BF16)
    s_c = _nt(q_abs, cc) + _nt(q_stack, ckr)
    s_n = _nt(q_abs, nc) + _nt(q_stack, nkr)
    m = jnp.maximum(jnp.max(s_c, axis=-1, keepdims=True), jnp.max(s_n, axis=-1, keepdims=True))
    p_c = jnp.exp(s_c - m)
    p_n = jnp.exp(s_n - m)
    l = jnp.sum(p_c, axis=-1, keepdims=True) + jnp.sum(p_n, axis=-1, keepdims=True)
    o_lat = (_dot(p_c.astype(BF16), cc) + _dot(p_n.astype(BF16), nc)) / l
    o_lat = o_lat.astype(BF16)
    out = jnp.zeros((t, MLA_WIDTH), F32)
    for hd in range(MLA_HEADS):
        out = out + _dot(o_lat[hd * t:(hd + 1) * t], wuvp_ref[hd])
    o_ref[0] = out.astype(BF16)


def _attn_sample(q, cache_ckv, cache_kr, new_ckv, new_kr, wts):
    nb, t, _ = q.shape
    past = cache_ckv.shape[2]
    place = np.zeros((QK_ROPE, HEAD_PAD), np.float32)
    place[np.arange(QK_ROPE), ROPE_LANE0 + np.arange(QK_ROPE)] = 1.0
    return pl.pallas_call(
        _attn_sample_kernel,
        grid=(nb,),
        in_specs=[pl.BlockSpec((1, t, MLA_HEADS * HEAD_PAD), lambda b: (b, 0, 0)),
                  pl.BlockSpec((1, 1, past, KV_LORA), lambda b: (0, b, 0, 0)),
                  pl.BlockSpec((1, 1, past, QK_ROPE), lambda b: (0, b, 0, 0)),
                  pl.BlockSpec((1, t, KV_LORA), lambda b: (b, 0, 0)),
                  pl.BlockSpec((1, t, QK_ROPE), lambda b: (b, 0, 0)),
                  _const_spec(wts["wuk3"].shape), _const_spec(wts["wuvp"].shape),
                  _const_spec((QK_ROPE, HEAD_PAD))],
        out_specs=pl.BlockSpec((1, t, MLA_WIDTH), lambda b: (b, 0, 0)),
        out_shape=jax.ShapeDtypeStruct((nb, t, MLA_WIDTH), BF16),
        compiler_params=pltpu.CompilerParams(dimension_semantics=("arbitrary",),
                                             vmem_limit_bytes=VMEM_LIMIT),
        name="attn_sample",
    )(q, cache_ckv, cache_kr, new_ckv, new_kr, wts["wuk3"], wts["wuvp"], jnp.asarray(place, BF16))


def _split3(x):
    hi = x.astype(BF16)
    r1 = x - hi.astype(F32)
    mid = r1.astype(BF16)
    lo = (r1 - mid.astype(F32)).astype(BF16)
    return hi, mid, lo


def _hgrn_blocks(load, store, lb_ref, gn_ref, st_sc, sh_sc, L, nb, fillers=()):
    fillers = list(fillers)

    def fill(count):
        for _ in range(min(count, len(fillers))):
            fillers.pop(0)()

    heads = range(HG_HEADS)
    rmod = lax.broadcasted_iota(jnp.int32, (L, 1), 0) % HG_SUB
    tril = (lax.broadcasted_iota(jnp.int32, (L, L), 0) >= lax.broadcasted_iota(jnp.int32, (L, L), 1)
            ).astype(BF16)
    nsub = L // HG_SUB
    hsl = lambda a, hd: a[:, hd * HG_EXPAND:(hd + 1) * HG_EXPAND]
    tile = lambda a, i: a[HG_SUB * i:HG_SUB * (i + 1)]
    zero_tile = jnp.zeros((HG_SUB, HG_EXPAND), F32)
    lb = lb_ref[...]
    gain = gn_ref[...]
    sh_sc[:, :, :, :HG_SUB, :] = jnp.zeros((nb, HG_HEADS, 3, HG_SUB, HG_EXPAND), F32)

    blocks = []
    for j in range(nb):
        hq, hf, v, hg = (load(part, j) for part in range(4))
        sg = _sigmoid(hf)
        g = jnp.log2(lb + (1.0 - lb) * sg)
        blocks.append(dict(v=v, hg=hg,
                           kk=(1.0 - lb) * (1.0 - sg),
                           qq=hq * _sigmoid(hq) * HG_SCALE,
                           g3=_split3(g)))
    for blk in blocks:
        g_hi, g_mid, g_lo = blk.pop("g3")
        blk["bc"] = _dot(tril, g_hi) + _dot(tril, g_mid) + _dot(tril, g_lo)
    fill(2)
    for blk in blocks:
        bc, qq, kk = blk["bc"], blk["qq"], blk["kk"]
        last = bc[L - 1:L, :]
        blk["qdec"] = (qq * jnp.exp2(bc)).astype(BF16)
        blk["kdec"] = (kk * jnp.exp2(last - bc)).astype(BF16)
        blk["dec"] = jnp.exp2(last)
        blk["vb"] = blk["v"].astype(BF16)
        if nsub > 1:
            blk["q2"], blk["k2"] = [], []
            for hd in heads:
                qh, kh, bh = hsl(qq, hd), hsl(kk, hd), hsl(bc, hd)
                q_rows = [jnp.concatenate([zero_tile] * (nsub - 1), axis=1)]
                k_cols = []
                for i in range(1, nsub):
                    b_i = bh[HG_SUB * i - 1:HG_SUB * i, :]
                    q_i = tile(qh, i) * jnp.exp2(tile(bh, i) - b_i)
                    q_rows.append(jnp.concatenate([zero_tile] * (i - 1) + [q_i] + [zero_tile] * (nsub - 1 - i),
                                                  axis=1))
                    k_i = kh[:HG_SUB * i] * jnp.exp2(b_i - bh[:HG_SUB * i])
                    k_cols.append(jnp.concatenate([k_i] + [zero_tile] * (nsub - i), axis=0))
                blk["q2"].append(jnp.concatenate(q_rows, axis=0).astype(BF16))
                blk["k2"].append(jnp.concatenate(k_cols, axis=1).astype(BF16))
    for blk in blocks:
        if nsub > 1:
            blk["att"] = [_nt(q2, k2).astype(BF16) for q2, k2 in zip(blk.pop("q2"), blk.pop("k2"))]
        blk["upd"] = [_tn(hsl(blk["vb"], hd), hsl(blk["kdec"], hd)) for hd in heads]
    fill(2)
    for j, blk in enumerate(blocks):
        blk["od"] = []
        for hd in heads:
            bc, qq, kk, v = (hsl(blk[name], hd) for name in ("bc", "qq", "kk", "v"))
            sh_sc[j, hd, 0, HG_SUB:, :] = kk
            sh_sc[j, hd, 1, HG_SUB:, :] = bc
            sh_sc[j, hd, 2, HG_SUB:, :] = v
            od = jnp.sum(qq * kk, axis=1, keepdims=True) * v
            for d in range(1, HG_SUB):
                shifted = pl.ds(HG_SUB - d, L)
                e = jnp.exp2(jnp.where(rmod >= d, bc - sh_sc[j, hd, 1, shifted, :], -jnp.inf))
                od = od + (jnp.sum(qq * sh_sc[j, hd, 0, shifted, :] * e, axis=1, keepdims=True)
                           * sh_sc[j, hd, 2, shifted, :])
            blk["od"].append(od)
    for blk in blocks:
        blk["st"] = []
    for hd in heads:
        st = st_sc[hd]
        for blk in blocks:
            blk["st"].append(st.astype(BF16))
            st = st * hsl(blk["dec"], hd) + blk["upd"][hd]
        st_sc[hd] = st
    for blk in blocks:
        o = [_nt(hsl(blk["qdec"], hd), blk["st"][hd]) for hd in heads]
        if nsub > 1:
            o = [o[hd] + _dot(blk["att"][hd], hsl(blk["vb"], hd)) for hd in heads]
        blk["o"] = o
    fill(len(fillers))
    for j, blk in enumerate(blocks):
        hg = blk["hg"]
        gate = hg * _sigmoid(hg)
        store(j, jnp.concatenate([_rms(blk["o"][hd] + blk["od"][hd], gain) * hsl(gate, hd) for hd in heads],
                                 axis=1))


def _hgrn_kernel(hh_ref, lb_ref, gn_ref, s0_ref, o_ref, sout_ref, st_sc, sh_sc, *, L, nb):
    tstep = pl.program_id(1)

    @pl.when(tstep == 0)
    def _():
        for hd in range(HG_HEADS):
            st_sc[hd] = s0_ref[0, hd].T

    def load(part, j):
        return hh_ref[0, j * L:(j + 1) * L, part * HG_WIDTH:(part + 1) * HG_WIDTH]

    def store(j, o):
        o_ref[0, j * L:(j + 1) * L, :] = o

    _hgrn_blocks(load, store, lb_ref, gn_ref, st_sc, sh_sc, L, nb)

    @pl.when(tstep == pl.num_programs(1) - 1)
    def _():
        for hd in range(HG_HEADS):
            sout_ref[0, hd] = st_sc[hd].T


def _hgrn(hh, lb, hg_norm, s0, L):
    bp, tp, _ = hh.shape
    tt = min(HG_TIME_TILE, tp)
    nb = tt // L
    return pl.pallas_call(
        functools.partial(_hgrn_kernel, L=L, nb=nb),
        grid=(bp, tp // tt),
        in_specs=[pl.BlockSpec((1, tt, 4 * HG_WIDTH), lambda b, t: (b, t, 0)),
                  _const_spec((1, HG_WIDTH)), _const_spec((1, HG_VDIM)),
                  pl.BlockSpec((1, HG_HEADS, HG_EXPAND, HG_VDIM), lambda b, t: (b, 0, 0, 0))],
        out_specs=[pl.BlockSpec((1, tt, HG_WIDTH), lambda b, t: (b, t, 0)),
                   pl.BlockSpec((1, HG_HEADS, HG_EXPAND, HG_VDIM), lambda b, t: (b, 0, 0, 0))],
        out_shape=[jax.ShapeDtypeStruct((bp, tp, HG_WIDTH), F32),
                   jax.ShapeDtypeStruct((bp, HG_HEADS, HG_EXPAND, HG_VDIM), F32)],
        scratch_shapes=[pltpu.VMEM((HG_HEADS, HG_VDIM, HG_EXPAND), F32),
                        pltpu.VMEM((nb, HG_HEADS, 3, HG_SUB + L, HG_EXPAND), F32)],
        compiler_params=pltpu.CompilerParams(dimension_semantics=("arbitrary", "arbitrary"),
                                             vmem_limit_bytes=VMEM_LIMIT),
        name="hgrn",
    )(hh, lb, hg_norm, s0)


def _post_kernel(x_ref, at_ref, ho_ref, gg_ref, g1_ref, sh2_ref, sc2_ref, g2_ref,
                 wpa_ref, wpb_ref, wout_ref, n2_ref, wgu_ref, wdown_ref, fn_ref, y_ref):
    x = x_ref[0]
    ya = _dot(at_ref[0], wpa_ref[...])
    yb = _dot(ho_ref[0].astype(BF16), wpb_ref[...])
    mixed = _sigmoid(gg_ref[0, :, :D_MODEL]) * ya + _sigmoid(gg_ref[0, :, D_MODEL:]) * yb
    x1 = x + g1_ref[0] * _dot(mixed.astype(BF16), wout_ref[...])
    hb = (_rms(x1, n2_ref[...]) * (1.0 + sc2_ref[0]) + sh2_ref[0]).astype(BF16)
    ff = jnp.zeros(x.shape, F32)
    for c in range(D_FF // FF_CHUNK):
        gate = _dot(hb, wgu_ref[:, c * FF_CHUNK:(c + 1) * FF_CHUNK])
        up = _dot(hb, wgu_ref[:, D_FF + c * FF_CHUNK:D_FF + (c + 1) * FF_CHUNK])
        act = (gate * _sigmoid(gate) * up).astype(BF16)
        ff = ff + _dot(act, wdown_ref[c * FF_CHUNK:(c + 1) * FF_CHUNK, :])
    x2 = x1 + g2_ref[0] * ff
    y_ref[0] = _rms(x2, fn_ref[...])


def _post(x, attn, ho, gg, g1, sh2, sc2, g2, wts):
    bp, tp, _ = x.shape
    r = min(ROW_TILE, tp)
    mod_rows = g1.shape[1]
    mod_blk = (1, 1, D_MODEL) if mod_rows == 1 else (1, r, D_MODEL)
    mod_map = (lambda b, t: (b, 0, 0)) if mod_rows == 1 else (lambda b, t: (b, t, 0))
    row3 = lambda w: pl.BlockSpec((1, r, w), lambda b, t: (b, t, 0))
    mod = pl.BlockSpec(mod_blk, mod_map)
    names = ["wpa", "wpb", "wout", "norm2", "wgu", "wdown", "final_norm"]
    return pl.pallas_call(
        _post_kernel,
        grid=(bp, tp // r),
        in_specs=[row3(D_MODEL), row3(MLA_WIDTH), row3(HG_WIDTH), row3(2 * D_MODEL), mod, mod, mod, mod]
                 + [_const_spec(wts[n].shape) for n in names],
        out_specs=row3(D_MODEL),
        out_shape=jax.ShapeDtypeStruct((bp, tp, D_MODEL), F32),
        compiler_params=pltpu.CompilerParams(dimension_semantics=("arbitrary", "arbitrary"),
                                             vmem_limit_bytes=VMEM_LIMIT),
        name="post",
    )(x, attn, ho, gg, g1, sh2, sc2, g2, *[wts[n] for n in names])


def _prep_weights(w_in, q_norm, w_uq, kv_norm, w_ukv, lb_param, hg_norm, w_pa, w_pb, w_out,
                  norm1, norm2, w_gu, w_down, final_norm):
    half = QK_ROPE // 2
    offs = np.cumsum([0, Q_LORA, KV_LORA, QK_ROPE, HG_WIDTH, HG_WIDTH, HG_WIDTH, HG_WIDTH, D_MODEL, D_MODEL])
    w = w_in[0]
    w_kr = w[:, offs[2]:offs[3]]
    zeros = lambda n: jnp.zeros((D_MODEL, n), w.dtype)
    kr_a = jnp.concatenate([zeros(ROPE_LANE0), w_kr, zeros(LANES - ROPE_LANE0 - QK_ROPE)], axis=1)
    kr_b = jnp.concatenate([zeros(ROPE_LANE0), w_kr[:, half:], w_kr[:, :half],
                            zeros(LANES - ROPE_LANE0 - QK_ROPE)], axis=1)
    w_in_p = jnp.concatenate([w[:, offs[0]:offs[2]], w[:, offs[3]:offs[9]], kr_a, kr_b], axis=1).astype(BF16)

    uq = w_uq[0].reshape(Q_LORA, MLA_HEADS, QK_NOPE + QK_ROPE)
    pad = jnp.zeros((Q_LORA, MLA_HEADS, HEAD_PAD - QK_NOPE - QK_ROPE), uq.dtype)
    wqa = jnp.concatenate([uq, pad], axis=2).reshape(Q_LORA, MLA_HEADS * HEAD_PAD).astype(BF16)
    wqb = jnp.concatenate([jnp.zeros((Q_LORA, MLA_HEADS, QK_NOPE), uq.dtype),
                           uq[:, :, QK_NOPE + half:], uq[:, :, QK_NOPE:QK_NOPE + half], pad],
                          axis=2).reshape(Q_LORA, MLA_HEADS * HEAD_PAD).astype(BF16)

    ukv = w_ukv[0].reshape(KV_LORA, MLA_HEADS, QK_NOPE + V_HEAD)
    uk = ukv[:, :, :QK_NOPE]
    uv = ukv[:, :, QK_NOPE:]
    wuk = jnp.concatenate([uk, jnp.zeros((KV_LORA, MLA_HEADS, HEAD_PAD - QK_NOPE), uk.dtype)],
                          axis=2).reshape(KV_LORA, MLA_HEADS * HEAD_PAD).astype(BF16)
    wuvt = uv.reshape(KV_LORA, MLA_WIDTH).T.astype(BF16)
    wuk3 = jnp.transpose(uk, (1, 0, 2)).astype(BF16)
    eye = jnp.eye(MLA_HEADS, dtype=uv.dtype)
    wuvp = (jnp.transpose(uv, (1, 0, 2))[:, :, None, :] * eye[:, None, :, None]
            ).reshape(MLA_HEADS, KV_LORA, MLA_WIDTH).astype(BF16)

    lb = jnp.cumsum(jax.nn.softmax(lb_param.astype(F32), axis=0), axis=0)[0].reshape(1, HG_WIDTH)
    return {
        "w_in": w_in_p, "wqa": wqa, "wqb": wqb, "wqa_t": wqa.T, "wqb_t": wqb.T,
        "wuk": wuk, "wuvt": wuvt, "wuk3": wuk3, "wuvp": wuvp, "lb": lb,
        "q_norm": q_norm[0].reshape(1, Q_LORA), "kv_norm": kv_norm[0].reshape(1, KV_LORA),
        "hg_norm": hg_norm[0].reshape(1, HG_VDIM),
        "norm1": norm1[0].reshape(1, D_MODEL), "norm2": norm2[0].reshape(1, D_MODEL),
        "final_norm": final_norm.reshape(1, D_MODEL),
        "wpa": w_pa[0].astype(BF16), "wpb": w_pb[0].astype(BF16), "wout": w_out[0].astype(BF16),
        "wgu": w_gu[0].astype(BF16), "wdown": w_down[0].astype(BF16),
    }


def _rope_tables(pos):
    half = QK_ROPE // 2
    inv = ROPE_THETA ** (-np.arange(half, dtype=np.float64) / half)
    ang = np.asarray(pos, np.float64)[:, None] * inv[None, :]
    cos, sin = np.cos(ang), np.sin(ang)
    t = len(pos)
    ck = np.zeros((t, HEAD_PAD))
    sk = np.zeros((t, HEAD_PAD))
    ck[:, ROPE_LANE0:ROPE_LANE0 + QK_ROPE] = np.concatenate([cos, cos], axis=1)
    sk[:, ROPE_LANE0:ROPE_LANE0 + QK_ROPE] = np.concatenate([-sin, sin], axis=1)
    cq = ck.copy()
    cq[:, :QK_NOPE] = 1.0
    f = lambda a: jnp.asarray(a.astype(np.float32))
    return {"ck": f(ck), "sk": f(sk), "cq": f(cq), "sq": f(sk)}


def _trunk(prompt, x, mod, wts, tabs, s0=None, cache=None):
    sh1, sc1, g1, sh2, sc2, g2 = mod
    if prompt:
        w = dict(wts, wqa=wts["wqa_t"], wqb=wts["wqb_t"])
        t = dict(tabs, cq=tabs["cq"].T, sq=tabs["sq"].T)
        qt, k, vt, ckv, kr, ho, s_new, gg = _pre(True, x, sh1, sc1, w, t)
        attn = _attn_prompt(qt, k, vt)
    else:
        q, ckv, kr, hh, gg = _pre(False, x, sh1, sc1, wts, tabs)
        nb, t_new = cache[0].shape[1], x.shape[1] // cache[0].shape[1]
        attn = _attn_sample(q.reshape(nb, t_new, -1), cache[0], cache[1],
                            ckv.reshape(nb, t_new, KV_LORA), kr.reshape(nb, t_new, QK_ROPE), wts)
        attn = attn.reshape(1, nb * t_new, MLA_WIDTH)
        ho, s_new = _hgrn(hh.reshape(nb, t_new, 4 * HG_WIDTH), wts["lb"], wts["hg_norm"], s0, t_new)
        ho = ho.reshape(x.shape[0], x.shape[1], HG_WIDTH)
    y = _post(x, attn, ho, gg, g1, sh2, sc2, g2, wts)
    return y, ckv, kr, s_new


def kernel(x_prompt, x_sample, cache_ckv, cache_krope, state_hgrn, c_prompt, c_sample, w_in, q_norm, w_uq,
           kv_norm, w_ukv, lb_param, hg_norm, w_pa, w_pb, w_out, norm1, norm2, w_ada, b_ada, w_gu, w_down,
           final_norm):
    bsz, t_p, _ = x_prompt.shape
    nb, t_s, _ = x_sample.shape
    past = cache_ckv.shape[2]
    assert past % CHUNK == 0 and t_s <= CHUNK
    wts = _prep_weights(w_in, q_norm, w_uq, kv_norm, w_ukv, lb_param, hg_norm, w_pa, w_pb, w_out,
                        norm1, norm2, w_gu, w_down, final_norm)

    n_c = bsz + nb
    n_pad = -n_c % 8
    c_all = jnp.concatenate([c_prompt, c_sample, jnp.zeros((n_pad, D_MODEL), F32)], axis=0)
    mod_all = _ada(c_all, w_ada[0], b_ada[0])
    mod_p = [m[:, None, :] for m in jnp.split(mod_all[:bsz], 6, axis=-1)]
    mod_s = [jnp.repeat(m, t_s, axis=0)[None] for m in jnp.split(mod_all[bsz:n_c], 6, axis=-1)]

    tabs_p = _rope_tables(np.arange(t_p))
    tabs_s = _rope_tables(np.tile(past + np.arange(t_s), nb))

    y_p, ckv_p, kr_p, s_p = _trunk(True, x_prompt, mod_p, wts, tabs_p)
    y_s, ckv_s, kr_s, s_s = _trunk(False, x_sample.reshape(1, nb * t_s, D_MODEL), mod_s, wts, tabs_s,
                                   s0=state_hgrn[0], cache=(cache_ckv, cache_krope))
    return (y_p, y_s.reshape(nb, t_s, D_MODEL), ckv_p[None], kr_p[None], s_p[None],
            ckv_s.reshape(nb, t_s, KV_LORA)[None], kr_s.reshape(nb, t_s, QK_ROPE)[None], s_s[None])
```

```python
import functools
import math

import numpy as np
import jax
import jax.numpy as jnp
from jax import lax
from jax.experimental import pallas as pl
from jax.experimental.pallas import tpu as pltpu

F32 = jnp.float32
BF16 = jnp.bfloat16

D_MODEL = 1024
CHUNK = 64
EPS = 1e-6
MLA_HEADS = 8
QK_NOPE = 64
QK_ROPE = 32
V_HEAD = 64
Q_LORA = 384
KV_LORA = 256
ROPE_THETA = 10000.0
MLA_WIDTH = MLA_HEADS * V_HEAD
MLA_SCALE = 1.0 / math.sqrt(QK_NOPE + QK_ROPE)
HG_HEADS = 4
HG_EXPAND = 128
HG_VDIM = 128
HG_WIDTH = HG_HEADS * HG_EXPAND
HG_SCALE = 1.0 / math.sqrt(HG_EXPAND)
D_FF = 2816
LOG2E = math.log2(math.e)

LANES = 128
HEAD_PAD = LANES
ROPE_LANE0 = QK_NOPE
VT_ROWS = V_HEAD + 16
VMEM_LIMIT = 56 * 1024 * 1024

ROW_TILE = 512
PRE_ROW_TILE = 256
ATT_BLOCK = 1024
ATT_KV_BLOCK = 2048
ATT_SUB = 256
ATT_AHEAD = 1
HG_SUB = 8
HG_TIME_TILE = 512
FF_CHUNK = 1408

_C_QLAT = 0
_C_KVLAT = _C_QLAT + Q_LORA
_C_HG = _C_KVLAT + KV_LORA
_C_GATE = _C_HG + 4 * HG_WIDTH
_C_KRA = _C_GATE + 2 * D_MODEL
_C_KRB = _C_KRA + LANES
_C_END = _C_KRB + LANES


def _const_spec(shape):
    nd = len(shape)
    return pl.BlockSpec(shape, lambda *_: (0,) * nd, pipeline_mode=pl.Buffered(1))


def _nt(a, b):
    return lax.dot_general(a, b, (((1,), (1,)), ((), ())), preferred_element_type=F32)


def _tn(a, b):
    return lax.dot_general(a, b, (((0,), (0,)), ((), ())), preferred_element_type=F32)


def _dot(a, b):
    return jnp.dot(a, b, preferred_element_type=F32)


def _rms(x, w):
    return x * lax.rsqrt(jnp.mean(x * x, axis=-1, keepdims=True) + EPS) * w


def _sigmoid(x):
    return 1.0 / (1.0 + jnp.exp(-x))


def _ada_kernel(c_ref, w_ref, b_ref, o_ref):
    c = c_ref[...]
    a = (c * _sigmoid(c)).astype(BF16)
    o_ref[...] = _dot(a, w_ref[...].astype(BF16)) + b_ref[...]


def _ada(c, w_ada, b_ada):
    n = c.shape[0]
    cols = w_ada.shape[1]
    blk = D_MODEL
    return pl.pallas_call(
        _ada_kernel,
        grid=(cols // blk,),
        in_specs=[pl.BlockSpec((n, D_MODEL), lambda j: (0, 0)),
                  pl.BlockSpec((D_MODEL, blk), lambda j: (0, j)),
                  pl.BlockSpec((1, blk), lambda j: (0, j))],
        out_specs=pl.BlockSpec((n, blk), lambda j: (0, j)),
        out_shape=jax.ShapeDtypeStruct((n, cols), F32),
        compiler_params=pltpu.CompilerParams(dimension_semantics=("arbitrary",)),
        name="ada",
    )(c, w_ada, b_ada.reshape(1, cols))


def _pre_kernel(prompt, x_ref, sh_ref, sc_ref, n1_ref, win_ref, qn_ref, wqa_ref, wqb_ref, kvn_ref,
                cq_ref, sq_ref, ck_ref, sk_ref, *rest):
    if prompt:
        (wuk_ref, wuvt_ref, lb_ref, gn_ref, qt_ref, k_ref, vt_ref, ckv_ref, kr_ref, ho_ref, sout_ref, gg_ref,
         hh_ref, st_sc, sh_sc) = rest
        tstep = pl.program_id(1)

        @pl.when(tstep == 0)
        def _():
            st_sc[...] = jnp.zeros(st_sc.shape, F32)
    else:
        q_ref, ckv_ref, kr_ref, hh_ref, gg_ref = rest
    x = x_ref[0]
    h = _rms(x, n1_ref[...]) * (1.0 + sc_ref[0]) + sh_ref[0]
    hb = h.astype(BF16)

    def proj(a, b):
        return _dot(hb, win_ref[:, a:b])

    def latents():
        kr_full = proj(_C_KRA, _C_KRB) * ck_ref[...] + proj(_C_KRB, _C_END) * sk_ref[...]
        if prompt:
            kr_ref[0] = kr_full.T[ROPE_LANE0:ROPE_LANE0 + QK_ROPE]
        else:
            kr_ref[0] = kr_full[:, ROPE_LANE0:ROPE_LANE0 + QK_ROPE]
        c_kv = _rms(proj(_C_KVLAT, _C_HG), kvn_ref[...])
        ckv_ref[0] = c_kv
        return kr_full, c_kv, _rms(proj(_C_QLAT, _C_KVLAT), qn_ref[...]).astype(BF16)

    if prompt:
        lat = {}

        def gate_cols(lo, hi):
            gg_ref[0, :, lo:hi] = proj(_C_GATE + lo, _C_GATE + hi)

        def do_latents():
            lat["kr"], c_kv, lat["cq"] = latents()
            lat["ckv"] = c_kv.astype(BF16)

        def do_q():
            qa = _nt(wqa_ref[...], lat["cq"])
            cq = cq_ref[...]
            sq = sq_ref[...]
            half = QK_ROPE // 2
            r0, r1, r2 = ROPE_LANE0, ROPE_LANE0 + half, ROPE_LANE0 + QK_ROPE
            for hd in range(MLA_HEADS):
                qh = qa[hd * HEAD_PAD:(hd + 1) * HEAD_PAD]
                qsw = jnp.concatenate([qh[:r0], qh[r1:r2], qh[r0:r1], qh[r2:]], axis=0)
                qt_ref[0, hd] = ((qh * cq + qsw * sq) * (MLA_SCALE * LOG2E)).astype(BF16)

        def do_kv():
            kall = _dot(lat["ckv"], wuk_ref[...])
            for hd in range(MLA_HEADS):
                sl = slice(hd * HEAD_PAD, (hd + 1) * HEAD_PAD)
                k_ref[0, hd] = (kall[:, sl] + lat["kr"]).astype(BF16)
            vt = _nt(wuvt_ref[...], lat["ckv"])
            ones_rows = (lax.broadcasted_iota(jnp.int32, (VT_ROWS - V_HEAD, vt.shape[1]), 0) == 0).astype(BF16)
            for hd in range(MLA_HEADS):
                vt_ref[0, hd, :V_HEAD] = vt[hd * V_HEAD:(hd + 1) * V_HEAD].astype(BF16)
                vt_ref[0, hd, V_HEAD:] = ones_rows

        gstep = D_MODEL // 2
        mla_work = [functools.partial(gate_cols, 0, gstep), functools.partial(gate_cols, gstep, 2 * gstep),
                    do_latents, functools.partial(gate_cols, 2 * gstep, 3 * gstep), do_q,
                    functools.partial(gate_cols, 3 * gstep, 4 * gstep), do_kv]
        def hg_proj(slot):
            hh_ref[slot] = proj(_C_HG, _C_GATE)

        def hgrn(slot, fillers):
            def load(part, j):
                return hh_ref[slot, j * CHUNK:(j + 1) * CHUNK, part * HG_WIDTH:(part + 1) * HG_WIDTH]

            def store(j, o):
                ho_ref[0, j * CHUNK:(j + 1) * CHUNK, :] = o

            _hgrn_blocks(load, store, lb_ref, gn_ref, st_sc, sh_sc, CHUNK, hh_ref.shape[1] // CHUNK, fillers)

        slot = tstep % 2
        drain = pl.num_programs(1) - 1

        @pl.when(tstep == 0)
        def _():
            for work in [functools.partial(hg_proj, slot)] + mla_work:
                work()

        @pl.when((tstep > 0) & (tstep < drain))
        def _():
            hgrn(1 - slot, [functools.partial(hg_proj, slot)] + mla_work)

        @pl.when(tstep == drain)
        def _():
            hgrn(1 - slot, [])
            for hd in range(HG_HEADS):
                sout_ref[0, hd] = st_sc[hd].T
    else:
        hh_ref[0] = proj(_C_HG, _C_GATE)
        gg_ref[0] = proj(_C_GATE, _C_KRA)
        _, _, cqb = latents()
        qa = _dot(cqb, wqa_ref[...])
        qb = _dot(cqb, wqb_ref[...])
        cq = cq_ref[...]
        sq = sq_ref[...]
        for hd in range(MLA_HEADS):
            sl = slice(hd * HEAD_PAD, (hd + 1) * HEAD_PAD)
            q_ref[0, :, sl] = ((qa[:, sl] * cq + qb[:, sl] * sq) * MLA_SCALE).astype(BF16)


def _pre(prompt, x, sh1, sc1, wts, tabs):
    bp, tp, _ = x.shape
    r = min(PRE_ROW_TILE if prompt else ROW_TILE, tp)
    nt = tp // r
    steps = nt + 1 if prompt else nt
    cur = (lambda t: jnp.minimum(t, nt - 1)) if prompt else (lambda t: t)
    mod_rows = sh1.shape[1]
    mod_blk = (1, 1, D_MODEL) if mod_rows == 1 else (1, r, D_MODEL)
    mod_map = (lambda b, t: (b, 0, 0)) if mod_rows == 1 else (lambda b, t: (b, cur(t), 0))
    row3 = lambda w: pl.BlockSpec((1, r, w), lambda b, t: (b, cur(t), 0))
    in_specs = [row3(D_MODEL), pl.BlockSpec(mod_blk, mod_map), pl.BlockSpec(mod_blk, mod_map),
                _const_spec((1, D_MODEL)), _const_spec(wts["w_in"].shape), _const_spec((1, Q_LORA)),
                _const_spec(wts["wqa"].shape), _const_spec(wts["wqb"].shape), _const_spec((1, KV_LORA))]
    if prompt:
        in_specs += [pl.BlockSpec((HEAD_PAD, r), lambda b, t: (0, cur(t)))] * 2
    else:
        in_specs += [pl.BlockSpec((r, HEAD_PAD), lambda b, t: (cur(t), 0))] * 2
    in_specs += [pl.BlockSpec((r, HEAD_PAD), lambda b, t: (cur(t), 0))] * 2
    args = [x, sh1, sc1, wts["norm1"], wts["w_in"], wts["q_norm"], wts["wqa"], wts["wqb"], wts["kv_norm"],
            tabs["cq"], tabs["sq"], tabs["ck"], tabs["sk"]]
    f32_rows = lambda w: jax.ShapeDtypeStruct((bp, tp, w), F32)
    state_shape = (bp, HG_HEADS, HG_EXPAND, HG_VDIM)
    scratch = []
    if prompt:
        in_specs += [_const_spec(wts["wuk"].shape), _const_spec(wts["wuvt"].shape),
                     _const_spec((1, HG_WIDTH)), _const_spec((1, HG_VDIM))]
        args += [wts["wuk"], wts["wuvt"], wts["lb"], wts["hg_norm"]]
        out_shape = [jax.ShapeDtypeStruct((bp, MLA_HEADS, HEAD_PAD, tp), BF16),
                     jax.ShapeDtypeStruct((bp, MLA_HEADS, tp, HEAD_PAD), BF16),
                     jax.ShapeDtypeStruct((bp, MLA_HEADS, VT_ROWS, tp), BF16),
                     f32_rows(KV_LORA), jax.ShapeDtypeStruct((bp, QK_ROPE, tp), F32), f32_rows(HG_WIDTH),
                     jax.ShapeDtypeStruct(state_shape, F32), f32_rows(2 * D_MODEL)]
        out_specs = [pl.BlockSpec((1, MLA_HEADS, HEAD_PAD, r), lambda b, t: (b, 0, 0, cur(t))),
                     pl.BlockSpec((1, MLA_HEADS, r, HEAD_PAD), lambda b, t: (b, 0, cur(t), 0)),
                     pl.BlockSpec((1, MLA_HEADS, VT_ROWS, r), lambda b, t: (b, 0, 0, cur(t))),
                     row3(KV_LORA), pl.BlockSpec((1, QK_ROPE, r), lambda b, t: (b, 0, cur(t))),
                     pl.BlockSpec((1, r, HG_WIDTH), lambda b, t: (b, jnp.maximum(t - 1, 0), 0)),
                     pl.BlockSpec((1,) + state_shape[1:], lambda b, t: (b, 0, 0, 0)), row3(2 * D_MODEL)]
        scratch = [pltpu.VMEM((2, r, 4 * HG_WIDTH), F32),
                   pltpu.VMEM((HG_HEADS, HG_VDIM, HG_EXPAND), F32),
                   pltpu.VMEM((r // CHUNK, HG_HEADS, 3, HG_SUB + CHUNK, HG_EXPAND), F32)]
    else:
        out_shape = [jax.ShapeDtypeStruct((bp, tp, MLA_HEADS * HEAD_PAD), BF16),
                     f32_rows(KV_LORA), f32_rows(QK_ROPE), f32_rows(4 * HG_WIDTH), f32_rows(2 * D_MODEL)]
        out_specs = [row3(MLA_HEADS * HEAD_PAD), row3(KV_LORA), row3(QK_ROPE), row3(4 * HG_WIDTH),
                     row3(2 * D_MODEL)]
    return pl.pallas_call(
        functools.partial(_pre_kernel, prompt),
        grid=(bp, steps),
        in_specs=in_specs,
        out_specs=out_specs,
        out_shape=out_shape,
        scratch_shapes=scratch,
        compiler_params=pltpu.CompilerParams(dimension_semantics=("arbitrary", "arbitrary"),
                                             vmem_limit_bytes=VMEM_LIMIT),
        name="pre_prompt" if prompt else "pre_sample",
    )(*args)


def _attn_kernel(qi_ref, kj_ref, kind_ref, qt_ref, k_ref, vt_ref, o_ref, m_sc, acc_sc, *slabs, blk, kblk, sub):
    p = pl.program_id(1)
    kj = kj_ref[p]
    kind = kind_ref[p]
    ns = blk // sub
    s_bufs, st_bufs = slabs[:ns], slabs[ns:]

    @pl.when(kj == 0)
    def _():
        m_sc[...] = jnp.full(m_sc.shape, -jnp.inf, F32)
        acc_sc[...] = jnp.zeros(acc_sc.shape, F32)

    def run(part):
        diag = part is not None
        if diag:
            kc = lax.broadcasted_iota(jnp.int32, (sub, sub), 0) // CHUNK
            qc = lax.broadcasted_iota(jnp.int32, (sub, sub), 1) // CHUNK
            visible = kc <= qc

        def rows(qs):
            return part * blk + (qs + 1) * sub if diag else kblk

        def scores(hd, qs):
            ql = slice(qs * sub, (qs + 1) * sub)
            nrow = rows(qs)
            s_buf, st_buf = s_bufs[qs], st_bufs[qs]
            q_t = qt_ref[0, hd, :, ql]
            half = max(sub, (nrow // 2) // sub * sub)
            mx = None
            for lo, hi in ((0, half), (half, nrow)):
                if lo == hi:
                    continue
                s = _dot(k_ref[0, hd, lo:hi, :], q_t)
                if diag and hi == nrow:
                    top = nrow - sub - lo
                    dg = jnp.where(visible, s[top:], -jnp.inf)
                    s_buf[nrow - sub:nrow, :] = dg
                    part = jnp.max(dg, axis=0, keepdims=True)
                    if top > 0:
                        s_buf[lo:nrow - sub, :] = s[:top]
                        part = jnp.maximum(part, jnp.max(s[:top], axis=0, keepdims=True))
                else:
                    s_buf[lo:hi, :] = s
                    part = jnp.max(s, axis=0, keepdims=True)
                mx = part if mx is None else jnp.maximum(mx, part)
            m_old = m_sc[hd, :, ql]
            m_new = jnp.maximum(m_old, mx)
            m_sc[hd, :, ql] = m_new
            st_buf[0] = m_new
            st_buf[1] = jnp.exp2(m_old - m_new)

        def weighted(hd, qs):
            ql = slice(qs * sub, (qs + 1) * sub)
            nrow = rows(qs)
            s_buf, st_buf = s_bufs[qs], st_bufs[qs]
            pm = jnp.exp2(s_buf[:nrow, :] - st_buf[0])
            acc_sc[hd, :, ql] = st_buf[1] * acc_sc[hd, :, ql] + _dot(vt_ref[0, hd, :, :nrow], pm.astype(BF16))

        units = [(hd, qs) for hd in range(MLA_HEADS) for qs in range(ns)]
        for u in units[:ATT_AHEAD]:
            scores(*u)
        for i, u in enumerate(units):
            if i + ATT_AHEAD < len(units):
                scores(*units[i + ATT_AHEAD])
            weighted(*u)

    @pl.when(kind == 0)
    def _():
        run(None)

    for part in range(kblk // blk):
        @pl.when(kind == 1 + part)
        def _(part=part):
            run(part)
            o = acc_sc[:, :V_HEAD, :] / acc_sc[:, V_HEAD:V_HEAD + 1, :]
            o_ref[0] = o.reshape(MLA_WIDTH, blk).T.astype(BF16)


def _attn_prompt(qt, k, vt):
    bp, _, _, tp = qt.shape
    blk = min(ATT_BLOCK, tp)
    kblk = min(ATT_KV_BLOCK, tp)
    sub = min(ATT_SUB, blk)
    ns = blk // sub
    ratio = kblk // blk
    pairs = [(q, j) for q in range(tp // blk) for j in range(q // ratio + 1)]
    qi_of = np.array([q for q, _ in pairs], np.int32)
    kj_of = np.array([j for _, j in pairs], np.int32)
    kind_of = np.array([0 if j < q // ratio else 1 + q % ratio for q, j in pairs], np.int32)
    grid_spec = pltpu.PrefetchScalarGridSpec(
        num_scalar_prefetch=3,
        grid=(bp, len(pairs)),
        in_specs=[pl.BlockSpec((1, MLA_HEADS, HEAD_PAD, blk), lambda b, p, qi, kj, kd: (b, 0, 0, qi[p])),
                  pl.BlockSpec((1, MLA_HEADS, kblk, HEAD_PAD), lambda b, p, qi, kj, kd: (b, 0, kj[p], 0)),
                  pl.BlockSpec((1, MLA_HEADS, VT_ROWS, kblk), lambda b, p, qi, kj, kd: (b, 0, 0, kj[p]))],
        out_specs=pl.BlockSpec((1, blk, MLA_WIDTH), lambda b, p, qi, kj, kd: (b, qi[p], 0)),
        scratch_shapes=[pltpu.VMEM((MLA_HEADS, 1, blk), F32),
                        pltpu.VMEM((MLA_HEADS, VT_ROWS, blk), F32)]
                       + [pltpu.VMEM((kblk, sub), F32)] * ns
                       + [pltpu.VMEM((2, 1, sub), F32)] * ns,
    )
    return pl.pallas_call(
        functools.partial(_attn_kernel, blk=blk, kblk=kblk, sub=sub),
        grid_spec=grid_spec,
        out_shape=jax.ShapeDtypeStruct((bp, tp, MLA_WIDTH), BF16),
        compiler_params=pltpu.CompilerParams(dimension_semantics=("arbitrary", "arbitrary"),
                                             vmem_limit_bytes=VMEM_LIMIT),
        name="attn_prompt",
    )(jnp.asarray(qi_of), jnp.asarray(kj_of), jnp.asarray(kind_of), qt, k, vt)


def _attn_sample_kernel(q_ref, cc_ref, ckrt_ref, nc_ref, nkr_ref, wuk_ref, wuvp_ref, o_ref):
    t = q_ref.shape[1]
    q = q_ref[0]
    qh = [q[:, hd * HEAD_PAD:(hd + 1) * HEAD_PAD] for hd in range(MLA_HEADS)]
    q_rope = jnp.concatenate([h[:, ROPE_LANE0:ROPE_LANE0 + QK_ROPE] for h in qh], axis=0)
    q_abs = jnp.concatenate(
        [_nt(qh[hd][:, :QK_NOPE], wuk_ref[hd]) for hd in range(MLA_HEADS)], axis=0).astype(BF16)
    cc = cc_ref[0, 0].astype(BF16)
    nc = nc_ref[0].astype(BF16)
    s_c = _nt(q_abs, cc) + _dot(q_rope, ckrt_ref[0].astype(BF16))
    s_n = _nt(q_abs, nc) + _nt(q_rope, nkr_ref[0].astype(BF16))
    m = jnp.maximum(jnp.max(s_c, axis=-1, keepdims=True), jnp.max(s_n, axis=-1, keepdims=True))
    p_c = jnp.exp(s_c - m)
    p_n = jnp.exp(s_n - m)
    l = jnp.sum(p_c, axis=-1, keepdims=True) + jnp.sum(p_n, axis=-1, keepdims=True)
    o_lat = (_dot(p_c.astype(BF16), cc) + _dot(p_n.astype(BF16), nc)) / l
    o_lat = o_lat.astype(BF16)
    out = jnp.zeros((t, MLA_WIDTH), F32)
    for hd in range(MLA_HEADS):
        out = out + _dot(o_lat[hd * t:(hd + 1) * t], wuvp_ref[hd])
    o_ref[0] = out.astype(BF16)


def _attn_sample(q, cache_ckv, cache_kr, new_ckv, new_kr, wts):
    nb, t, _ = q.shape
    past = cache_ckv.shape[2]
    cache_kr_t = jnp.swapaxes(cache_kr[0], 1, 2)
    return pl.pallas_call(
        _attn_sample_kernel,
        grid=(nb,),
        in_specs=[pl.BlockSpec((1, t, MLA_HEADS * HEAD_PAD), lambda b: (b, 0, 0)),
                  pl.BlockSpec((1, 1, past, KV_LORA), lambda b: (0, b, 0, 0)),
                  pl.BlockSpec((1, QK_ROPE, past), lambda b: (b, 0, 0)),
                  pl.BlockSpec((1, t, KV_LORA), lambda b: (b, 0, 0)),
                  pl.BlockSpec((1, t, QK_ROPE), lambda b: (b, 0, 0)),
                  _const_spec(wts["wuk3"].shape), _const_spec(wts["wuvp"].shape)],
        out_specs=pl.BlockSpec((1, t, MLA_WIDTH), lambda b: (b, 0, 0)),
        out_shape=jax.ShapeDtypeStruct((nb, t, MLA_WIDTH), BF16),
        compiler_params=pltpu.CompilerParams(dimension_semantics=("arbitrary",),
                                             vmem_limit_bytes=VMEM_LIMIT),
        name="attn_sample",
    )(q, cache_ckv, cache_kr_t, new_ckv, new_kr, wts["wuk3"], wts["wuvp"])


def _split3(x):
    hi = x.astype(BF16)
    r1 = x - hi.astype(F32)
    mid = r1.astype(BF16)
    lo = (r1 - mid.astype(F32)).astype(BF16)
    return hi, mid, lo


def _hgrn_blocks(load, store, lb_ref, gn_ref, st_sc, sh_sc, L, nb, fillers=()):
    fillers = list(fillers)

    def fill(count):
        for _ in range(min(count, len(fillers))):
            fillers.pop(0)()

    heads = range(HG_HEADS)
    rmod = lax.broadcasted_iota(jnp.int32, (L, 1), 0) % HG_SUB
    tril = (lax.broadcasted_iota(jnp.int32, (L, L), 0) >= lax.broadcasted_iota(jnp.int32, (L, L), 1)
            ).astype(BF16)
    nsub = L // HG_SUB
    hsl = lambda a, hd: a[:, hd * HG_EXPAND:(hd + 1) * HG_EXPAND]
    tile = lambda a, i: a[HG_SUB * i:HG_SUB * (i + 1)]
    zero_tile = jnp.zeros((HG_SUB, HG_EXPAND), F32)
    lb = lb_ref[...]
    gain = gn_ref[...]
    sh_sc[:, :, :, :HG_SUB, :] = jnp.zeros((nb, HG_HEADS, 3, HG_SUB, HG_EXPAND), F32)

    blocks = []
    for j in range(nb):
        hq, hf, v, hg = (load(part, j) for part in range(4))
        sg = _sigmoid(hf)
        g = jnp.log2(lb + (1.0 - lb) * sg)
        blocks.append(dict(v=v, hg=hg,
                           kk=(1.0 - lb) * (1.0 - sg),
                           qq=hq * _sigmoid(hq) * HG_SCALE,
                           g3=_split3(g)))
    for blk in blocks:
        g_hi, g_mid, g_lo = blk.pop("g3")
        blk["bc"] = _dot(tril, g_hi) + _dot(tril, g_mid) + _dot(tril, g_lo)
    fill(2)
    for blk in blocks:
        bc, qq, kk = blk["bc"], blk["qq"], blk["kk"]
        last = bc[L - 1:L, :]
        blk["qdec"] = (qq * jnp.exp2(bc)).astype(BF16)
        blk["kdec"] = (kk * jnp.exp2(last - bc)).astype(BF16)
        blk["dec"] = jnp.exp2(last)
        blk["vb"] = blk["v"].astype(BF16)
        if nsub > 1:
            blk["q2"], blk["k2"] = [], []
            for hd in heads:
                qh, kh, bh = hsl(qq, hd), hsl(kk, hd), hsl(bc, hd)
                q_rows = [jnp.concatenate([zero_tile] * (nsub - 1), axis=1)]
                k_cols = []
                for i in range(1, nsub):
                    b_i = bh[HG_SUB * i - 1:HG_SUB * i, :]
                    q_i = tile(qh, i) * jnp.exp2(tile(bh, i) - b_i)
                    q_rows.append(jnp.concatenate([zero_tile] * (i - 1) + [q_i] + [zero_tile] * (nsub - 1 - i),
                                                  axis=1))
                    k_i = kh[:HG_SUB * i] * jnp.exp2(b_i - bh[:HG_SUB * i])
                    k_cols.append(jnp.concatenate([k_i] + [zero_tile] * (nsub - i), axis=0))
                blk["q2"].append(jnp.concatenate(q_rows, axis=0).astype(BF16))
                blk["k2"].append(jnp.concatenate(k_cols, axis=1).astype(BF16))
    for blk in blocks:
        if nsub > 1:
            blk["att"] = [_nt(q2, k2).astype(BF16) for q2, k2 in zip(blk.pop("q2"), blk.pop("k2"))]
        blk["upd"] = [_tn(hsl(blk["vb"], hd), hsl(blk["kdec"], hd)) for hd in heads]
    fill(2)
    for j, blk in enumerate(blocks):
        blk["od"] = []
        for hd in heads:
            bc, qq, kk, v = (hsl(blk[name], hd) for name in ("bc", "qq", "kk", "v"))
            sh_sc[j, hd, 0, HG_SUB:, :] = kk
            sh_sc[j, hd, 1, HG_SUB:, :] = bc
            sh_sc[j, hd, 2, HG_SUB:, :] = v
            od = jnp.sum(qq * kk, axis=1, keepdims=True) * v
            for d in range(1, HG_SUB):
                shifted = pl.ds(HG_SUB - d, L)
                e = jnp.exp2(jnp.where(rmod >= d, bc - sh_sc[j, hd, 1, shifted, :], -jnp.inf))
                od = od + (jnp.sum(qq * sh_sc[j, hd, 0, shifted, :] * e, axis=1, keepdims=True)
                           * sh_sc[j, hd, 2, shifted, :])
            blk["od"].append(od)
    for blk in blocks:
        blk["st"] = []
    for hd in heads:
        st = st_sc[hd]
        for blk in blocks:
            blk["st"].append(st.astype(BF16))
            st = st * hsl(blk["dec"], hd) + blk["upd"][hd]
        st_sc[hd] = st
    for blk in blocks:
        o = [_nt(hsl(blk["qdec"], hd), blk["st"][hd]) for hd in heads]
        if nsub > 1:
            o = [o[hd] + _dot(blk["att"][hd], hsl(blk["vb"], hd)) for hd in heads]
        blk["o"] = o
    fill(len(fillers))
    for j, blk in enumerate(blocks):
        hg = blk["hg"]
        gate = hg * _sigmoid(hg)
        store(j, jnp.concatenate([_rms(blk["o"][hd] + blk["od"][hd], gain) * hsl(gate, hd) for hd in heads],
                                 axis=1))


def _hgrn_kernel(hh_ref, lb_ref, gn_ref, s0_ref, o_ref, sout_ref, st_sc, sh_sc, *, L, nb):
    tstep = pl.program_id(1)

    @pl.when(tstep == 0)
    def _():
        for hd in range(HG_HEADS):
            st_sc[hd] = s0_ref[0, hd].T

    def load(part, j):
        return hh_ref[0, j * L:(j + 1) * L, part * HG_WIDTH:(part + 1) * HG_WIDTH]

    def store(j, o):
        o_ref[0, j * L:(j + 1) * L, :] = o

    _hgrn_blocks(load, store, lb_ref, gn_ref, st_sc, sh_sc, L, nb)

    @pl.when(tstep == pl.num_programs(1) - 1)
    def _():
        for hd in range(HG_HEADS):
            sout_ref[0, hd] = st_sc[hd].T


def _hgrn(hh, lb, hg_norm, s0, L):
    bp, tp, _ = hh.shape
    tt = min(HG_TIME_TILE, tp)
    nb = tt // L
    return pl.pallas_call(
        functools.partial(_hgrn_kernel, L=L, nb=nb),
        grid=(bp, tp // tt),
        in_specs=[pl.BlockSpec((1, tt, 4 * HG_WIDTH), lambda b, t: (b, t, 0)),
                  _const_spec((1, HG_WIDTH)), _const_spec((1, HG_VDIM)),
                  pl.BlockSpec((1, HG_HEADS, HG_EXPAND, HG_VDIM), lambda b, t: (b, 0, 0, 0))],
        out_specs=[pl.BlockSpec((1, tt, HG_WIDTH), lambda b, t: (b, t, 0)),
                   pl.BlockSpec((1, HG_HEADS, HG_EXPAND, HG_VDIM), lambda b, t: (b, 0, 0, 0))],
        out_shape=[jax.ShapeDtypeStruct((bp, tp, HG_WIDTH), F32),
                   jax.ShapeDtypeStruct((bp, HG_HEADS, HG_EXPAND, HG_VDIM), F32)],
        scratch_shapes=[pltpu.VMEM((HG_HEADS, HG_VDIM, HG_EXPAND), F32),
                        pltpu.VMEM((nb, HG_HEADS, 3, HG_SUB + L, HG_EXPAND), F32)],
        compiler_params=pltpu.CompilerParams(dimension_semantics=("arbitrary", "arbitrary"),
                                             vmem_limit_bytes=VMEM_LIMIT),
        name="hgrn",
    )(hh, lb, hg_norm, s0)


def _post_kernel(x_ref, at_ref, ho_ref, gg_ref, g1_ref, sh2_ref, sc2_ref, g2_ref,
                 wpa_ref, wpb_ref, wout_ref, n2_ref, wgu_ref, wdown_ref, fn_ref, y_ref):
    x = x_ref[0]
    ya = _dot(at_ref[0], wpa_ref[...])
    yb = _dot(ho_ref[0].astype(BF16), wpb_ref[...])
    mixed = _sigmoid(gg_ref[0, :, :D_MODEL]) * ya + _sigmoid(gg_ref[0, :, D_MODEL:]) * yb
    x1 = x + g1_ref[0] * _dot(mixed.astype(BF16), wout_ref[...])
    hb = (_rms(x1, n2_ref[...]) * (1.0 + sc2_ref[0]) + sh2_ref[0]).astype(BF16)
    ff = jnp.zeros(x.shape, F32)
    for c in range(D_FF // FF_CHUNK):
        gate = _dot(hb, wgu_ref[:, c * FF_CHUNK:(c + 1) * FF_CHUNK])
        up = _dot(hb, wgu_ref[:, D_FF + c * FF_CHUNK:D_FF + (c + 1) * FF_CHUNK])
        act = (gate * _sigmoid(gate) * up).astype(BF16)
        ff = ff + _dot(act, wdown_ref[c * FF_CHUNK:(c + 1) * FF_CHUNK, :])
    x2 = x1 + g2_ref[0] * ff
    y_ref[0] = _rms(x2, fn_ref[...])


def _post(x, attn, ho, gg, g1, sh2, sc2, g2, wts):
    bp, tp, _ = x.shape
    r = min(ROW_TILE, tp)
    mod_rows = g1.shape[1]
    mod_blk = (1, 1, D_MODEL) if mod_rows == 1 else (1, r, D_MODEL)
    mod_map = (lambda b, t: (b, 0, 0)) if mod_rows == 1 else (lambda b, t: (b, t, 0))
    row3 = lambda w: pl.BlockSpec((1, r, w), lambda b, t: (b, t, 0))
    mod = pl.BlockSpec(mod_blk, mod_map)
    names = ["wpa", "wpb", "wout", "norm2", "wgu", "wdown", "final_norm"]
    return pl.pallas_call(
        _post_kernel,
        grid=(bp, tp // r),
        in_specs=[row3(D_MODEL), row3(MLA_WIDTH), row3(HG_WIDTH), row3(2 * D_MODEL), mod, mod, mod, mod]
                 + [_const_spec(wts[n].shape) for n in names],
        out_specs=row3(D_MODEL),
        out_shape=jax.ShapeDtypeStruct((bp, tp, D_MODEL), F32),
        compiler_params=pltpu.CompilerParams(dimension_semantics=("arbitrary", "arbitrary"),
                                             vmem_limit_bytes=VMEM_LIMIT),
        name="post",
    )(x, attn, ho, gg, g1, sh2, sc2, g2, *[wts[n] for n in names])


def _prep_weights(w_in, q_norm, w_uq, kv_norm, w_ukv, lb_param, hg_norm, w_pa, w_pb, w_out,
                  norm1, norm2, w_gu, w_down, final_norm):
    half = QK_ROPE // 2
    offs = np.cumsum([0, Q_LORA, KV_LORA, QK_ROPE, HG_WIDTH, HG_WIDTH, HG_WIDTH, HG_WIDTH, D_MODEL, D_MODEL])
    w = w_in[0]
    w_kr = w[:, offs[2]:offs[3]]
    zeros = lambda n: jnp.zeros((D_MODEL, n), w.dtype)
    kr_a = jnp.concatenate([zeros(ROPE_LANE0), w_kr, zeros(LANES - ROPE_LANE0 - QK_ROPE)], axis=1)
    kr_b = jnp.concatenate([zeros(ROPE_LANE0), w_kr[:, half:], w_kr[:, :half],
                            zeros(LANES - ROPE_LANE0 - QK_ROPE)], axis=1)
    w_in_p = jnp.concatenate([w[:, offs[0]:offs[2]], w[:, offs[3]:offs[9]], kr_a, kr_b], axis=1).astype(BF16)

    uq = w_uq[0].reshape(Q_LORA, MLA_HEADS, QK_NOPE + QK_ROPE)
    pad = jnp.zeros((Q_LORA, MLA_HEADS, HEAD_PAD - QK_NOPE - QK_ROPE), uq.dtype)
    wqa = jnp.concatenate([uq, pad], axis=2).reshape(Q_LORA, MLA_HEADS * HEAD_PAD).astype(BF16)
    wqb = jnp.concatenate([jnp.zeros((Q_LORA, MLA_HEADS, QK_NOPE), uq.dtype),
                           uq[:, :, QK_NOPE + half:], uq[:, :, QK_NOPE:QK_NOPE + half], pad],
                          axis=2).reshape(Q_LORA, MLA_HEADS * HEAD_PAD).astype(BF16)

    ukv = w_ukv[0].reshape(KV_LORA, MLA_HEADS, QK_NOPE + V_HEAD)
    uk = ukv[:, :, :QK_NOPE]
    uv = ukv[:, :, QK_NOPE:]
    wuk = jnp.concatenate([uk, jnp.zeros((KV_LORA, MLA_HEADS, HEAD_PAD - QK_NOPE), uk.dtype)],
                          axis=2).reshape(KV_LORA, MLA_HEADS * HEAD_PAD).astype(BF16)
    wuvt = uv.reshape(KV_LORA, MLA_WIDTH).T.astype(BF16)
    wuk3 = jnp.transpose(uk, (1, 0, 2)).astype(BF16)
    eye = jnp.eye(MLA_HEADS, dtype=uv.dtype)
    wuvp = (jnp.transpose(uv, (1, 0, 2))[:, :, None, :] * eye[:, None, :, None]
            ).reshape(MLA_HEADS, KV_LORA, MLA_WIDTH).astype(BF16)

    lb = jnp.cumsum(jax.nn.softmax(lb_param.astype(F32), axis=0), axis=0)[0].reshape(1, HG_WIDTH)
    return {
        "w_in": w_in_p, "wqa": wqa, "wqb": wqb, "wqa_t": wqa.T, "wqb_t": wqb.T,
        "wuk": wuk, "wuvt": wuvt, "wuk3": wuk3, "wuvp": wuvp, "lb": lb,
        "q_norm": q_norm[0].reshape(1, Q_LORA), "kv_norm": kv_norm[0].reshape(1, KV_LORA),
        "hg_norm": hg_norm[0].reshape(1, HG_VDIM),
        "norm1": norm1[0].reshape(1, D_MODEL), "norm2": norm2[0].reshape(1, D_MODEL),
        "final_norm": final_norm.reshape(1, D_MODEL),
        "wpa": w_pa[0].astype(BF16), "wpb": w_pb[0].astype(BF16), "wout": w_out[0].astype(BF16),
        "wgu": w_gu[0].astype(BF16), "wdown": w_down[0].astype(BF16),
    }


def _rope_tables(pos):
    half = QK_ROPE // 2
    inv = ROPE_THETA ** (-np.arange(half, dtype=np.float64) / half)
    ang = np.asarray(pos, np.float64)[:, None] * inv[None, :]
    cos, sin = np.cos(ang), np.sin(ang)
    t = len(pos)
    ck = np.zeros((t, HEAD_PAD))
    sk = np.zeros((t, HEAD_PAD))
    ck[:, ROPE_LANE0:ROPE_LANE0 + QK_ROPE] = np.concatenate([cos, cos], axis=1)
    sk[:, ROPE_LANE0:ROPE_LANE0 + QK_ROPE] = np.concatenate([-sin, sin], axis=1)
    cq = ck.copy()
    cq[:, :QK_NOPE] = 1.0
    f = lambda a: jnp.asarray(a.astype(np.float32))
    return {"ck": f(ck), "sk": f(sk), "cq": f(cq), "sq": f(sk)}


def _trunk(prompt, x, mod, wts, tabs, s0=None, cache=None):
    sh1, sc1, g1, sh2, sc2, g2 = mod
    if prompt:
        w = dict(wts, wqa=wts["wqa_t"], wqb=wts["wqb_t"])
        t = dict(tabs, cq=tabs["cq"].T, sq=tabs["sq"].T)
        qt, k, vt, ckv, kr_t, ho, s_new, gg = _pre(True, x, sh1, sc1, w, t)
        kr = jnp.swapaxes(kr_t, 1, 2)
        attn = _attn_prompt(qt, k, vt)
    else:
        q, ckv, kr, hh, gg = _pre(False, x, sh1, sc1, wts, tabs)
        nb, t_new = cache[0].shape[1], x.shape[1] // cache[0].shape[1]
        attn = _attn_sample(q.reshape(nb, t_new, -1), cache[0], cache[1],
                            ckv.reshape(nb, t_new, KV_LORA), kr.reshape(nb, t_new, QK_ROPE), wts)
        attn = attn.reshape(1, nb * t_new, MLA_WIDTH)
        ho, s_new = _hgrn(hh.reshape(nb, t_new, 4 * HG_WIDTH), wts["lb"], wts["hg_norm"], s0, t_new)
        ho = ho.reshape(x.shape[0], x.shape[1], HG_WIDTH)
    y = _post(x, attn, ho, gg, g1, sh2, sc2, g2, wts)
    return y, ckv, kr, s_new


def kernel(x_prompt, x_sample, cache_ckv, cache_krope, state_hgrn, c_prompt, c_sample, w_in, q_norm, w_uq,
           kv_norm, w_ukv, lb_param, hg_norm, w_pa, w_pb, w_out, norm1, norm2, w_ada, b_ada, w_gu, w_down,
           final_norm):
    bsz, t_p, _ = x_prompt.shape
    nb, t_s, _ = x_sample.shape
    past = cache_ckv.shape[2]
    assert past % CHUNK == 0 and t_s <= CHUNK
    wts = _prep_weights(w_in, q_norm, w_uq, kv_norm, w_ukv, lb_param, hg_norm, w_pa, w_pb, w_out,
                        norm1, norm2, w_gu, w_down, final_norm)

    n_c = bsz + nb
    n_pad = -n_c % 8
    c_all = jnp.concatenate([c_prompt, c_sample, jnp.zeros((n_pad, D_MODEL), F32)], axis=0)
    mod_all = _ada(c_all, w_ada[0], b_ada[0])
    mod_p = [m[:, None, :] for m in jnp.split(mod_all[:bsz], 6, axis=-1)]
    mod_s = [jnp.repeat(m, t_s, axis=0)[None] for m in jnp.split(mod_all[bsz:n_c], 6, axis=-1)]

    tabs_p = _rope_tables(np.arange(t_p))
    tabs_s = _rope_tables(np.tile(past + np.arange(t_s), nb))

    y_p, ckv_p, kr_p, s_p = _trunk(True, x_prompt, mod_p, wts, tabs_p)
    y_s, ckv_s, kr_s, s_s = _trunk(False, x_sample.reshape(1, nb * t_s, D_MODEL), mod_s, wts, tabs_s,
                                   s0=state_hgrn[0], cache=(cache_ckv, cache_krope))
    return (y_p, y_s.reshape(nb, t_s, D_MODEL), ckv_p[None], kr_p[None], s_p[None],
            ckv_s.reshape(nb, t_s, KV_LORA)[None], kr_s.reshape(nb, t_s, QK_ROPE)[None], s_s[None])
```

```python
import functools
import math

import numpy as np
import jax
import jax.numpy as jnp
from jax import lax
from jax.experimental import pallas as pl
from jax.experimental.pallas import tpu as pltpu

F32 = jnp.float32
BF16 = jnp.bfloat16

D_MODEL = 1024
CHUNK = 64
EPS = 1e-6
MLA_HEADS = 8
QK_NOPE = 64
QK_ROPE = 32
V_HEAD = 64
Q_LORA = 384
KV_LORA = 256
ROPE_THETA = 10000.0
MLA_WIDTH = MLA_HEADS * V_HEAD
MLA_SCALE = 1.0 / math.sqrt(QK_NOPE + QK_ROPE)
HG_HEADS = 4
HG_EXPAND = 128
HG_VDIM = 128
HG_WIDTH = HG_HEADS * HG_EXPAND
HG_SCALE = 1.0 / math.sqrt(HG_EXPAND)
D_FF = 2816
LOG2E = math.log2(math.e)

LANES = 128
HEAD_PAD = LANES
ROPE_LANE0 = QK_NOPE
VT_ROWS = V_HEAD + 16
VMEM_LIMIT = 56 * 1024 * 1024

ROW_TILE = 512
PRE_ROW_TILE = 256
ATT_BLOCK = 1024
ATT_KV_BLOCK = 2048
ATT_SUB = 256
ATT_AHEAD = 1
HG_SUB = 8
HG_TIME_TILE = 512
FF_CHUNK = 1408

_C_QLAT = 0
_C_KVLAT = _C_QLAT + Q_LORA
_C_HG = _C_KVLAT + KV_LORA
_C_GATE = _C_HG + 4 * HG_WIDTH
_C_KRA = _C_GATE + 2 * D_MODEL
_C_KRB = _C_KRA + LANES
_C_END = _C_KRB + LANES


def _const_spec(shape):
    nd = len(shape)
    return pl.BlockSpec(shape, lambda *_: (0,) * nd, pipeline_mode=pl.Buffered(1))


def _nt(a, b):
    return lax.dot_general(a, b, (((1,), (1,)), ((), ())), preferred_element_type=F32)


def _tn(a, b):
    return lax.dot_general(a, b, (((0,), (0,)), ((), ())), preferred_element_type=F32)


def _dot(a, b):
    return jnp.dot(a, b, preferred_element_type=F32)


def _rms(x, w):
    return x * lax.rsqrt(jnp.mean(x * x, axis=-1, keepdims=True) + EPS) * w


def _sigmoid(x):
    return 1.0 / (1.0 + jnp.exp(-x))


def _ada_kernel(c_ref, w_ref, b_ref, o_ref):
    c = c_ref[...]
    a = (c * _sigmoid(c)).astype(BF16)
    o_ref[...] = _dot(a, w_ref[...].astype(BF16)) + b_ref[...]


def _ada(c, w_ada, b_ada):
    n = c.shape[0]
    cols = w_ada.shape[1]
    blk = D_MODEL
    return pl.pallas_call(
        _ada_kernel,
        grid=(cols // blk,),
        in_specs=[pl.BlockSpec((n, D_MODEL), lambda j: (0, 0)),
                  pl.BlockSpec((D_MODEL, blk), lambda j: (0, j)),
                  pl.BlockSpec((1, blk), lambda j: (0, j))],
        out_specs=pl.BlockSpec((n, blk), lambda j: (0, j)),
        out_shape=jax.ShapeDtypeStruct((n, cols), F32),
        compiler_params=pltpu.CompilerParams(dimension_semantics=("arbitrary",)),
        name="ada",
    )(c, w_ada, b_ada.reshape(1, cols))


def _pre_kernel(prompt, x_ref, sh_ref, sc_ref, n1_ref, win_ref, qn_ref, wqa_ref, wqb_ref, kvn_ref,
                cq_ref, sq_ref, ck_ref, sk_ref, *rest):
    if prompt:
        (wuk_ref, wuvt_ref, lb_ref, gn_ref, qt_ref, k_ref, vt_ref, ckv_ref, kr_ref, ho_ref, sout_ref, gg_ref,
         hh_ref, st_sc, sh_sc) = rest
        tstep = pl.program_id(1)

        @pl.when(tstep == 0)
        def _():
            st_sc[...] = jnp.zeros(st_sc.shape, F32)
    else:
        q_ref, ckv_ref, kr_ref, hh_ref, gg_ref = rest
    x = x_ref[0]
    h = _rms(x, n1_ref[...]) * (1.0 + sc_ref[0]) + sh_ref[0]
    hb = h.astype(BF16)

    def proj(a, b):
        return _dot(hb, win_ref[:, a:b])

    def latents():
        kr_full = proj(_C_KRA, _C_KRB) * ck_ref[...] + proj(_C_KRB, _C_END) * sk_ref[...]
        if prompt:
            kr_ref[0] = kr_full.T[ROPE_LANE0:ROPE_LANE0 + QK_ROPE]
        else:
            kr_ref[0] = kr_full[:, ROPE_LANE0:ROPE_LANE0 + QK_ROPE]
        c_kv = _rms(proj(_C_KVLAT, _C_HG), kvn_ref[...])
        ckv_ref[0] = c_kv
        return kr_full, c_kv, _rms(proj(_C_QLAT, _C_KVLAT), qn_ref[...]).astype(BF16)

    if prompt:
        lat = {}

        def gate_cols(lo, hi):
            gg_ref[0, :, lo:hi] = proj(_C_GATE + lo, _C_GATE + hi)

        def do_latents():
            lat["kr"], c_kv, lat["cq"] = latents()
            lat["ckv"] = c_kv.astype(BF16)

        def do_q():
            qa = _nt(wqa_ref[...], lat["cq"])
            cq = cq_ref[...]
            sq = sq_ref[...]
            half = QK_ROPE // 2
            r0, r1, r2 = ROPE_LANE0, ROPE_LANE0 + half, ROPE_LANE0 + QK_ROPE
            for hd in range(MLA_HEADS):
                qh = qa[hd * HEAD_PAD:(hd + 1) * HEAD_PAD]
                qsw = jnp.concatenate([qh[:r0], qh[r1:r2], qh[r0:r1], qh[r2:]], axis=0)
                qt_ref[0, hd] = ((qh * cq + qsw * sq) * (MLA_SCALE * LOG2E)).astype(BF16)

        def do_kv():
            kall = _dot(lat["ckv"], wuk_ref[...])
            for hd in range(MLA_HEADS):
                sl = slice(hd * HEAD_PAD, (hd + 1) * HEAD_PAD)
                k_ref[0, hd] = (kall[:, sl] + lat["kr"]).astype(BF16)
            vt = _nt(wuvt_ref[...], lat["ckv"])
            ones_rows = (lax.broadcasted_iota(jnp.int32, (VT_ROWS - V_HEAD, vt.shape[1]), 0) == 0).astype(BF16)
            for hd in range(MLA_HEADS):
                vt_ref[0, hd, :V_HEAD] = vt[hd * V_HEAD:(hd + 1) * V_HEAD].astype(BF16)
                vt_ref[0, hd, V_HEAD:] = ones_rows

        gstep = D_MODEL // 2
        mla_work = [functools.partial(gate_cols, 0, gstep), functools.partial(gate_cols, gstep, 2 * gstep),
                    do_latents, functools.partial(gate_cols, 2 * gstep, 3 * gstep), do_q,
                    functools.partial(gate_cols, 3 * gstep, 4 * gstep), do_kv]
        hh_ref[...] = proj(_C_HG, _C_GATE)

        def load(part, j):
            return hh_ref[j * CHUNK:(j + 1) * CHUNK, part * HG_WIDTH:(part + 1) * HG_WIDTH]

        def store(j, o):
            ho_ref[0, j * CHUNK:(j + 1) * CHUNK, :] = o

        _hgrn_blocks(load, store, lb_ref, gn_ref, st_sc, sh_sc, CHUNK, hh_ref.shape[0] // CHUNK, mla_work)

        @pl.when(tstep == pl.num_programs(1) - 1)
        def _():
            for hd in range(HG_HEADS):
                sout_ref[0, hd] = st_sc[hd].T
    else:
        hh_ref[0] = proj(_C_HG, _C_GATE)
        gg_ref[0] = proj(_C_GATE, _C_KRA)
        _, _, cqb = latents()
        qa = _dot(cqb, wqa_ref[...])
        qb = _dot(cqb, wqb_ref[...])
        cq = cq_ref[...]
        sq = sq_ref[...]
        for hd in range(MLA_HEADS):
            sl = slice(hd * HEAD_PAD, (hd + 1) * HEAD_PAD)
            q_ref[0, :, sl] = ((qa[:, sl] * cq + qb[:, sl] * sq) * MLA_SCALE).astype(BF16)


def _pre(prompt, x, sh1, sc1, wts, tabs):
    bp, tp, _ = x.shape
    r = min(PRE_ROW_TILE if prompt else ROW_TILE, tp)
    nt = tp // r
    mod_rows = sh1.shape[1]
    mod_blk = (1, 1, D_MODEL) if mod_rows == 1 else (1, r, D_MODEL)
    mod_map = (lambda b, t: (b, 0, 0)) if mod_rows == 1 else (lambda b, t: (b, t, 0))
    row3 = lambda w: pl.BlockSpec((1, r, w), lambda b, t: (b, t, 0))
    in_specs = [row3(D_MODEL), pl.BlockSpec(mod_blk, mod_map), pl.BlockSpec(mod_blk, mod_map),
                _const_spec((1, D_MODEL)), _const_spec(wts["w_in"].shape), _const_spec((1, Q_LORA)),
                _const_spec(wts["wqa"].shape), _const_spec(wts["wqb"].shape), _const_spec((1, KV_LORA))]
    if prompt:
        in_specs += [pl.BlockSpec((HEAD_PAD, r), lambda b, t: (0, t))] * 2
    else:
        in_specs += [pl.BlockSpec((r, HEAD_PAD), lambda b, t: (t, 0))] * 2
    in_specs += [pl.BlockSpec((r, HEAD_PAD), lambda b, t: (t, 0))] * 2
    args = [x, sh1, sc1, wts["norm1"], wts["w_in"], wts["q_norm"], wts["wqa"], wts["wqb"], wts["kv_norm"],
            tabs["cq"], tabs["sq"], tabs["ck"], tabs["sk"]]
    f32_rows = lambda w: jax.ShapeDtypeStruct((bp, tp, w), F32)
    state_shape = (bp, HG_HEADS, HG_EXPAND, HG_VDIM)
    scratch = []
    if prompt:
        in_specs += [_const_spec(wts["wuk"].shape), _const_spec(wts["wuvt"].shape),
                     _const_spec((1, HG_WIDTH)), _const_spec((1, HG_VDIM))]
        args += [wts["wuk"], wts["wuvt"], wts["lb"], wts["hg_norm"]]
        out_shape = [jax.ShapeDtypeStruct((bp, MLA_HEADS, HEAD_PAD, tp), BF16),
                     jax.ShapeDtypeStruct((bp, MLA_HEADS, tp, HEAD_PAD), BF16),
                     jax.ShapeDtypeStruct((bp, MLA_HEADS, VT_ROWS, tp), BF16),
                     f32_rows(KV_LORA), jax.ShapeDtypeStruct((bp, QK_ROPE, tp), F32), f32_rows(HG_WIDTH),
                     jax.ShapeDtypeStruct(state_shape, F32), f32_rows(2 * D_MODEL)]
        out_specs = [pl.BlockSpec((1, MLA_HEADS, HEAD_PAD, r), lambda b, t: (b, 0, 0, t)),
                     pl.BlockSpec((1, MLA_HEADS, r, HEAD_PAD), lambda b, t: (b, 0, t, 0)),
                     pl.BlockSpec((1, MLA_HEADS, VT_ROWS, r), lambda b, t: (b, 0, 0, t)),
                     row3(KV_LORA), pl.BlockSpec((1, QK_ROPE, r), lambda b, t: (b, 0, t)), row3(HG_WIDTH),
                     pl.BlockSpec((1,) + state_shape[1:], lambda b, t: (b, 0, 0, 0)), row3(2 * D_MODEL)]
        scratch = [pltpu.VMEM((r, 4 * HG_WIDTH), F32),
                   pltpu.VMEM((HG_HEADS, HG_VDIM, HG_EXPAND), F32),
                   pltpu.VMEM((r // CHUNK, HG_HEADS, 3, HG_SUB + CHUNK, HG_EXPAND), F32)]
    else:
        out_shape = [jax.ShapeDtypeStruct((bp, tp, MLA_HEADS * HEAD_PAD), BF16),
                     f32_rows(KV_LORA), f32_rows(QK_ROPE), f32_rows(4 * HG_WIDTH), f32_rows(2 * D_MODEL)]
        out_specs = [row3(MLA_HEADS * HEAD_PAD), row3(KV_LORA), row3(QK_ROPE), row3(4 * HG_WIDTH),
                     row3(2 * D_MODEL)]
    return pl.pallas_call(
        functools.partial(_pre_kernel, prompt),
        grid=(bp, nt),
        in_specs=in_specs,
        out_specs=out_specs,
        out_shape=out_shape,
        scratch_shapes=scratch,
        compiler_params=pltpu.CompilerParams(dimension_semantics=("arbitrary", "arbitrary"),
                                             vmem_limit_bytes=VMEM_LIMIT),
        name="pre_prompt" if prompt else "pre_sample",
    )(*args)


def _attn_kernel(qi_ref, kj_ref, kind_ref, qt_ref, k_ref, vt_ref, o_ref, m_sc, acc_sc, *slabs, blk, kblk, sub):
    p = pl.program_id(1)
    kj = kj_ref[p]
    kind = kind_ref[p]
    ns = blk // sub
    s_bufs, st_bufs = slabs[:ns], slabs[ns:]

    @pl.when(kj == 0)
    def _():
        m_sc[...] = jnp.full(m_sc.shape, -jnp.inf, F32)
        acc_sc[...] = jnp.zeros(acc_sc.shape, F32)

    def run(part):
        diag = part is not None
        if diag:
            kc = lax.broadcasted_iota(jnp.int32, (sub, sub), 0) // CHUNK
            qc = lax.broadcasted_iota(jnp.int32, (sub, sub), 1) // CHUNK
            visible = kc <= qc

        def rows(qs):
            return part * blk + (qs + 1) * sub if diag else kblk

        def scores(hd, qs):
            ql = slice(qs * sub, (qs + 1) * sub)
            nrow = rows(qs)
            s_buf, st_buf = s_bufs[qs], st_bufs[qs]
            q_t = qt_ref[0, hd, :, ql]
            half = max(sub, (nrow // 2) // sub * sub)
            mx = None
            for lo, hi in ((0, half), (half, nrow)):
                if lo == hi:
                    continue
                s = _dot(k_ref[0, hd, lo:hi, :], q_t)
                if diag and hi == nrow:
                    top = nrow - sub - lo
                    dg = jnp.where(visible, s[top:], -jnp.inf)
                    s_buf[nrow - sub:nrow, :] = dg
                    part = jnp.max(dg, axis=0, keepdims=True)
                    if top > 0:
                        s_buf[lo:nrow - sub, :] = s[:top]
                        part = jnp.maximum(part, jnp.max(s[:top], axis=0, keepdims=True))
                else:
                    s_buf[lo:hi, :] = s
                    part = jnp.max(s, axis=0, keepdims=True)
                mx = part if mx is None else jnp.maximum(mx, part)
            m_old = m_sc[hd, :, ql]
            m_new = jnp.maximum(m_old, mx)
            m_sc[hd, :, ql] = m_new
            st_buf[0] = m_new
            st_buf[1] = jnp.exp2(m_old - m_new)

        def weighted(hd, qs):
            ql = slice(qs * sub, (qs + 1) * sub)
            nrow = rows(qs)
            s_buf, st_buf = s_bufs[qs], st_bufs[qs]
            pm = jnp.exp2(s_buf[:nrow, :] - st_buf[0])
            acc_sc[hd, :, ql] = st_buf[1] * acc_sc[hd, :, ql] + _dot(vt_ref[0, hd, :, :nrow], pm.astype(BF16))

        units = [(hd, qs) for hd in range(MLA_HEADS) for qs in range(ns)]
        for u in units[:ATT_AHEAD]:
            scores(*u)
        for i, u in enumerate(units):
            if i + ATT_AHEAD < len(units):
                scores(*units[i + ATT_AHEAD])
            weighted(*u)

    @pl.when(kind == 0)
    def _():
        run(None)

    for part in range(kblk // blk):
        @pl.when(kind == 1 + part)
        def _(part=part):
            run(part)
            o = acc_sc[:, :V_HEAD, :] / acc_sc[:, V_HEAD:V_HEAD + 1, :]
            o_ref[0] = o.reshape(MLA_WIDTH, blk).T.astype(BF16)


def _attn_prompt(qt, k, vt):
    bp, _, _, tp = qt.shape
    blk = min(ATT_BLOCK, tp)
    kblk = min(ATT_KV_BLOCK, tp)
    sub = min(ATT_SUB, blk)
    ns = blk // sub
    ratio = kblk // blk
    pairs = [(q, j) for q in range(tp // blk) for j in range(q // ratio + 1)]
    qi_of = np.array([q for q, _ in pairs], np.int32)
    kj_of = np.array([j for _, j in pairs], np.int32)
    kind_of = np.array([0 if j < q // ratio else 1 + q % ratio for q, j in pairs], np.int32)
    grid_spec = pltpu.PrefetchScalarGridSpec(
        num_scalar_prefetch=3,
        grid=(bp, len(pairs)),
        in_specs=[pl.BlockSpec((1, MLA_HEADS, HEAD_PAD, blk), lambda b, p, qi, kj, kd: (b, 0, 0, qi[p])),
                  pl.BlockSpec((1, MLA_HEADS, kblk, HEAD_PAD), lambda b, p, qi, kj, kd: (b, 0, kj[p], 0)),
                  pl.BlockSpec((1, MLA_HEADS, VT_ROWS, kblk), lambda b, p, qi, kj, kd: (b, 0, 0, kj[p]))],
        out_specs=pl.BlockSpec((1, blk, MLA_WIDTH), lambda b, p, qi, kj, kd: (b, qi[p], 0)),
        scratch_shapes=[pltpu.VMEM((MLA_HEADS, 1, blk), F32),
                        pltpu.VMEM((MLA_HEADS, VT_ROWS, blk), F32)]
                       + [pltpu.VMEM((kblk, sub), F32)] * ns
                       + [pltpu.VMEM((2, 1, sub), F32)] * ns,
    )
    return pl.pallas_call(
        functools.partial(_attn_kernel, blk=blk, kblk=kblk, sub=sub),
        grid_spec=grid_spec,
        out_shape=jax.ShapeDtypeStruct((bp, tp, MLA_WIDTH), BF16),
        compiler_params=pltpu.CompilerParams(dimension_semantics=("arbitrary", "arbitrary"),
                                             vmem_limit_bytes=VMEM_LIMIT),
        name="attn_prompt",
    )(jnp.asarray(qi_of), jnp.asarray(kj_of), jnp.asarray(kind_of), qt, k, vt)


def _attn_sample_kernel(q_ref, cc_ref, ckrt_ref, nc_ref, nkr_ref, wuk_ref, wuvp_ref, o_ref):
    t = q_ref.shape[1]
    q = q_ref[0]
    qh = [q[:, hd * HEAD_PAD:(hd + 1) * HEAD_PAD] for hd in range(MLA_HEADS)]
    q_rope = jnp.concatenate([h[:, ROPE_LANE0:ROPE_LANE0 + QK_ROPE] for h in qh], axis=0)
    q_abs = jnp.concatenate(
        [_nt(qh[hd][:, :QK_NOPE], wuk_ref[hd]) for hd in range(MLA_HEADS)], axis=0).astype(BF16)
    cc = cc_ref[0, 0].astype(BF16)
    nc = nc_ref[0].astype(BF16)
    s_c = _nt(q_abs, cc) + _dot(q_rope, ckrt_ref[0].astype(BF16))
    s_n = _nt(q_abs, nc) + _nt(q_rope, nkr_ref[0].astype(BF16))
    m = jnp.maximum(jnp.max(s_c, axis=-1, keepdims=True), jnp.max(s_n, axis=-1, keepdims=True))
    p_c = jnp.exp(s_c - m)
    p_n = jnp.exp(s_n - m)
    l = jnp.sum(p_c, axis=-1, keepdims=True) + jnp.sum(p_n, axis=-1, keepdims=True)
    o_lat = (_dot(p_c.astype(BF16), cc) + _dot(p_n.astype(BF16), nc)) / l
    o_lat = o_lat.astype(BF16)
    out = jnp.zeros((t, MLA_WIDTH), F32)
    for hd in range(MLA_HEADS):
        out = out + _dot(o_lat[hd * t:(hd + 1) * t], wuvp_ref[hd])
    o_ref[0] = out.astype(BF16)


def _attn_sample(q, cache_ckv, cache_kr, new_ckv, new_kr, wts):
    nb, t, _ = q.shape
    past = cache_ckv.shape[2]
    cache_kr_t = jnp.swapaxes(cache_kr[0], 1, 2)
    return pl.pallas_call(
        _attn_sample_kernel,
        grid=(nb,),
        in_specs=[pl.BlockSpec((1, t, MLA_HEADS * HEAD_PAD), lambda b: (b, 0, 0)),
                  pl.BlockSpec((1, 1, past, KV_LORA), lambda b: (0, b, 0, 0)),
                  pl.BlockSpec((1, QK_ROPE, past), lambda b: (b, 0, 0)),
                  pl.BlockSpec((1, t, KV_LORA), lambda b: (b, 0, 0)),
                  pl.BlockSpec((1, t, QK_ROPE), lambda b: (b, 0, 0)),
                  _const_spec(wts["wuk3"].shape), _const_spec(wts["wuvp"].shape)],
        out_specs=pl.BlockSpec((1, t, MLA_WIDTH), lambda b: (b, 0, 0)),
        out_shape=jax.ShapeDtypeStruct((nb, t, MLA_WIDTH), BF16),
        compiler_params=pltpu.CompilerParams(dimension_semantics=("arbitrary",),
                                             vmem_limit_bytes=VMEM_LIMIT),
        name="attn_sample",
    )(q, cache_ckv, cache_kr_t, new_ckv, new_kr, wts["wuk3"], wts["wuvp"])


def _split3(x):
    hi = x.astype(BF16)
    r1 = x - hi.astype(F32)
    mid = r1.astype(BF16)
    lo = (r1 - mid.astype(F32)).astype(BF16)
    return hi, mid, lo


def _hgrn_blocks(load, store, lb_ref, gn_ref, st_sc, sh_sc, L, nb, fillers=()):
    fillers = list(fillers)

    def fill(count):
        for _ in range(min(count, len(fillers))):
            fillers.pop(0)()

    heads = range(HG_HEADS)
    rmod = lax.broadcasted_iota(jnp.int32, (L, 1), 0) % HG_SUB
    tril = (lax.broadcasted_iota(jnp.int32, (L, L), 0) >= lax.broadcasted_iota(jnp.int32, (L, L), 1)
            ).astype(BF16)
    nsub = L // HG_SUB
    hsl = lambda a, hd: a[:, hd * HG_EXPAND:(hd + 1) * HG_EXPAND]
    tile = lambda a, i: a[HG_SUB * i:HG_SUB * (i + 1)]
    zero_tile = jnp.zeros((HG_SUB, HG_EXPAND), F32)
    lb = lb_ref[...]
    gain = gn_ref[...]
    sh_sc[:, :, :, :HG_SUB, :] = jnp.zeros((nb, HG_HEADS, 3, HG_SUB, HG_EXPAND), F32)

    blocks = []
    for j in range(nb):
        hq, hf, v, hg = (load(part, j) for part in range(4))
        sg = _sigmoid(hf)
        g = jnp.log2(lb + (1.0 - lb) * sg)
        blocks.append(dict(v=v, hg=hg,
                           kk=(1.0 - lb) * (1.0 - sg),
                           qq=hq * _sigmoid(hq) * HG_SCALE,
                           g3=_split3(g)))
    for blk in blocks:
        g_hi, g_mid, g_lo = blk.pop("g3")
        blk["bc"] = _dot(tril, g_hi) + _dot(tril, g_mid) + _dot(tril, g_lo)
    fill(2)
    for blk in blocks:
        bc, qq, kk = blk["bc"], blk["qq"], blk["kk"]
        last = bc[L - 1:L, :]
        blk["qdec"] = (qq * jnp.exp2(bc)).astype(BF16)
        blk["kdec"] = (kk * jnp.exp2(last - bc)).astype(BF16)
        blk["dec"] = jnp.exp2(last)
        blk["vb"] = blk["v"].astype(BF16)
        if nsub > 1:
            blk["q2"], blk["k2"] = [], []
            for hd in heads:
                qh, kh, bh = hsl(qq, hd), hsl(kk, hd), hsl(bc, hd)
                q_rows = [jnp.concatenate([zero_tile] * (nsub - 1), axis=1)]
                k_cols = []
                for i in range(1, nsub):
                    b_i = bh[HG_SUB * i - 1:HG_SUB * i, :]
                    q_i = tile(qh, i) * jnp.exp2(tile(bh, i) - b_i)
                    q_rows.append(jnp.concatenate([zero_tile] * (i - 1) + [q_i] + [zero_tile] * (nsub - 1 - i),
                                                  axis=1))
                    k_i = kh[:HG_SUB * i] * jnp.exp2(b_i - bh[:HG_SUB * i])
                    k_cols.append(jnp.concatenate([k_i] + [zero_tile] * (nsub - i), axis=0))
                blk["q2"].append(jnp.concatenate(q_rows, axis=0).astype(BF16))
                blk["k2"].append(jnp.concatenate(k_cols, axis=1).astype(BF16))
    for blk in blocks:
        if nsub > 1:
            blk["att"] = [_nt(q2, k2).astype(BF16) for q2, k2 in zip(blk.pop("q2"), blk.pop("k2"))]
        blk["upd"] = [_tn(hsl(blk["vb"], hd), hsl(blk["kdec"], hd)) for hd in heads]
    fill(2)
    for j, blk in enumerate(blocks):
        blk["od"] = []
        for hd in heads:
            bc, qq, kk, v = (hsl(blk[name], hd) for name in ("bc", "qq", "kk", "v"))
            sh_sc[j, hd, 0, HG_SUB:, :] = kk
            sh_sc[j, hd, 1, HG_SUB:, :] = bc
            sh_sc[j, hd, 2, HG_SUB:, :] = v
            od = jnp.sum(qq * kk, axis=1, keepdims=True) * v
            for d in range(1, HG_SUB):
                shifted = pl.ds(HG_SUB - d, L)
                e = jnp.exp2(jnp.where(rmod >= d, bc - sh_sc[j, hd, 1, shifted, :], -jnp.inf))
                od = od + (jnp.sum(qq * sh_sc[j, hd, 0, shifted, :] * e, axis=1, keepdims=True)
                           * sh_sc[j, hd, 2, shifted, :])
            blk["od"].append(od)
    for blk in blocks:
        blk["st"] = []
    for hd in heads:
        st = st_sc[hd]
        for blk in blocks:
            blk["st"].append(st.astype(BF16))
            st = st * hsl(blk["dec"], hd) + blk["upd"][hd]
        st_sc[hd] = st
    for blk in blocks:
        o = [_nt(hsl(blk["qdec"], hd), blk["st"][hd]) for hd in heads]
        if nsub > 1:
            o = [o[hd] + _dot(blk["att"][hd], hsl(blk["vb"], hd)) for hd in heads]
        blk["o"] = o
    fill(len(fillers))
    for j, blk in enumerate(blocks):
        hg = blk["hg"]
        gate = hg * _sigmoid(hg)
        store(j, jnp.concatenate([_rms(blk["o"][hd] + blk["od"][hd], gain) * hsl(gate, hd) for hd in heads],
                                 axis=1))


def _hgrn_kernel(hh_ref, lb_ref, gn_ref, s0_ref, o_ref, sout_ref, st_sc, sh_sc, *, L, nb):
    tstep = pl.program_id(1)

    @pl.when(tstep == 0)
    def _():
        for hd in range(HG_HEADS):
            st_sc[hd] = s0_ref[0, hd].T

    def load(part, j):
        return hh_ref[0, j * L:(j + 1) * L, part * HG_WIDTH:(part + 1) * HG_WIDTH]

    def store(j, o):
        o_ref[0, j * L:(j + 1) * L, :] = o

    _hgrn_blocks(load, store, lb_ref, gn_ref, st_sc, sh_sc, L, nb)

    @pl.when(tstep == pl.num_programs(1) - 1)
    def _():
        for hd in range(HG_HEADS):
            sout_ref[0, hd] = st_sc[hd].T


def _hgrn(hh, lb, hg_norm, s0, L):
    bp, tp, _ = hh.shape
    tt = min(HG_TIME_TILE, tp)
    nb = tt // L
    return pl.pallas_call(
        functools.partial(_hgrn_kernel, L=L, nb=nb),
        grid=(bp, tp // tt),
        in_specs=[pl.BlockSpec((1, tt, 4 * HG_WIDTH), lambda b, t: (b, t, 0)),
                  _const_spec((1, HG_WIDTH)), _const_spec((1, HG_VDIM)),
                  pl.BlockSpec((1, HG_HEADS, HG_EXPAND, HG_VDIM), lambda b, t: (b, 0, 0, 0))],
        out_specs=[pl.BlockSpec((1, tt, HG_WIDTH), lambda b, t: (b, t, 0)),
                   pl.BlockSpec((1, HG_HEADS, HG_EXPAND, HG_VDIM), lambda b, t: (b, 0, 0, 0))],
        out_shape=[jax.ShapeDtypeStruct((bp, tp, HG_WIDTH), F32),
                   jax.ShapeDtypeStruct((bp, HG_HEADS, HG_EXPAND, HG_VDIM), F32)],
        scratch_shapes=[pltpu.VMEM((HG_HEADS, HG_VDIM, HG_EXPAND), F32),
                        pltpu.VMEM((nb, HG_HEADS, 3, HG_SUB + L, HG_EXPAND), F32)],
        compiler_params=pltpu.CompilerParams(dimension_semantics=("arbitrary", "arbitrary"),
                                             vmem_limit_bytes=VMEM_LIMIT),
        name="hgrn",
    )(hh, lb, hg_norm, s0)


def _post_kernel(x_ref, at_ref, ho_ref, gg_ref, g1_ref, sh2_ref, sc2_ref, g2_ref,
                 wpa_ref, wpb_ref, wout_ref, n2_ref, wgu_ref, wdown_ref, fn_ref, y_ref):
    x = x_ref[0]
    ya = _dot(at_ref[0], wpa_ref[...])
    yb = _dot(ho_ref[0].astype(BF16), wpb_ref[...])
    mixed = _sigmoid(gg_ref[0, :, :D_MODEL]) * ya + _sigmoid(gg_ref[0, :, D_MODEL:]) * yb
    x1 = x + g1_ref[0] * _dot(mixed.astype(BF16), wout_ref[...])
    hb = (_rms(x1, n2_ref[...]) * (1.0 + sc2_ref[0]) + sh2_ref[0]).astype(BF16)
    ff = jnp.zeros(x.shape, F32)
    for c in range(D_FF // FF_CHUNK):
        gate = _dot(hb, wgu_ref[:, c * FF_CHUNK:(c + 1) * FF_CHUNK])
        up = _dot(hb, wgu_ref[:, D_FF + c * FF_CHUNK:D_FF + (c + 1) * FF_CHUNK])
        act = (gate * _sigmoid(gate) * up).astype(BF16)
        ff = ff + _dot(act, wdown_ref[c * FF_CHUNK:(c + 1) * FF_CHUNK, :])
    x2 = x1 + g2_ref[0] * ff
    y_ref[0] = _rms(x2, fn_ref[...])


def _post(x, attn, ho, gg, g1, sh2, sc2, g2, wts):
    bp, tp, _ = x.shape
    r = min(ROW_TILE, tp)
    mod_rows = g1.shape[1]
    mod_blk = (1, 1, D_MODEL) if mod_rows == 1 else (1, r, D_MODEL)
    mod_map = (lambda b, t: (b, 0, 0)) if mod_rows == 1 else (lambda b, t: (b, t, 0))
    row3 = lambda w: pl.BlockSpec((1, r, w), lambda b, t: (b, t, 0))
    mod = pl.BlockSpec(mod_blk, mod_map)
    names = ["wpa", "wpb", "wout", "norm2", "wgu", "wdown", "final_norm"]
    return pl.pallas_call(
        _post_kernel,
        grid=(bp, tp // r),
        in_specs=[row3(D_MODEL), row3(MLA_WIDTH), row3(HG_WIDTH), row3(2 * D_MODEL), mod, mod, mod, mod]
                 + [_const_spec(wts[n].shape) for n in names],
        out_specs=row3(D_MODEL),
        out_shape=jax.ShapeDtypeStruct((bp, tp, D_MODEL), F32),
        compiler_params=pltpu.CompilerParams(dimension_semantics=("arbitrary", "arbitrary"),
                                             vmem_limit_bytes=VMEM_LIMIT),
        name="post",
    )(x, attn, ho, gg, g1, sh2, sc2, g2, *[wts[n] for n in names])


def _prep_weights(w_in, q_norm, w_uq, kv_norm, w_ukv, lb_param, hg_norm, w_pa, w_pb, w_out,
                  norm1, norm2, w_gu, w_down, final_norm):
    half = QK_ROPE // 2
    offs = np.cumsum([0, Q_LORA, KV_LORA, QK_ROPE, HG_WIDTH, HG_WIDTH, HG_WIDTH, HG_WIDTH, D_MODEL, D_MODEL])
    w = w_in[0]
    w_kr = w[:, offs[2]:offs[3]]
    zeros = lambda n: jnp.zeros((D_MODEL, n), w.dtype)
    kr_a = jnp.concatenate([zeros(ROPE_LANE0), w_kr, zeros(LANES - ROPE_LANE0 - QK_ROPE)], axis=1)
    kr_b = jnp.concatenate([zeros(ROPE_LANE0), w_kr[:, half:], w_kr[:, :half],
                            zeros(LANES - ROPE_LANE0 - QK_ROPE)], axis=1)
    w_in_p = jnp.concatenate([w[:, offs[0]:offs[2]], w[:, offs[3]:offs[9]], kr_a, kr_b], axis=1).astype(BF16)

    uq = w_uq[0].reshape(Q_LORA, MLA_HEADS, QK_NOPE + QK_ROPE)
    pad = jnp.zeros((Q_LORA, MLA_HEADS, HEAD_PAD - QK_NOPE - QK_ROPE), uq.dtype)
    wqa = jnp.concatenate([uq, pad], axis=2).reshape(Q_LORA, MLA_HEADS * HEAD_PAD).astype(BF16)
    wqb = jnp.concatenate([jnp.zeros((Q_LORA, MLA_HEADS, QK_NOPE), uq.dtype),
                           uq[:, :, QK_NOPE + half:], uq[:, :, QK_NOPE:QK_NOPE + half], pad],
                          axis=2).reshape(Q_LORA, MLA_HEADS * HEAD_PAD).astype(BF16)

    ukv = w_ukv[0].reshape(KV_LORA, MLA_HEADS, QK_NOPE + V_HEAD)
    uk = ukv[:, :, :QK_NOPE]
    uv = ukv[:, :, QK_NOPE:]
    wuk = jnp.concatenate([uk, jnp.zeros((KV_LORA, MLA_HEADS, HEAD_PAD - QK_NOPE), uk.dtype)],
                          axis=2).reshape(KV_LORA, MLA_HEADS * HEAD_PAD).astype(BF16)
    wuvt = uv.reshape(KV_LORA, MLA_WIDTH).T.astype(BF16)
    wuk3 = jnp.transpose(uk, (1, 0, 2)).astype(BF16)
    eye = jnp.eye(MLA_HEADS, dtype=uv.dtype)
    wuvp = (jnp.transpose(uv, (1, 0, 2))[:, :, None, :] * eye[:, None, :, None]
            ).reshape(MLA_HEADS, KV_LORA, MLA_WIDTH).astype(BF16)

    lb = jnp.cumsum(jax.nn.softmax(lb_param.astype(F32), axis=0), axis=0)[0].reshape(1, HG_WIDTH)
    return {
        "w_in": w_in_p, "wqa": wqa, "wqb": wqb, "wqa_t": wqa.T, "wqb_t": wqb.T,
        "wuk": wuk, "wuvt": wuvt, "wuk3": wuk3, "wuvp": wuvp, "lb": lb,
        "q_norm": q_norm[0].reshape(1, Q_LORA), "kv_norm": kv_norm[0].reshape(1, KV_LORA),
        "hg_norm": hg_norm[0].reshape(1, HG_VDIM),
        "norm1": norm1[0].reshape(1, D_MODEL), "norm2": norm2[0].reshape(1, D_MODEL),
        "final_norm": final_norm.reshape(1, D_MODEL),
        "wpa": w_pa[0].astype(BF16), "wpb": w_pb[0].astype(BF16), "wout": w_out[0].astype(BF16),
        "wgu": w_gu[0].astype(BF16), "wdown": w_down[0].astype(BF16),
    }


def _rope_tables(pos):
    half = QK_ROPE // 2
    inv = ROPE_THETA ** (-np.arange(half, dtype=np.float64) / half)
    ang = np.asarray(pos, np.float64)[:, None] * inv[None, :]
    cos, sin = np.cos(ang), np.sin(ang)
    t = len(pos)
    ck = np.zeros((t, HEAD_PAD))
    sk = np.zeros((t, HEAD_PAD))
    ck[:, ROPE_LANE0:ROPE_LANE0 + QK_ROPE] = np.concatenate([cos, cos], axis=1)
    sk[:, ROPE_LANE0:ROPE_LANE0 + QK_ROPE] = np.concatenate([-sin, sin], axis=1)
    cq = ck.copy()
    cq[:, :QK_NOPE] = 1.0
    f = lambda a: jnp.asarray(a.astype(np.float32))
    return {"ck": f(ck), "sk": f(sk), "cq": f(cq), "sq": f(sk)}


def _trunk(prompt, x, mod, wts, tabs, s0=None, cache=None):
    sh1, sc1, g1, sh2, sc2, g2 = mod
    if prompt:
        w = dict(wts, wqa=wts["wqa_t"], wqb=wts["wqb_t"])
        t = dict(tabs, cq=tabs["cq"].T, sq=tabs["sq"].T)
        qt, k, vt, ckv, kr_t, ho, s_new, gg = _pre(True, x, sh1, sc1, w, t)
        kr = jnp.swapaxes(kr_t, 1, 2)
        attn = _attn_prompt(qt, k, vt)
    else:
        q, ckv, kr, hh, gg = _pre(False, x, sh1, sc1, wts, tabs)
        nb, t_new = cache[0].shape[1], x.shape[1] // cache[0].shape[1]
        attn = _attn_sample(q.reshape(nb, t_new, -1), cache[0], cache[1],
                            ckv.reshape(nb, t_new, KV_LORA), kr.reshape(nb, t_new, QK_ROPE), wts)
        attn = attn.reshape(1, nb * t_new, MLA_WIDTH)
        ho, s_new = _hgrn(hh.reshape(nb, t_new, 4 * HG_WIDTH), wts["lb"], wts["hg_norm"], s0, t_new)
        ho = ho.reshape(x.shape[0], x.shape[1], HG_WIDTH)
    y = _post(x, attn, ho, gg, g1, sh2, sc2, g2, wts)
    return y, ckv, kr, s_new


def kernel(x_prompt, x_sample, cache_ckv, cache_krope, state_hgrn, c_prompt, c_sample, w_in, q_norm, w_uq,
           kv_norm, w_ukv, lb_param, hg_norm, w_pa, w_pb, w_out, norm1, norm2, w_ada, b_ada, w_gu, w_down,
           final_norm):
    bsz, t_p, _ = x_prompt.shape
    nb, t_s, _ = x_sample.shape
    past = cache_ckv.shape[2]
    assert past % CHUNK == 0 and t_s <= CHUNK
    wts = _prep_weights(w_in, q_norm, w_uq, kv_norm, w_ukv, lb_param, hg_norm, w_pa, w_pb, w_out,
                        norm1, norm2, w_gu, w_down, final_norm)

    n_c = bsz + nb
    n_pad = -n_c % 8
    c_all = jnp.concatenate([c_prompt, c_sample, jnp.zeros((n_pad, D_MODEL), F32)], axis=0)
    mod_all = _ada(c_all, w_ada[0], b_ada[0])
    mod_p = [m[:, None, :] for m in jnp.split(mod_all[:bsz], 6, axis=-1)]
    mod_s = [jnp.repeat(m, t_s, axis=0)[None] for m in jnp.split(mod_all[bsz:n_c], 6, axis=-1)]

    tabs_p = _rope_tables(np.arange(t_p))
    tabs_s = _rope_tables(np.tile(past + np.arange(t_s), nb))

    y_p, ckv_p, kr_p, s_p = _trunk(True, x_prompt, mod_p, wts, tabs_p)
    y_s, ckv_s, kr_s, s_s = _trunk(False, x_sample.reshape(1, nb * t_s, D_MODEL), mod_s, wts, tabs_s,
                                   s0=state_hgrn[0], cache=(cache_ckv, cache_krope))
    return (y_p, y_s.reshape(nb, t_s, D_MODEL), ckv_p[None], kr_p[None], s_p[None],
            ckv_s.reshape(nb, t_s, KV_LORA)[None], kr_s.reshape(nb, t_s, QK_ROPE)[None], s_s[None])
```

```python
import functools
import math

import numpy as np
import jax
import jax.numpy as jnp
from jax import lax
from jax.experimental import pallas as pl
from jax.experimental.pallas import tpu as pltpu

F32 = jnp.float32
BF16 = jnp.bfloat16

D_MODEL = 1024
CHUNK = 64
EPS = 1e-6
MLA_HEADS = 8
QK_NOPE = 64
QK_ROPE = 32
V_HEAD = 64
Q_LORA = 384
KV_LORA = 256
ROPE_THETA = 10000.0
MLA_WIDTH = MLA_HEADS * V_HEAD
MLA_SCALE = 1.0 / math.sqrt(QK_NOPE + QK_ROPE)
HG_HEADS = 4
HG_EXPAND = 128
HG_VDIM = 128
HG_WIDTH = HG_HEADS * HG_EXPAND
HG_SCALE = 1.0 / math.sqrt(HG_EXPAND)
D_FF = 2816
LOG2E = math.log2(math.e)

LANES = 128
F32_SUBLANES = 8
BF16_SUBLANES = 16
HEAD_PAD = LANES
ROPE_LANE0 = QK_NOPE
VT_ROWS = V_HEAD + BF16_SUBLANES
VMEM_LIMIT = 56 * 1024 * 1024

ROW_TILE = 512
PRE_ROW_TILE = 256
ATT_BLOCK = 1024
ATT_KV_BLOCK = 2048
ATT_SUB = 256
ATT_AHEAD = 1
HG_SUB = F32_SUBLANES
HG_TIME_TILE = 512
FF_CHUNK = 1408

_C_QLAT = 0
_C_KVLAT = _C_QLAT + Q_LORA
_C_HG = _C_KVLAT + KV_LORA
_C_GATE = _C_HG + 4 * HG_WIDTH
_C_KRA = _C_GATE + 2 * D_MODEL
_C_KRB = _C_KRA + LANES
_C_END = _C_KRB + LANES


def _const_spec(shape):
    nd = len(shape)
    return pl.BlockSpec(shape, lambda *_: (0,) * nd, pipeline_mode=pl.Buffered(1))


def _nt(a, b):
    return lax.dot_general(a, b, (((1,), (1,)), ((), ())), preferred_element_type=F32)


def _tn(a, b):
    return lax.dot_general(a, b, (((0,), (0,)), ((), ())), preferred_element_type=F32)


def _dot(a, b):
    return jnp.dot(a, b, preferred_element_type=F32)


def _rms(x, w):
    return x * lax.rsqrt(jnp.mean(x * x, axis=-1, keepdims=True) + EPS) * w


def _sigmoid(x):
    return 1.0 / (1.0 + jnp.exp(-x))


def _ada_kernel(c_ref, w_ref, b_ref, o_ref):
    c = c_ref[...]
    a = (c * _sigmoid(c)).astype(BF16)
    o_ref[...] = _dot(a, w_ref[...].astype(BF16)) + b_ref[...]


def _ada(c, w_ada, b_ada):
    n = c.shape[0]
    cols = w_ada.shape[1]
    blk = D_MODEL
    return pl.pallas_call(
        _ada_kernel,
        grid=(cols // blk,),
        in_specs=[pl.BlockSpec((n, D_MODEL), lambda j: (0, 0)),
                  pl.BlockSpec((D_MODEL, blk), lambda j: (0, j)),
                  pl.BlockSpec((1, blk), lambda j: (0, j))],
        out_specs=pl.BlockSpec((n, blk), lambda j: (0, j)),
        out_shape=jax.ShapeDtypeStruct((n, cols), F32),
        compiler_params=pltpu.CompilerParams(dimension_semantics=("arbitrary",)),
        name="ada",
    )(c, w_ada, b_ada.reshape(1, cols))


def _pre_kernel(prompt, x_ref, sh_ref, sc_ref, n1_ref, win_ref, qn_ref, wqa_ref, kvn_ref,
                cq_ref, sq_ref, ck_ref, sk_ref, *rest):
    if prompt:
        (wuk_ref, wuvt_ref, lb_ref, gn_ref, qt_ref, k_ref, vt_ref, ckv_ref, kr_ref, ho_ref, sout_ref, gg_ref,
         hh_ref, st_sc, sh_sc) = rest
        tstep = pl.program_id(1)

        @pl.when(tstep == 0)
        def _():
            st_sc[...] = jnp.zeros(st_sc.shape, F32)
    else:
        wqb_ref, q_ref, ckv_ref, kr_ref, hh_ref, gg_ref = rest
    x = x_ref[0]
    h = _rms(x, n1_ref[...]) * (1.0 + sc_ref[0]) + sh_ref[0]
    hb = h.astype(BF16)

    def proj(a, b):
        return _dot(hb, win_ref[:, a:b])

    def latents():
        kr_full = proj(_C_KRA, _C_KRB) * ck_ref[...] + proj(_C_KRB, _C_END) * sk_ref[...]
        if prompt:
            kr_ref[0] = kr_full.T[ROPE_LANE0:ROPE_LANE0 + QK_ROPE]
        else:
            kr_ref[0] = kr_full[:, ROPE_LANE0:ROPE_LANE0 + QK_ROPE]
        c_kv = _rms(proj(_C_KVLAT, _C_HG), kvn_ref[...])
        ckv_ref[0] = c_kv
        return kr_full, c_kv, _rms(proj(_C_QLAT, _C_KVLAT), qn_ref[...]).astype(BF16)

    if prompt:
        lat = {}

        def gate_cols(lo, hi):
            gg_ref[0, :, lo:hi] = proj(_C_GATE + lo, _C_GATE + hi)

        def do_latents():
            lat["kr"], c_kv, lat["cq"] = latents()
            lat["ckv"] = c_kv.astype(BF16)

        def do_q():
            qa = _nt(wqa_ref[...], lat["cq"])
            cq = cq_ref[...]
            sq = sq_ref[...]
            half = QK_ROPE // 2
            r0, r1, r2 = ROPE_LANE0, ROPE_LANE0 + half, ROPE_LANE0 + QK_ROPE
            for hd in range(MLA_HEADS):
                qh = qa[hd * HEAD_PAD:(hd + 1) * HEAD_PAD]
                qsw = jnp.concatenate([qh[:r0], qh[r1:r2], qh[r0:r1], qh[r2:]], axis=0)
                qt_ref[0, hd] = ((qh * cq + qsw * sq) * (MLA_SCALE * LOG2E)).astype(BF16)

        def do_kv():
            kall = _dot(lat["ckv"], wuk_ref[...])
            for hd in range(MLA_HEADS):
                sl = slice(hd * HEAD_PAD, (hd + 1) * HEAD_PAD)
                k_ref[0, hd] = (kall[:, sl] + lat["kr"]).astype(BF16)
            vt = _nt(wuvt_ref[...], lat["ckv"])
            ones_rows = (lax.broadcasted_iota(jnp.int32, (VT_ROWS - V_HEAD, vt.shape[1]), 0) == 0).astype(BF16)
            for hd in range(MLA_HEADS):
                vt_ref[0, hd, :V_HEAD] = vt[hd * V_HEAD:(hd + 1) * V_HEAD].astype(BF16)
                vt_ref[0, hd, V_HEAD:] = ones_rows

        gstep = D_MODEL // 2
        mla_work = [functools.partial(gate_cols, 0, gstep), functools.partial(gate_cols, gstep, 2 * gstep),
                    do_latents, functools.partial(gate_cols, 2 * gstep, 3 * gstep), do_q,
                    functools.partial(gate_cols, 3 * gstep, 4 * gstep), do_kv]
        hh_ref[...] = proj(_C_HG, _C_GATE)

        def load(part, j):
            return hh_ref[j * CHUNK:(j + 1) * CHUNK, part * HG_WIDTH:(part + 1) * HG_WIDTH]

        def store(j, o):
            ho_ref[0, j * CHUNK:(j + 1) * CHUNK, :] = o

        _hgrn_blocks(load, store, lb_ref, gn_ref, st_sc, sh_sc, CHUNK, hh_ref.shape[0] // CHUNK, mla_work)

        @pl.when(tstep == pl.num_programs(1) - 1)
        def _():
            for hd in range(HG_HEADS):
                sout_ref[0, hd] = st_sc[hd].T
    else:
        hh_ref[0] = proj(_C_HG, _C_GATE)
        gg_ref[0] = proj(_C_GATE, _C_KRA)
        _, _, cqb = latents()
        qa = _dot(cqb, wqa_ref[...])
        qb = _dot(cqb, wqb_ref[...])
        cq = cq_ref[...]
        sq = sq_ref[...]
        for hd in range(MLA_HEADS):
            sl = slice(hd * HEAD_PAD, (hd + 1) * HEAD_PAD)
            q_ref[0, :, sl] = ((qa[:, sl] * cq + qb[:, sl] * sq) * MLA_SCALE).astype(BF16)


def _pre(prompt, x, sh1, sc1, wts, tabs):
    bp, tp, _ = x.shape
    r = min(PRE_ROW_TILE if prompt else ROW_TILE, tp)
    nt = tp // r
    mod_rows = sh1.shape[1]
    mod_blk = (1, 1, D_MODEL) if mod_rows == 1 else (1, r, D_MODEL)
    mod_map = (lambda b, t: (b, 0, 0)) if mod_rows == 1 else (lambda b, t: (b, t, 0))
    row3 = lambda w: pl.BlockSpec((1, r, w), lambda b, t: (b, t, 0))
    in_specs = [row3(D_MODEL), pl.BlockSpec(mod_blk, mod_map), pl.BlockSpec(mod_blk, mod_map),
                _const_spec((1, D_MODEL)), _const_spec(wts["w_in"].shape), _const_spec((1, Q_LORA)),
                _const_spec(wts["wqa"].shape), _const_spec((1, KV_LORA))]
    if prompt:
        in_specs += [pl.BlockSpec((HEAD_PAD, r), lambda b, t: (0, t))] * 2
    else:
        in_specs += [pl.BlockSpec((r, HEAD_PAD), lambda b, t: (t, 0))] * 2
    in_specs += [pl.BlockSpec((r, HEAD_PAD), lambda b, t: (t, 0))] * 2
    args = [x, sh1, sc1, wts["norm1"], wts["w_in"], wts["q_norm"], wts["wqa"], wts["kv_norm"],
            tabs["cq"], tabs["sq"], tabs["ck"], tabs["sk"]]
    f32_rows = lambda w: jax.ShapeDtypeStruct((bp, tp, w), F32)
    state_shape = (bp, HG_HEADS, HG_EXPAND, HG_VDIM)
    scratch = []
    if prompt:
        in_specs += [_const_spec(wts["wuk"].shape), _const_spec(wts["wuvt"].shape),
                     _const_spec((1, HG_WIDTH)), _const_spec((1, HG_VDIM))]
        args += [wts["wuk"], wts["wuvt"], wts["lb"], wts["hg_norm"]]
        out_shape = [jax.ShapeDtypeStruct((bp, MLA_HEADS, HEAD_PAD, tp), BF16),
                     jax.ShapeDtypeStruct((bp, MLA_HEADS, tp, HEAD_PAD), BF16),
                     jax.ShapeDtypeStruct((bp, MLA_HEADS, VT_ROWS, tp), BF16),
                     f32_rows(KV_LORA), jax.ShapeDtypeStruct((bp, QK_ROPE, tp), F32), f32_rows(HG_WIDTH),
                     jax.ShapeDtypeStruct(state_shape, F32), f32_rows(2 * D_MODEL)]
        out_specs = [pl.BlockSpec((1, MLA_HEADS, HEAD_PAD, r), lambda b, t: (b, 0, 0, t)),
                     pl.BlockSpec((1, MLA_HEADS, r, HEAD_PAD), lambda b, t: (b, 0, t, 0)),
                     pl.BlockSpec((1, MLA_HEADS, VT_ROWS, r), lambda b, t: (b, 0, 0, t)),
                     row3(KV_LORA), pl.BlockSpec((1, QK_ROPE, r), lambda b, t: (b, 0, t)), row3(HG_WIDTH),
                     pl.BlockSpec((1,) + state_shape[1:], lambda b, t: (b, 0, 0, 0)), row3(2 * D_MODEL)]
        scratch = [pltpu.VMEM((r, 4 * HG_WIDTH), F32),
                   pltpu.VMEM((HG_HEADS, HG_VDIM, HG_EXPAND), F32),
                   pltpu.VMEM((r // CHUNK, HG_HEADS, 3, HG_SUB + CHUNK, HG_EXPAND), F32)]
    else:
        in_specs += [_const_spec(wts["wqb"].shape)]
        args += [wts["wqb"]]
        out_shape = [jax.ShapeDtypeStruct((bp, tp, MLA_HEADS * HEAD_PAD), BF16),
                     f32_rows(KV_LORA), f32_rows(QK_ROPE), f32_rows(4 * HG_WIDTH), f32_rows(2 * D_MODEL)]
        out_specs = [row3(MLA_HEADS * HEAD_PAD), row3(KV_LORA), row3(QK_ROPE), row3(4 * HG_WIDTH),
                     row3(2 * D_MODEL)]
    return pl.pallas_call(
        functools.partial(_pre_kernel, prompt),
        grid=(bp, nt),
        in_specs=in_specs,
        out_specs=out_specs,
        out_shape=out_shape,
        scratch_shapes=scratch,
        compiler_params=pltpu.CompilerParams(dimension_semantics=("arbitrary", "arbitrary"),
                                             vmem_limit_bytes=VMEM_LIMIT),
        name="pre_prompt" if prompt else "pre_sample",
    )(*args)


def _attn_kernel(qi_ref, kj_ref, kind_ref, qt_ref, k_ref, vt_ref, o_ref, m_sc, acc_sc, *slabs, blk, kblk, sub):
    p = pl.program_id(1)
    kj = kj_ref[p]
    kind = kind_ref[p]
    ns = blk // sub
    s_bufs, st_bufs = slabs[:ns], slabs[ns:]

    @pl.when(kj == 0)
    def _():
        m_sc[...] = jnp.full(m_sc.shape, -jnp.inf, F32)
        acc_sc[...] = jnp.zeros(acc_sc.shape, F32)

    def run(part):
        diag = part is not None
        if diag:
            kc = lax.broadcasted_iota(jnp.int32, (sub, sub), 0) // CHUNK
            qc = lax.broadcasted_iota(jnp.int32, (sub, sub), 1) // CHUNK
            visible = kc <= qc

        def rows(qs):
            return part * blk + (qs + 1) * sub if diag else kblk

        def scores(hd, qs):
            ql = slice(qs * sub, (qs + 1) * sub)
            nrow = rows(qs)
            s_buf, st_buf = s_bufs[qs], st_bufs[qs]
            q_t = qt_ref[0, hd, :, ql]
            half = max(sub, (nrow // 2) // sub * sub)
            mx = None
            for lo, hi in ((0, half), (half, nrow)):
                if lo == hi:
                    continue
                s = _dot(k_ref[0, hd, lo:hi, :], q_t)
                if diag and hi == nrow:
                    top = nrow - sub - lo
                    dg = jnp.where(visible, s[top:], -jnp.inf)
                    s_buf[nrow - sub:nrow, :] = dg
                    part = jnp.max(dg, axis=0, keepdims=True)
                    if top > 0:
                        s_buf[lo:nrow - sub, :] = s[:top]
                        part = jnp.maximum(part, jnp.max(s[:top], axis=0, keepdims=True))
                else:
                    s_buf[lo:hi, :] = s
                    part = jnp.max(s, axis=0, keepdims=True)
                mx = part if mx is None else jnp.maximum(mx, part)
            m_old = m_sc[hd, :, ql]
            m_new = jnp.maximum(m_old, mx)
            m_sc[hd, :, ql] = m_new
            st_buf[0] = m_new
            st_buf[1] = jnp.exp2(m_old - m_new)

        def weighted(hd, qs):
            ql = slice(qs * sub, (qs + 1) * sub)
            nrow = rows(qs)
            s_buf, st_buf = s_bufs[qs], st_bufs[qs]
            pm = jnp.exp2(s_buf[:nrow, :] - st_buf[0])
            acc_sc[hd, :, ql] = st_buf[1] * acc_sc[hd, :, ql] + _dot(vt_ref[0, hd, :, :nrow], pm.astype(BF16))

        units = [(hd, qs) for hd in range(MLA_HEADS) for qs in range(ns)]
        for u in units[:ATT_AHEAD]:
            scores(*u)
        for i, u in enumerate(units):
            if i + ATT_AHEAD < len(units):
                scores(*units[i + ATT_AHEAD])
            weighted(*u)

    @pl.when(kind == 0)
    def _():
        run(None)

    for part in range(kblk // blk):
        @pl.when(kind == 1 + part)
        def _(part=part):
            run(part)
            o = acc_sc[:, :V_HEAD, :] / acc_sc[:, V_HEAD:V_HEAD + 1, :]
            o_ref[0] = o.reshape(MLA_WIDTH, blk).T.astype(BF16)


def _attn_prompt(qt, k, vt):
    bp, _, _, tp = qt.shape
    blk = min(ATT_BLOCK, tp)
    kblk = min(ATT_KV_BLOCK, tp)
    sub = min(ATT_SUB, blk)
    ns = blk // sub
    ratio = kblk // blk
    pairs = [(q, j) for q in range(tp // blk) for j in range(q // ratio + 1)]
    qi_of = np.array([q for q, _ in pairs], np.int32)
    kj_of = np.array([j for _, j in pairs], np.int32)
    kind_of = np.array([0 if j < q // ratio else 1 + q % ratio for q, j in pairs], np.int32)
    grid_spec = pltpu.PrefetchScalarGridSpec(
        num_scalar_prefetch=3,
        grid=(bp, len(pairs)),
        in_specs=[pl.BlockSpec((1, MLA_HEADS, HEAD_PAD, blk), lambda b, p, qi, kj, kd: (b, 0, 0, qi[p])),
                  pl.BlockSpec((1, MLA_HEADS, kblk, HEAD_PAD), lambda b, p, qi, kj, kd: (b, 0, kj[p], 0)),
                  pl.BlockSpec((1, MLA_HEADS, VT_ROWS, kblk), lambda b, p, qi, kj, kd: (b, 0, 0, kj[p]))],
        out_specs=pl.BlockSpec((1, blk, MLA_WIDTH), lambda b, p, qi, kj, kd: (b, qi[p], 0)),
        scratch_shapes=[pltpu.VMEM((MLA_HEADS, 1, blk), F32),
                        pltpu.VMEM((MLA_HEADS, VT_ROWS, blk), F32)]
                       + [pltpu.VMEM((kblk, sub), F32)] * ns
                       + [pltpu.VMEM((2, 1, sub), F32)] * ns,
    )
    return pl.pallas_call(
        functools.partial(_attn_kernel, blk=blk, kblk=kblk, sub=sub),
        grid_spec=grid_spec,
        out_shape=jax.ShapeDtypeStruct((bp, tp, MLA_WIDTH), BF16),
        compiler_params=pltpu.CompilerParams(dimension_semantics=("arbitrary", "arbitrary"),
                                             vmem_limit_bytes=VMEM_LIMIT),
        name="attn_prompt",
    )(jnp.asarray(qi_of), jnp.asarray(kj_of), jnp.asarray(kind_of), qt, k, vt)


def _attn_sample_kernel(q_ref, cc_ref, ckrt_ref, nc_ref, nkr_ref, wuk_ref, wuvp_ref, o_ref):
    t = q_ref.shape[1]
    q = q_ref[0]
    qh = [q[:, hd * HEAD_PAD:(hd + 1) * HEAD_PAD] for hd in range(MLA_HEADS)]
    q_rope = jnp.concatenate([h[:, ROPE_LANE0:ROPE_LANE0 + QK_ROPE] for h in qh], axis=0)
    q_abs = jnp.concatenate(
        [_nt(qh[hd][:, :QK_NOPE], wuk_ref[hd]) for hd in range(MLA_HEADS)], axis=0).astype(BF16)
    cc = cc_ref[0, 0].astype(BF16)
    nc = nc_ref[0].astype(BF16)
    s_c = _nt(q_abs, cc) + _dot(q_rope, ckrt_ref[0].astype(BF16))
    s_n = _nt(q_abs, nc) + _nt(q_rope, nkr_ref[0].astype(BF16))
    m = jnp.maximum(jnp.max(s_c, axis=-1, keepdims=True), jnp.max(s_n, axis=-1, keepdims=True))
    p_c = jnp.exp(s_c - m)
    p_n = jnp.exp(s_n - m)
    l = jnp.sum(p_c, axis=-1, keepdims=True) + jnp.sum(p_n, axis=-1, keepdims=True)
    o_lat = (_dot(p_c.astype(BF16), cc) + _dot(p_n.astype(BF16), nc)) / l
    o_lat = o_lat.astype(BF16)
    out = jnp.zeros((t, MLA_WIDTH), F32)
    for hd in range(MLA_HEADS):
        out = out + _dot(o_lat[hd * t:(hd + 1) * t], wuvp_ref[hd])
    o_ref[0] = out.astype(BF16)


def _attn_sample(q, cache_ckv, cache_kr, new_ckv, new_kr, wts):
    nb, t, _ = q.shape
    past = cache_ckv.shape[2]
    cache_kr_t = jnp.swapaxes(cache_kr[0], 1, 2)
    return pl.pallas_call(
        _attn_sample_kernel,
        grid=(nb,),
        in_specs=[pl.BlockSpec((1, t, MLA_HEADS * HEAD_PAD), lambda b: (b, 0, 0)),
                  pl.BlockSpec((1, 1, past, KV_LORA), lambda b: (0, b, 0, 0)),
                  pl.BlockSpec((1, QK_ROPE, past), lambda b: (b, 0, 0)),
                  pl.BlockSpec((1, t, KV_LORA), lambda b: (b, 0, 0)),
                  pl.BlockSpec((1, t, QK_ROPE), lambda b: (b, 0, 0)),
                  _const_spec(wts["wuk3"].shape), _const_spec(wts["wuvp"].shape)],
        out_specs=pl.BlockSpec((1, t, MLA_WIDTH), lambda b: (b, 0, 0)),
        out_shape=jax.ShapeDtypeStruct((nb, t, MLA_WIDTH), BF16),
        compiler_params=pltpu.CompilerParams(dimension_semantics=("arbitrary",),
                                             vmem_limit_bytes=VMEM_LIMIT),
        name="attn_sample",
    )(q, cache_ckv, cache_kr_t, new_ckv, new_kr, wts["wuk3"], wts["wuvp"])


def _split3(x):
    hi = x.astype(BF16)
    r1 = x - hi.astype(F32)
    mid = r1.astype(BF16)
    lo = (r1 - mid.astype(F32)).astype(BF16)
    return hi, mid, lo


def _hgrn_blocks(load, store, lb_ref, gn_ref, st_sc, sh_sc, L, nb, fillers=()):
    fillers = list(fillers)

    def fill(count):
        for _ in range(min(count, len(fillers))):
            fillers.pop(0)()

    heads = range(HG_HEADS)
    rmod = lax.broadcasted_iota(jnp.int32, (L, 1), 0) % HG_SUB
    tril = (lax.broadcasted_iota(jnp.int32, (L, L), 0) >= lax.broadcasted_iota(jnp.int32, (L, L), 1)
            ).astype(BF16)
    nsub = L // HG_SUB
    hsl = lambda a, hd: a[:, hd * HG_EXPAND:(hd + 1) * HG_EXPAND]
    tile = lambda a, i: a[HG_SUB * i:HG_SUB * (i + 1)]
    zero_tile = jnp.zeros((HG_SUB, HG_EXPAND), F32)
    lb = lb_ref[...]
    gain = gn_ref[...]
    sh_sc[:, :, :, :HG_SUB, :] = jnp.zeros((nb, HG_HEADS, 3, HG_SUB, HG_EXPAND), F32)

    blocks = []
    for j in range(nb):
        hq, hf, v, hg = (load(part, j) for part in range(4))
        sg = _sigmoid(hf)
        g = jnp.log2(lb + (1.0 - lb) * sg)
        blocks.append(dict(v=v, hg=hg,
                           kk=(1.0 - lb) * (1.0 - sg),
                           qq=hq * _sigmoid(hq) * HG_SCALE,
                           g3=_split3(g)))
    for blk in blocks:
        g_hi, g_mid, g_lo = blk.pop("g3")
        blk["bc"] = _dot(tril, g_hi) + _dot(tril, g_mid) + _dot(tril, g_lo)
    fill(2)
    for blk in blocks:
        bc, qq, kk = blk["bc"], blk["qq"], blk["kk"]
        last = bc[L - 1:L, :]
        blk["qdec"] = (qq * jnp.exp2(bc)).astype(BF16)
        blk["kdec"] = (kk * jnp.exp2(last - bc)).astype(BF16)
        blk["dec"] = jnp.exp2(last)
        blk["vb"] = blk["v"].astype(BF16)
        if nsub > 1:
            blk["q2"], blk["k2"] = [], []
            for hd in heads:
                qh, kh, bh = hsl(qq, hd), hsl(kk, hd), hsl(bc, hd)
                q_rows = [jnp.concatenate([zero_tile] * (nsub - 1), axis=1)]
                k_cols = []
                for i in range(1, nsub):
                    b_i = bh[HG_SUB * i - 1:HG_SUB * i, :]
                    q_i = tile(qh, i) * jnp.exp2(tile(bh, i) - b_i)
                    q_rows.append(jnp.concatenate([zero_tile] * (i - 1) + [q_i] + [zero_tile] * (nsub - 1 - i),
                                                  axis=1))
                    k_i = kh[:HG_SUB * i] * jnp.exp2(b_i - bh[:HG_SUB * i])
                    k_cols.append(jnp.concatenate([k_i] + [zero_tile] * (nsub - i), axis=0))
                blk["q2"].append(jnp.concatenate(q_rows, axis=0).astype(BF16))
                blk["k2"].append(jnp.concatenate(k_cols, axis=1).astype(BF16))
    for blk in blocks:
        if nsub > 1:
            blk["att"] = [_nt(q2, k2).astype(BF16) for q2, k2 in zip(blk.pop("q2"), blk.pop("k2"))]
        blk["upd"] = [_tn(hsl(blk["vb"], hd), hsl(blk["kdec"], hd)) for hd in heads]
    fill(2)
    for j, blk in enumerate(blocks):
        blk["od"] = []
        for hd in heads:
            bc, qq, kk, v = (hsl(blk[name], hd) for name in ("bc", "qq", "kk", "v"))
            sh_sc[j, hd, 0, HG_SUB:, :] = kk
            sh_sc[j, hd, 1, HG_SUB:, :] = bc
            sh_sc[j, hd, 2, HG_SUB:, :] = v
            od = jnp.sum(qq * kk, axis=1, keepdims=True) * v
            for d in range(1, HG_SUB):
                shifted = pl.ds(HG_SUB - d, L)
                e = jnp.exp2(jnp.where(rmod >= d, bc - sh_sc[j, hd, 1, shifted, :], -jnp.inf))
                od = od + (jnp.sum(qq * sh_sc[j, hd, 0, shifted, :] * e, axis=1, keepdims=True)
                           * sh_sc[j, hd, 2, shifted, :])
            blk["od"].append(od)
    for blk in blocks:
        blk["st"] = []
    for hd in heads:
        st = st_sc[hd]
        for blk in blocks:
            blk["st"].append(st.astype(BF16))
            st = st * hsl(blk["dec"], hd) + blk["upd"][hd]
        st_sc[hd] = st
    for blk in blocks:
        o = [_nt(hsl(blk["qdec"], hd), blk["st"][hd]) for hd in heads]
        if nsub > 1:
            o = [o[hd] + _dot(blk["att"][hd], hsl(blk["vb"], hd)) for hd in heads]
        blk["o"] = o
    fill(len(fillers))
    for j, blk in enumerate(blocks):
        hg = blk["hg"]
        gate = hg * _sigmoid(hg)
        store(j, jnp.concatenate([_rms(blk["o"][hd] + blk["od"][hd], gain) * hsl(gate, hd) for hd in heads],
                                 axis=1))


def _hgrn_kernel(hh_ref, lb_ref, gn_ref, s0_ref, o_ref, sout_ref, st_sc, sh_sc, *, L, nb):
    tstep = pl.program_id(1)

    @pl.when(tstep == 0)
    def _():
        for hd in range(HG_HEADS):
            st_sc[hd] = s0_ref[0, hd].T

    def load(part, j):
        return hh_ref[0, j * L:(j + 1) * L, part * HG_WIDTH:(part + 1) * HG_WIDTH]

    def store(j, o):
        o_ref[0, j * L:(j + 1) * L, :] = o

    _hgrn_blocks(load, store, lb_ref, gn_ref, st_sc, sh_sc, L, nb)

    @pl.when(tstep == pl.num_programs(1) - 1)
    def _():
        for hd in range(HG_HEADS):
            sout_ref[0, hd] = st_sc[hd].T


def _hgrn(hh, lb, hg_norm, s0, L):
    bp, tp, _ = hh.shape
    tt = min(HG_TIME_TILE, tp)
    nb = tt // L
    return pl.pallas_call(
        functools.partial(_hgrn_kernel, L=L, nb=nb),
        grid=(bp, tp // tt),
        in_specs=[pl.BlockSpec((1, tt, 4 * HG_WIDTH), lambda b, t: (b, t, 0)),
                  _const_spec((1, HG_WIDTH)), _const_spec((1, HG_VDIM)),
                  pl.BlockSpec((1, HG_HEADS, HG_EXPAND, HG_VDIM), lambda b, t: (b, 0, 0, 0))],
        out_specs=[pl.BlockSpec((1, tt, HG_WIDTH), lambda b, t: (b, t, 0)),
                   pl.BlockSpec((1, HG_HEADS, HG_EXPAND, HG_VDIM), lambda b, t: (b, 0, 0, 0))],
        out_shape=[jax.ShapeDtypeStruct((bp, tp, HG_WIDTH), F32),
                   jax.ShapeDtypeStruct((bp, HG_HEADS, HG_EXPAND, HG_VDIM), F32)],
        scratch_shapes=[pltpu.VMEM((HG_HEADS, HG_VDIM, HG_EXPAND), F32),
                        pltpu.VMEM((nb, HG_HEADS, 3, HG_SUB + L, HG_EXPAND), F32)],
        compiler_params=pltpu.CompilerParams(dimension_semantics=("arbitrary", "arbitrary"),
                                             vmem_limit_bytes=VMEM_LIMIT),
        name="hgrn",
    )(hh, lb, hg_norm, s0)


def _post_kernel(x_ref, at_ref, ho_ref, gg_ref, g1_ref, sh2_ref, sc2_ref, g2_ref,
                 wpa_ref, wpb_ref, wout_ref, n2_ref, wgu_ref, wdown_ref, fn_ref, y_ref):
    x = x_ref[0]
    ya = _dot(at_ref[0], wpa_ref[...])
    yb = _dot(ho_ref[0].astype(BF16), wpb_ref[...])
    mixed = _sigmoid(gg_ref[0, :, :D_MODEL]) * ya + _sigmoid(gg_ref[0, :, D_MODEL:]) * yb
    x1 = x + g1_ref[0] * _dot(mixed.astype(BF16), wout_ref[...])
    hb = (_rms(x1, n2_ref[...]) * (1.0 + sc2_ref[0]) + sh2_ref[0]).astype(BF16)
    ff = jnp.zeros(x.shape, F32)
    for c in range(D_FF // FF_CHUNK):
        gate = _dot(hb, wgu_ref[:, c * FF_CHUNK:(c + 1) * FF_CHUNK])
        up = _dot(hb, wgu_ref[:, D_FF + c * FF_CHUNK:D_FF + (c + 1) * FF_CHUNK])
        act = (gate * _sigmoid(gate) * up).astype(BF16)
        ff = ff + _dot(act, wdown_ref[c * FF_CHUNK:(c + 1) * FF_CHUNK, :])
    x2 = x1 + g2_ref[0] * ff
    y_ref[0] = _rms(x2, fn_ref[...])


def _post(x, attn, ho, gg, g1, sh2, sc2, g2, wts):
    bp, tp, _ = x.shape
    r = min(ROW_TILE, tp)
    mod_rows = g1.shape[1]
    mod_blk = (1, 1, D_MODEL) if mod_rows == 1 else (1, r, D_MODEL)
    mod_map = (lambda b, t: (b, 0, 0)) if mod_rows == 1 else (lambda b, t: (b, t, 0))
    row3 = lambda w: pl.BlockSpec((1, r, w), lambda b, t: (b, t, 0))
    mod = pl.BlockSpec(mod_blk, mod_map)
    names = ["wpa", "wpb", "wout", "norm2", "wgu", "wdown", "final_norm"]
    return pl.pallas_call(
        _post_kernel,
        grid=(bp, tp // r),
        in_specs=[row3(D_MODEL), row3(MLA_WIDTH), row3(HG_WIDTH), row3(2 * D_MODEL), mod, mod, mod, mod]
                 + [_const_spec(wts[n].shape) for n in names],
        out_specs=row3(D_MODEL),
        out_shape=jax.ShapeDtypeStruct((bp, tp, D_MODEL), F32),
        compiler_params=pltpu.CompilerParams(dimension_semantics=("arbitrary", "arbitrary"),
                                             vmem_limit_bytes=VMEM_LIMIT),
        name="post",
    )(x, attn, ho, gg, g1, sh2, sc2, g2, *[wts[n] for n in names])


def _prep_weights(w_in, q_norm, w_uq, kv_norm, w_ukv, lb_param, hg_norm, w_pa, w_pb, w_out,
                  norm1, norm2, w_gu, w_down, final_norm):
    half = QK_ROPE // 2
    offs = np.cumsum([0, Q_LORA, KV_LORA, QK_ROPE, HG_WIDTH, HG_WIDTH, HG_WIDTH, HG_WIDTH, D_MODEL, D_MODEL])
    w = w_in[0]
    w_kr = w[:, offs[2]:offs[3]]
    zeros = lambda n: jnp.zeros((D_MODEL, n), w.dtype)
    kr_a = jnp.concatenate([zeros(ROPE_LANE0), w_kr, zeros(LANES - ROPE_LANE0 - QK_ROPE)], axis=1)
    kr_b = jnp.concatenate([zeros(ROPE_LANE0), w_kr[:, half:], w_kr[:, :half],
                            zeros(LANES - ROPE_LANE0 - QK_ROPE)], axis=1)
    w_in_p = jnp.concatenate([w[:, offs[0]:offs[2]], w[:, offs[3]:offs[9]], kr_a, kr_b], axis=1).astype(BF16)

    uq = w_uq[0].reshape(Q_LORA, MLA_HEADS, QK_NOPE + QK_ROPE)
    pad = jnp.zeros((Q_LORA, MLA_HEADS, HEAD_PAD - QK_NOPE - QK_ROPE), uq.dtype)
    wqa = jnp.concatenate([uq, pad], axis=2).reshape(Q_LORA, MLA_HEADS * HEAD_PAD).astype(BF16)
    wqb = jnp.concatenate([jnp.zeros((Q_LORA, MLA_HEADS, QK_NOPE), uq.dtype),
                           uq[:, :, QK_NOPE + half:], uq[:, :, QK_NOPE:QK_NOPE + half], pad],
                          axis=2).reshape(Q_LORA, MLA_HEADS * HEAD_PAD).astype(BF16)

    ukv = w_ukv[0].reshape(KV_LORA, MLA_HEADS, QK_NOPE + V_HEAD)
    uk = ukv[:, :, :QK_NOPE]
    uv = ukv[:, :, QK_NOPE:]
    wuk = jnp.concatenate([uk, jnp.zeros((KV_LORA, MLA_HEADS, HEAD_PAD - QK_NOPE), uk.dtype)],
                          axis=2).reshape(KV_LORA, MLA_HEADS * HEAD_PAD).astype(BF16)
    wuvt = uv.reshape(KV_LORA, MLA_WIDTH).T.astype(BF16)
    wuk3 = jnp.transpose(uk, (1, 0, 2)).astype(BF16)
    eye = jnp.eye(MLA_HEADS, dtype=uv.dtype)
    wuvp = (jnp.transpose(uv, (1, 0, 2))[:, :, None, :] * eye[:, None, :, None]
            ).reshape(MLA_HEADS, KV_LORA, MLA_WIDTH).astype(BF16)

    lb = jnp.cumsum(jax.nn.softmax(lb_param.astype(F32), axis=0), axis=0)[0].reshape(1, HG_WIDTH)
    return {
        "w_in": w_in_p, "wqa": wqa, "wqb": wqb, "wqa_t": wqa.T,
        "wuk": wuk, "wuvt": wuvt, "wuk3": wuk3, "wuvp": wuvp, "lb": lb,
        "q_norm": q_norm[0].reshape(1, Q_LORA), "kv_norm": kv_norm[0].reshape(1, KV_LORA),
        "hg_norm": hg_norm[0].reshape(1, HG_VDIM),
        "norm1": norm1[0].reshape(1, D_MODEL), "norm2": norm2[0].reshape(1, D_MODEL),
        "final_norm": final_norm.reshape(1, D_MODEL),
        "wpa": w_pa[0].astype(BF16), "wpb": w_pb[0].astype(BF16), "wout": w_out[0].astype(BF16),
        "wgu": w_gu[0].astype(BF16), "wdown": w_down[0].astype(BF16),
    }


def _rope_tables(pos):
    half = QK_ROPE // 2
    inv = ROPE_THETA ** (-np.arange(half, dtype=np.float64) / half)
    ang = np.asarray(pos, np.float64)[:, None] * inv[None, :]
    cos, sin = np.cos(ang), np.sin(ang)
    t = len(pos)
    ck = np.zeros((t, HEAD_PAD))
    sk = np.zeros((t, HEAD_PAD))
    ck[:, ROPE_LANE0:ROPE_LANE0 + QK_ROPE] = np.concatenate([cos, cos], axis=1)
    sk[:, ROPE_LANE0:ROPE_LANE0 + QK_ROPE] = np.concatenate([-sin, sin], axis=1)
    cq = ck.copy()
    cq[:, :QK_NOPE] = 1.0
    f = lambda a: jnp.asarray(a.astype(np.float32))
    return {"ck": f(ck), "sk": f(sk), "cq": f(cq), "sq": f(sk)}


def _trunk(prompt, x, mod, wts, tabs, s0=None, cache=None):
    sh1, sc1, g1, sh2, sc2, g2 = mod
    if prompt:
        w = dict(wts, wqa=wts["wqa_t"])
        t = dict(tabs, cq=tabs["cq"].T, sq=tabs["sq"].T)
        qt, k, vt, ckv, kr_t, ho, s_new, gg = _pre(True, x, sh1, sc1, w, t)
        kr = jnp.swapaxes(kr_t, 1, 2)
        attn = _attn_prompt(qt, k, vt)
    else:
        q, ckv, kr, hh, gg = _pre(False, x, sh1, sc1, wts, tabs)
        nb, t_new = cache[0].shape[1], x.shape[1] // cache[0].shape[1]
        attn = _attn_sample(q.reshape(nb, t_new, -1), cache[0], cache[1],
                            ckv.reshape(nb, t_new, KV_LORA), kr.reshape(nb, t_new, QK_ROPE), wts)
        attn = attn.reshape(1, nb * t_new, MLA_WIDTH)
        ho, s_new = _hgrn(hh.reshape(nb, t_new, 4 * HG_WIDTH), wts["lb"], wts["hg_norm"], s0, t_new)
        ho = ho.reshape(x.shape[0], x.shape[1], HG_WIDTH)
    y = _post(x, attn, ho, gg, g1, sh2, sc2, g2, wts)
    return y, ckv, kr, s_new


def kernel(x_prompt, x_sample, cache_ckv, cache_krope, state_hgrn, c_prompt, c_sample, w_in, q_norm, w_uq,
           kv_norm, w_ukv, lb_param, hg_norm, w_pa, w_pb, w_out, norm1, norm2, w_ada, b_ada, w_gu, w_down,
           final_norm):
    bsz, t_p, _ = x_prompt.shape
    nb, t_s, _ = x_sample.shape
    past = cache_ckv.shape[2]
    assert past % CHUNK == 0 and t_s <= CHUNK
    wts = _prep_weights(w_in, q_norm, w_uq, kv_norm, w_ukv, lb_param, hg_norm, w_pa, w_pb, w_out,
                        norm1, norm2, w_gu, w_down, final_norm)

    n_c = bsz + nb
    n_pad = -n_c % 8
    c_all = jnp.concatenate([c_prompt, c_sample, jnp.zeros((n_pad, D_MODEL), F32)], axis=0)
    mod_all = _ada(c_all, w_ada[0], b_ada[0])
    mod_p = [m[:, None, :] for m in jnp.split(mod_all[:bsz], 6, axis=-1)]
    mod_s = [jnp.repeat(m, t_s, axis=0)[None] for m in jnp.split(mod_all[bsz:n_c], 6, axis=-1)]

    tabs_p = _rope_tables(np.arange(t_p))
    tabs_s = _rope_tables(np.tile(past + np.arange(t_s), nb))

    y_p, ckv_p, kr_p, s_p = _trunk(True, x_prompt, mod_p, wts, tabs_p)
    y_s, ckv_s, kr_s, s_s = _trunk(False, x_sample.reshape(1, nb * t_s, D_MODEL), mod_s, wts, tabs_s,
                                   s0=state_hgrn[0], cache=(cache_ckv, cache_krope))
    return (y_p, y_s.reshape(nb, t_s, D_MODEL), ckv_p[None], kr_p[None], s_p[None],
            ckv_s.reshape(nb, t_s, KV_LORA)[None], kr_s.reshape(nb, t_s, QK_ROPE)[None], s_s[None])
```

```python
import functools
import math

import numpy as np
import jax
import jax.numpy as jnp
from jax import lax
from jax.experimental import pallas as pl
from jax.experimental.pallas import tpu as pltpu

F32 = jnp.float32
BF16 = jnp.bfloat16

D_MODEL = 1024
CHUNK = 64
EPS = 1e-6
MLA_HEADS = 8
QK_NOPE = 64
QK_ROPE = 32
V_HEAD = 64
Q_LORA = 384
KV_LORA = 256
ROPE_THETA = 10000.0
MLA_WIDTH = MLA_HEADS * V_HEAD
MLA_SCALE = 1.0 / math.sqrt(QK_NOPE + QK_ROPE)
HG_HEADS = 4
HG_EXPAND = 128
HG_VDIM = 128
HG_WIDTH = HG_HEADS * HG_EXPAND
HG_SCALE = 1.0 / math.sqrt(HG_EXPAND)
D_FF = 2816
LOG2E = math.log2(math.e)

LANES = 128
F32_SUBLANES = 8
BF16_SUBLANES = 16
HEAD_PAD = LANES
ROPE_LANE0 = QK_NOPE
VT_ROWS = V_HEAD + BF16_SUBLANES
VMEM_LIMIT = 56 * 1024 * 1024
PRE_VMEM_LIMIT = 60 * 1024 * 1024

ROW_TILE = 512
PRE_ROW_TILE = 512
ATT_BLOCK = 1024
ATT_KV_BLOCK = 2048
ATT_SUB = 256
ATT_AHEAD = 1
HG_SUB = F32_SUBLANES
HG_TIME_TILE = 512
FF_CHUNK = 1408

_C_QLAT = 0
_C_KVLAT = _C_QLAT + Q_LORA
_C_HG = _C_KVLAT + KV_LORA
_C_GATE = _C_HG + 4 * HG_WIDTH
_C_KRA = _C_GATE + 2 * D_MODEL
_C_KRB = _C_KRA + LANES
_C_END = _C_KRB + LANES


def _const_spec(shape):
    nd = len(shape)
    return pl.BlockSpec(shape, lambda *_: (0,) * nd, pipeline_mode=pl.Buffered(1))


def _nt(a, b):
    return lax.dot_general(a, b, (((1,), (1,)), ((), ())), preferred_element_type=F32)


def _tn(a, b):
    return lax.dot_general(a, b, (((0,), (0,)), ((), ())), preferred_element_type=F32)


def _dot(a, b):
    return jnp.dot(a, b, preferred_element_type=F32)


def _rms(x, w):
    return x * lax.rsqrt(jnp.mean(x * x, axis=-1, keepdims=True) + EPS) * w


def _sigmoid(x):
    return 1.0 / (1.0 + jnp.exp(-x))


def _ada_kernel(c_ref, w_ref, b_ref, o_ref):
    c = c_ref[...]
    a = (c * _sigmoid(c)).astype(BF16)
    o_ref[...] = _dot(a, w_ref[...].astype(BF16)) + b_ref[...]


def _ada(c, w_ada, b_ada):
    n = c.shape[0]
    cols = w_ada.shape[1]
    blk = D_MODEL
    return pl.pallas_call(
        _ada_kernel,
        grid=(cols // blk,),
        in_specs=[pl.BlockSpec((n, D_MODEL), lambda j: (0, 0)),
                  pl.BlockSpec((D_MODEL, blk), lambda j: (0, j)),
                  pl.BlockSpec((1, blk), lambda j: (0, j))],
        out_specs=pl.BlockSpec((n, blk), lambda j: (0, j)),
        out_shape=jax.ShapeDtypeStruct((n, cols), F32),
        compiler_params=pltpu.CompilerParams(dimension_semantics=("arbitrary",)),
        name="ada",
    )(c, w_ada, b_ada.reshape(1, cols))


def _pre_kernel(prompt, x_ref, sh_ref, sc_ref, n1_ref, win_ref, qn_ref, wqa_ref, kvn_ref,
                cq_ref, sq_ref, ck_ref, sk_ref, *rest):
    if prompt:
        (wuk_ref, wuvt_ref, lb_ref, gn_ref, qt_ref, k_ref, vt_ref, ckv_ref, kr_ref, ho_ref, sout_ref, gg_ref,
         hh_ref, st_sc, sh_sc) = rest
        tstep = pl.program_id(1)

        @pl.when(tstep == 0)
        def _():
            st_sc[...] = jnp.zeros(st_sc.shape, F32)
    else:
        wqb_ref, q_ref, ckv_ref, kr_ref, hh_ref, gg_ref = rest
    x = x_ref[0]
    h = _rms(x, n1_ref[...]) * (1.0 + sc_ref[0]) + sh_ref[0]
    hb = h.astype(BF16)

    def proj(a, b):
        return _dot(hb, win_ref[:, a:b])

    def latents():
        kr_full = proj(_C_KRA, _C_KRB) * ck_ref[...] + proj(_C_KRB, _C_END) * sk_ref[...]
        if prompt:
            kr_ref[0] = kr_full.T[ROPE_LANE0:ROPE_LANE0 + QK_ROPE]
        else:
            kr_ref[0] = kr_full[:, ROPE_LANE0:ROPE_LANE0 + QK_ROPE]
        c_kv = _rms(proj(_C_KVLAT, _C_HG), kvn_ref[...])
        ckv_ref[0] = c_kv
        return kr_full, c_kv, _rms(proj(_C_QLAT, _C_KVLAT), qn_ref[...]).astype(BF16)

    if prompt:
        lat = {}

        def gate_cols(lo, hi):
            gg_ref[0, :, lo:hi] = proj(_C_GATE + lo, _C_GATE + hi)

        def do_latents():
            lat["kr"], c_kv, lat["cq"] = latents()
            lat["ckv"] = c_kv.astype(BF16)

        def do_q():
            qa = _nt(wqa_ref[...], lat["cq"])
            cq = cq_ref[...]
            sq = sq_ref[...]
            half = QK_ROPE // 2
            r0, r1, r2 = ROPE_LANE0, ROPE_LANE0 + half, ROPE_LANE0 + QK_ROPE
            for hd in range(MLA_HEADS):
                qh = qa[hd * HEAD_PAD:(hd + 1) * HEAD_PAD]
                qsw = jnp.concatenate([qh[:r0], qh[r1:r2], qh[r0:r1], qh[r2:]], axis=0)
                qt_ref[0, hd] = ((qh * cq + qsw * sq) * (MLA_SCALE * LOG2E)).astype(BF16)

        def do_kv():
            kall = _dot(lat["ckv"], wuk_ref[...])
            for hd in range(MLA_HEADS):
                sl = slice(hd * HEAD_PAD, (hd + 1) * HEAD_PAD)
                k_ref[0, hd] = (kall[:, sl] + lat["kr"]).astype(BF16)
            vt = _nt(wuvt_ref[...], lat["ckv"])
            ones_rows = (lax.broadcasted_iota(jnp.int32, (VT_ROWS - V_HEAD, vt.shape[1]), 0) == 0).astype(BF16)
            for hd in range(MLA_HEADS):
                vt_ref[0, hd, :V_HEAD] = vt[hd * V_HEAD:(hd + 1) * V_HEAD].astype(BF16)
                vt_ref[0, hd, V_HEAD:] = ones_rows

        gstep = D_MODEL // 2
        mla_work = [functools.partial(gate_cols, 0, gstep), functools.partial(gate_cols, gstep, 2 * gstep),
                    do_latents, functools.partial(gate_cols, 2 * gstep, 3 * gstep), do_q,
                    functools.partial(gate_cols, 3 * gstep, 4 * gstep), do_kv]
        hh_ref[...] = proj(_C_HG, _C_GATE)

        def load(part, j):
            return hh_ref[j * CHUNK:(j + 1) * CHUNK, part * HG_WIDTH:(part + 1) * HG_WIDTH]

        def store(j, o):
            ho_ref[0, j * CHUNK:(j + 1) * CHUNK, :] = o

        _hgrn_blocks(load, store, lb_ref, gn_ref, st_sc, sh_sc, CHUNK, hh_ref.shape[0] // CHUNK, mla_work)

        @pl.when(tstep == pl.num_programs(1) - 1)
        def _():
            for hd in range(HG_HEADS):
                sout_ref[0, hd] = st_sc[hd].T
    else:
        hh_ref[0] = proj(_C_HG, _C_GATE)
        gg_ref[0] = proj(_C_GATE, _C_KRA)
        _, _, cqb = latents()
        qa = _dot(cqb, wqa_ref[...])
        qb = _dot(cqb, wqb_ref[...])
        cq = cq_ref[...]
        sq = sq_ref[...]
        for hd in range(MLA_HEADS):
            sl = slice(hd * HEAD_PAD, (hd + 1) * HEAD_PAD)
            q_ref[0, :, sl] = ((qa[:, sl] * cq + qb[:, sl] * sq) * MLA_SCALE).astype(BF16)


def _pre(prompt, x, sh1, sc1, wts, tabs):
    bp, tp, _ = x.shape
    r = min(PRE_ROW_TILE if prompt else ROW_TILE, tp)
    nt = tp // r
    mod_rows = sh1.shape[1]
    mod_blk = (1, 1, D_MODEL) if mod_rows == 1 else (1, r, D_MODEL)
    mod_map = (lambda b, t: (b, 0, 0)) if mod_rows == 1 else (lambda b, t: (b, t, 0))
    row3 = lambda w: pl.BlockSpec((1, r, w), lambda b, t: (b, t, 0))
    in_specs = [row3(D_MODEL), pl.BlockSpec(mod_blk, mod_map), pl.BlockSpec(mod_blk, mod_map),
                _const_spec((1, D_MODEL)), _const_spec(wts["w_in"].shape), _const_spec((1, Q_LORA)),
                _const_spec(wts["wqa"].shape), _const_spec((1, KV_LORA))]
    if prompt:
        in_specs += [pl.BlockSpec((HEAD_PAD, r), lambda b, t: (0, t))] * 2
    else:
        in_specs += [pl.BlockSpec((r, HEAD_PAD), lambda b, t: (t, 0))] * 2
    in_specs += [pl.BlockSpec((r, HEAD_PAD), lambda b, t: (t, 0))] * 2
    args = [x, sh1, sc1, wts["norm1"], wts["w_in"], wts["q_norm"], wts["wqa"], wts["kv_norm"],
            tabs["cq"], tabs["sq"], tabs["ck"], tabs["sk"]]
    f32_rows = lambda w: jax.ShapeDtypeStruct((bp, tp, w), F32)
    state_shape = (bp, HG_HEADS, HG_EXPAND, HG_VDIM)
    scratch = []
    if prompt:
        in_specs += [_const_spec(wts["wuk"].shape), _const_spec(wts["wuvt"].shape),
                     _const_spec((1, HG_WIDTH)), _const_spec((1, HG_VDIM))]
        args += [wts["wuk"], wts["wuvt"], wts["lb"], wts["hg_norm"]]
        out_shape = [jax.ShapeDtypeStruct((bp, MLA_HEADS, HEAD_PAD, tp), BF16),
                     jax.ShapeDtypeStruct((bp, MLA_HEADS, tp, HEAD_PAD), BF16),
                     jax.ShapeDtypeStruct((bp, MLA_HEADS, VT_ROWS, tp), BF16),
                     f32_rows(KV_LORA), jax.ShapeDtypeStruct((bp, QK_ROPE, tp), F32), f32_rows(HG_WIDTH),
                     jax.ShapeDtypeStruct(state_shape, F32), f32_rows(2 * D_MODEL)]
        out_specs = [pl.BlockSpec((1, MLA_HEADS, HEAD_PAD, r), lambda b, t: (b, 0, 0, t)),
                     pl.BlockSpec((1, MLA_HEADS, r, HEAD_PAD), lambda b, t: (b, 0, t, 0)),
                     pl.BlockSpec((1, MLA_HEADS, VT_ROWS, r), lambda b, t: (b, 0, 0, t)),
                     row3(KV_LORA), pl.BlockSpec((1, QK_ROPE, r), lambda b, t: (b, 0, t)), row3(HG_WIDTH),
                     pl.BlockSpec((1,) + state_shape[1:], lambda b, t: (b, 0, 0, 0)), row3(2 * D_MODEL)]
        scratch = [pltpu.VMEM((r, 4 * HG_WIDTH), F32),
                   pltpu.VMEM((HG_HEADS, HG_VDIM, HG_EXPAND), F32),
                   pltpu.VMEM((r // CHUNK, HG_HEADS, 3, HG_SUB + CHUNK, HG_EXPAND), F32)]
    else:
        in_specs += [_const_spec(wts["wqb"].shape)]
        args += [wts["wqb"]]
        out_shape = [jax.ShapeDtypeStruct((bp, tp, MLA_HEADS * HEAD_PAD), BF16),
                     f32_rows(KV_LORA), f32_rows(QK_ROPE), f32_rows(4 * HG_WIDTH), f32_rows(2 * D_MODEL)]
        out_specs = [row3(MLA_HEADS * HEAD_PAD), row3(KV_LORA), row3(QK_ROPE), row3(4 * HG_WIDTH),
                     row3(2 * D_MODEL)]
    return pl.pallas_call(
        functools.partial(_pre_kernel, prompt),
        grid=(bp, nt),
        in_specs=in_specs,
        out_specs=out_specs,
        out_shape=out_shape,
        scratch_shapes=scratch,
        compiler_params=pltpu.CompilerParams(dimension_semantics=("arbitrary", "arbitrary"),
                                             vmem_limit_bytes=PRE_VMEM_LIMIT),
        name="pre_prompt" if prompt else "pre_sample",
    )(*args)


def _attn_kernel(qi_ref, kj_ref, kind_ref, qt_ref, k_ref, vt_ref, o_ref, m_sc, acc_sc, *slabs, blk, kblk, sub):
    p = pl.program_id(1)
    kj = kj_ref[p]
    kind = kind_ref[p]
    ns = blk // sub
    s_bufs, st_bufs = slabs[:ns], slabs[ns:]

    @pl.when(kj == 0)
    def _():
        m_sc[...] = jnp.full(m_sc.shape, -jnp.inf, F32)
        acc_sc[...] = jnp.zeros(acc_sc.shape, F32)

    def run(part):
        diag = part is not None
        if diag:
            kc = lax.broadcasted_iota(jnp.int32, (sub, sub), 0) // CHUNK
            qc = lax.broadcasted_iota(jnp.int32, (sub, sub), 1) // CHUNK
            visible = kc <= qc

        def rows(qs):
            return part * blk + (qs + 1) * sub if diag else kblk

        def scores(hd, qs):
            ql = slice(qs * sub, (qs + 1) * sub)
            nrow = rows(qs)
            s_buf, st_buf = s_bufs[qs], st_bufs[qs]
            q_t = qt_ref[0, hd, :, ql]
            half = max(sub, (nrow // 2) // sub * sub)
            mx = None
            for lo, hi in ((0, half), (half, nrow)):
                if lo == hi:
                    continue
                s = _dot(k_ref[0, hd, lo:hi, :], q_t)
                if diag and hi == nrow:
                    top = nrow - sub - lo
                    dg = jnp.where(visible, s[top:], -jnp.inf)
                    s_buf[nrow - sub:nrow, :] = dg
                    part = jnp.max(dg, axis=0, keepdims=True)
                    if top > 0:
                        s_buf[lo:nrow - sub, :] = s[:top]
                        part = jnp.maximum(part, jnp.max(s[:top], axis=0, keepdims=True))
                else:
                    s_buf[lo:hi, :] = s
                    part = jnp.max(s, axis=0, keepdims=True)
                mx = part if mx is None else jnp.maximum(mx, part)
            m_old = m_sc[hd, :, ql]
            m_new = jnp.maximum(m_old, mx)
            m_sc[hd, :, ql] = m_new
            st_buf[0] = m_new
            st_buf[1] = jnp.exp2(m_old - m_new)

        def weighted(hd, qs):
            ql = slice(qs * sub, (qs + 1) * sub)
            nrow = rows(qs)
            s_buf, st_buf = s_bufs[qs], st_bufs[qs]
            pm = jnp.exp2(s_buf[:nrow, :] - st_buf[0])
            acc_sc[hd, :, ql] = st_buf[1] * acc_sc[hd, :, ql] + _dot(vt_ref[0, hd, :, :nrow], pm.astype(BF16))

        units = [(hd, qs) for hd in range(MLA_HEADS) for qs in range(ns)]
        for u in units[:ATT_AHEAD]:
            scores(*u)
        for i, u in enumerate(units):
            if i + ATT_AHEAD < len(units):
                scores(*units[i + ATT_AHEAD])
            weighted(*u)

    @pl.when(kind == 0)
    def _():
        run(None)

    for part in range(kblk // blk):
        @pl.when(kind == 1 + part)
        def _(part=part):
            run(part)
            o = acc_sc[:, :V_HEAD, :] / acc_sc[:, V_HEAD:V_HEAD + 1, :]
            o_ref[0] = o.reshape(MLA_WIDTH, blk).T.astype(BF16)


def _attn_prompt(qt, k, vt):
    bp, _, _, tp = qt.shape
    blk = min(ATT_BLOCK, tp)
    kblk = min(ATT_KV_BLOCK, tp)
    sub = min(ATT_SUB, blk)
    ns = blk // sub
    ratio = kblk // blk
    pairs = [(q, j) for q in range(tp // blk) for j in range(q // ratio + 1)]
    qi_of = np.array([q for q, _ in pairs], np.int32)
    kj_of = np.array([j for _, j in pairs], np.int32)
    kind_of = np.array([0 if j < q // ratio else 1 + q % ratio for q, j in pairs], np.int32)
    grid_spec = pltpu.PrefetchScalarGridSpec(
        num_scalar_prefetch=3,
        grid=(bp, len(pairs)),
        in_specs=[pl.BlockSpec((1, MLA_HEADS, HEAD_PAD, blk), lambda b, p, qi, kj, kd: (b, 0, 0, qi[p])),
                  pl.BlockSpec((1, MLA_HEADS, kblk, HEAD_PAD), lambda b, p, qi, kj, kd: (b, 0, kj[p], 0)),
                  pl.BlockSpec((1, MLA_HEADS, VT_ROWS, kblk), lambda b, p, qi, kj, kd: (b, 0, 0, kj[p]))],
        out_specs=pl.BlockSpec((1, blk, MLA_WIDTH), lambda b, p, qi, kj, kd: (b, qi[p], 0)),
        scratch_shapes=[pltpu.VMEM((MLA_HEADS, 1, blk), F32),
                        pltpu.VMEM((MLA_HEADS, VT_ROWS, blk), F32)]
                       + [pltpu.VMEM((kblk, sub), F32)] * ns
                       + [pltpu.VMEM((2, 1, sub), F32)] * ns,
    )
    return pl.pallas_call(
        functools.partial(_attn_kernel, blk=blk, kblk=kblk, sub=sub),
        grid_spec=grid_spec,
        out_shape=jax.ShapeDtypeStruct((bp, tp, MLA_WIDTH), BF16),
        compiler_params=pltpu.CompilerParams(dimension_semantics=("arbitrary", "arbitrary"),
                                             vmem_limit_bytes=VMEM_LIMIT),
        name="attn_prompt",
    )(jnp.asarray(qi_of), jnp.asarray(kj_of), jnp.asarray(kind_of), qt, k, vt)


def _attn_sample_kernel(q_ref, cc_ref, ckrt_ref, nc_ref, nkr_ref, wuk_ref, wuvp_ref, o_ref):
    t = q_ref.shape[1]
    q = q_ref[0]
    qh = [q[:, hd * HEAD_PAD:(hd + 1) * HEAD_PAD] for hd in range(MLA_HEADS)]
    q_rope = jnp.concatenate([h[:, ROPE_LANE0:ROPE_LANE0 + QK_ROPE] for h in qh], axis=0)
    q_abs = jnp.concatenate(
        [_nt(qh[hd][:, :QK_NOPE], wuk_ref[hd]) for hd in range(MLA_HEADS)], axis=0).astype(BF16)
    cc = cc_ref[0, 0].astype(BF16)
    nc = nc_ref[0].astype(BF16)
    s_c = _nt(q_abs, cc) + _dot(q_rope, ckrt_ref[0].astype(BF16))
    s_n = _nt(q_abs, nc) + _nt(q_rope, nkr_ref[0].astype(BF16))
    m = jnp.maximum(jnp.max(s_c, axis=-1, keepdims=True), jnp.max(s_n, axis=-1, keepdims=True))
    p_c = jnp.exp(s_c - m)
    p_n = jnp.exp(s_n - m)
    l = jnp.sum(p_c, axis=-1, keepdims=True) + jnp.sum(p_n, axis=-1, keepdims=True)
    o_lat = (_dot(p_c.astype(BF16), cc) + _dot(p_n.astype(BF16), nc)) / l
    o_lat = o_lat.astype(BF16)
    out = jnp.zeros((t, MLA_WIDTH), F32)
    for hd in range(MLA_HEADS):
        out = out + _dot(o_lat[hd * t:(hd + 1) * t], wuvp_ref[hd])
    o_ref[0] = out.astype(BF16)


def _attn_sample(q, cache_ckv, cache_kr, new_ckv, new_kr, wts):
    nb, t, _ = q.shape
    past = cache_ckv.shape[2]
    cache_kr_t = jnp.swapaxes(cache_kr[0], 1, 2)
    return pl.pallas_call(
        _attn_sample_kernel,
        grid=(nb,),
        in_specs=[pl.BlockSpec((1, t, MLA_HEADS * HEAD_PAD), lambda b: (b, 0, 0)),
                  pl.BlockSpec((1, 1, past, KV_LORA), lambda b: (0, b, 0, 0)),
                  pl.BlockSpec((1, QK_ROPE, past), lambda b: (b, 0, 0)),
                  pl.BlockSpec((1, t, KV_LORA), lambda b: (b, 0, 0)),
                  pl.BlockSpec((1, t, QK_ROPE), lambda b: (b, 0, 0)),
                  _const_spec(wts["wuk3"].shape), _const_spec(wts["wuvp"].shape)],
        out_specs=pl.BlockSpec((1, t, MLA_WIDTH), lambda b: (b, 0, 0)),
        out_shape=jax.ShapeDtypeStruct((nb, t, MLA_WIDTH), BF16),
        compiler_params=pltpu.CompilerParams(dimension_semantics=("arbitrary",),
                                             vmem_limit_bytes=VMEM_LIMIT),
        name="attn_sample",
    )(q, cache_ckv, cache_kr_t, new_ckv, new_kr, wts["wuk3"], wts["wuvp"])


def _split3(x):
    hi = x.astype(BF16)
    r1 = x - hi.astype(F32)
    mid = r1.astype(BF16)
    lo = (r1 - mid.astype(F32)).astype(BF16)
    return hi, mid, lo


def _hgrn_blocks(load, store, lb_ref, gn_ref, st_sc, sh_sc, L, nb, fillers=()):
    fillers = list(fillers)

    def fill(count):
        for _ in range(min(count, len(fillers))):
            fillers.pop(0)()

    heads = range(HG_HEADS)
    rmod = lax.broadcasted_iota(jnp.int32, (L, 1), 0) % HG_SUB
    tril = (lax.broadcasted_iota(jnp.int32, (L, L), 0) >= lax.broadcasted_iota(jnp.int32, (L, L), 1)
            ).astype(BF16)
    nsub = L // HG_SUB
    hsl = lambda a, hd: a[:, hd * HG_EXPAND:(hd + 1) * HG_EXPAND]
    tile = lambda a, i: a[HG_SUB * i:HG_SUB * (i + 1)]
    zero_tile = jnp.zeros((HG_SUB, HG_EXPAND), F32)
    lb = lb_ref[...]
    gain = gn_ref[...]
    sh_sc[:, :, :, :HG_SUB, :] = jnp.zeros((nb, HG_HEADS, 3, HG_SUB, HG_EXPAND), F32)

    blocks = []
    for j in range(nb):
        hq, hf, v, hg = (load(part, j) for part in range(4))
        sg = _sigmoid(hf)
        g = jnp.log2(lb + (1.0 - lb) * sg)
        blocks.append(dict(v=v, hg=hg,
                           kk=(1.0 - lb) * (1.0 - sg),
                           qq=hq * _sigmoid(hq) * HG_SCALE,
                           g3=_split3(g)))
    for blk in blocks:
        g_hi, g_mid, g_lo = blk.pop("g3")
        blk["bc"] = _dot(tril, g_hi) + _dot(tril, g_mid) + _dot(tril, g_lo)
    fill(2)
    for blk in blocks:
        bc, qq, kk = blk["bc"], blk["qq"], blk["kk"]
        last = bc[L - 1:L, :]
        blk["qdec"] = (qq * jnp.exp2(bc)).astype(BF16)
        blk["kdec"] = (kk * jnp.exp2(last - bc)).astype(BF16)
        blk["dec"] = jnp.exp2(last)
        blk["vb"] = blk["v"].astype(BF16)
        if nsub > 1:
            blk["q2"], blk["k2"] = [], []
            for hd in heads:
                qh, kh, bh = hsl(qq, hd), hsl(kk, hd), hsl(bc, hd)
                q_rows = [jnp.concatenate([zero_tile] * (nsub - 1), axis=1)]
                k_cols = []
                for i in range(1, nsub):
                    b_i = bh[HG_SUB * i - 1:HG_SUB * i, :]
                    q_i = tile(qh, i) * jnp.exp2(tile(bh, i) - b_i)
                    q_rows.append(jnp.concatenate([zero_tile] * (i - 1) + [q_i] + [zero_tile] * (nsub - 1 - i),
                                                  axis=1))
                    k_i = kh[:HG_SUB * i] * jnp.exp2(b_i - bh[:HG_SUB * i])
                    k_cols.append(jnp.concatenate([k_i] + [zero_tile] * (nsub - i), axis=0))
                blk["q2"].append(jnp.concatenate(q_rows, axis=0).astype(BF16))
                blk["k2"].append(jnp.concatenate(k_cols, axis=1).astype(BF16))
    for blk in blocks:
        if nsub > 1:
            blk["att"] = [_nt(q2, k2).astype(BF16) for q2, k2 in zip(blk.pop("q2"), blk.pop("k2"))]
        blk["upd"] = [_tn(hsl(blk["vb"], hd), hsl(blk["kdec"], hd)) for hd in heads]
    fill(2)
    for j, blk in enumerate(blocks):
        blk["od"] = []
        for hd in heads:
            bc, qq, kk, v = (hsl(blk[name], hd) for name in ("bc", "qq", "kk", "v"))
            sh_sc[j, hd, 0, HG_SUB:, :] = kk
            sh_sc[j, hd, 1, HG_SUB:, :] = bc
            sh_sc[j, hd, 2, HG_SUB:, :] = v
            od = jnp.sum(qq * kk, axis=1, keepdims=True) * v
            for d in range(1, HG_SUB):
                shifted = pl.ds(HG_SUB - d, L)
                e = jnp.exp2(jnp.where(rmod >= d, bc - sh_sc[j, hd, 1, shifted, :], -jnp.inf))
                od = od + (jnp.sum(qq * sh_sc[j, hd, 0, shifted, :] * e, axis=1, keepdims=True)
                           * sh_sc[j, hd, 2, shifted, :])
            blk["od"].append(od)
    for blk in blocks:
        blk["st"] = []
    for hd in heads:
        st = st_sc[hd]
        for blk in blocks:
            blk["st"].append(st.astype(BF16))
            st = st * hsl(blk["dec"], hd) + blk["upd"][hd]
        st_sc[hd] = st
    for blk in blocks:
        o = [_nt(hsl(blk["qdec"], hd), blk["st"][hd]) for hd in heads]
        if nsub > 1:
            o = [o[hd] + _dot(blk["att"][hd], hsl(blk["vb"], hd)) for hd in heads]
        blk["o"] = o
    fill(len(fillers))
    for j, blk in enumerate(blocks):
        hg = blk["hg"]
        gate = hg * _sigmoid(hg)
        store(j, jnp.concatenate([_rms(blk["o"][hd] + blk["od"][hd], gain) * hsl(gate, hd) for hd in heads],
                                 axis=1))


def _hgrn_kernel(hh_ref, lb_ref, gn_ref, s0_ref, o_ref, sout_ref, st_sc, sh_sc, *, L, nb):
    tstep = pl.program_id(1)

    @pl.when(tstep == 0)
    def _():
        for hd in range(HG_HEADS):
            st_sc[hd] = s0_ref[0, hd].T

    def load(part, j):
        return hh_ref[0, j * L:(j + 1) * L, part * HG_WIDTH:(part + 1) * HG_WIDTH]

    def store(j, o):
        o_ref[0, j * L:(j + 1) * L, :] = o

    _hgrn_blocks(load, store, lb_ref, gn_ref, st_sc, sh_sc, L, nb)

    @pl.when(tstep == pl.num_programs(1) - 1)
    def _():
        for hd in range(HG_HEADS):
            sout_ref[0, hd] = st_sc[hd].T


def _hgrn(hh, lb, hg_norm, s0, L):
    bp, tp, _ = hh.shape
    tt = min(HG_TIME_TILE, tp)
    nb = tt // L
    return pl.pallas_call(
        functools.partial(_hgrn_kernel, L=L, nb=nb),
        grid=(bp, tp // tt),
        in_specs=[pl.BlockSpec((1, tt, 4 * HG_WIDTH), lambda b, t: (b, t, 0)),
                  _const_spec((1, HG_WIDTH)), _const_spec((1, HG_VDIM)),
                  pl.BlockSpec((1, HG_HEADS, HG_EXPAND, HG_VDIM), lambda b, t: (b, 0, 0, 0))],
        out_specs=[pl.BlockSpec((1, tt, HG_WIDTH), lambda b, t: (b, t, 0)),
                   pl.BlockSpec((1, HG_HEADS, HG_EXPAND, HG_VDIM), lambda b, t: (b, 0, 0, 0))],
        out_shape=[jax.ShapeDtypeStruct((bp, tp, HG_WIDTH), F32),
                   jax.ShapeDtypeStruct((bp, HG_HEADS, HG_EXPAND, HG_VDIM), F32)],
        scratch_shapes=[pltpu.VMEM((HG_HEADS, HG_VDIM, HG_EXPAND), F32),
                        pltpu.VMEM((nb, HG_HEADS, 3, HG_SUB + L, HG_EXPAND), F32)],
        compiler_params=pltpu.CompilerParams(dimension_semantics=("arbitrary", "arbitrary"),
                                             vmem_limit_bytes=VMEM_LIMIT),
        name="hgrn",
    )(hh, lb, hg_norm, s0)


def _post_kernel(x_ref, at_ref, ho_ref, gg_ref, g1_ref, sh2_ref, sc2_ref, g2_ref,
                 wpa_ref, wpb_ref, wout_ref, n2_ref, wgu_ref, wdown_ref, fn_ref, y_ref):
    x = x_ref[0]
    ya = _dot(at_ref[0], wpa_ref[...])
    yb = _dot(ho_ref[0].astype(BF16), wpb_ref[...])
    mixed = _sigmoid(gg_ref[0, :, :D_MODEL]) * ya + _sigmoid(gg_ref[0, :, D_MODEL:]) * yb
    x1 = x + g1_ref[0] * _dot(mixed.astype(BF16), wout_ref[...])
    hb = (_rms(x1, n2_ref[...]) * (1.0 + sc2_ref[0]) + sh2_ref[0]).astype(BF16)
    ff = jnp.zeros(x.shape, F32)
    for c in range(D_FF // FF_CHUNK):
        gate = _dot(hb, wgu_ref[:, c * FF_CHUNK:(c + 1) * FF_CHUNK])
        up = _dot(hb, wgu_ref[:, D_FF + c * FF_CHUNK:D_FF + (c + 1) * FF_CHUNK])
        act = (gate * _sigmoid(gate) * up).astype(BF16)
        ff = ff + _dot(act, wdown_ref[c * FF_CHUNK:(c + 1) * FF_CHUNK, :])
    x2 = x1 + g2_ref[0] * ff
    y_ref[0] = _rms(x2, fn_ref[...])


def _post(x, attn, ho, gg, g1, sh2, sc2, g2, wts):
    bp, tp, _ = x.shape
    r = min(ROW_TILE, tp)
    mod_rows = g1.shape[1]
    mod_blk = (1, 1, D_MODEL) if mod_rows == 1 else (1, r, D_MODEL)
    mod_map = (lambda b, t: (b, 0, 0)) if mod_rows == 1 else (lambda b, t: (b, t, 0))
    row3 = lambda w: pl.BlockSpec((1, r, w), lambda b, t: (b, t, 0))
    mod = pl.BlockSpec(mod_blk, mod_map)
    names = ["wpa", "wpb", "wout", "norm2", "wgu", "wdown", "final_norm"]
    return pl.pallas_call(
        _post_kernel,
        grid=(bp, tp // r),
        in_specs=[row3(D_MODEL), row3(MLA_WIDTH), row3(HG_WIDTH), row3(2 * D_MODEL), mod, mod, mod, mod]
                 + [_const_spec(wts[n].shape) for n in names],
        out_specs=row3(D_MODEL),
        out_shape=jax.ShapeDtypeStruct((bp, tp, D_MODEL), F32),
        compiler_params=pltpu.CompilerParams(dimension_semantics=("arbitrary", "arbitrary"),
                                             vmem_limit_bytes=VMEM_LIMIT),
        name="post",
    )(x, attn, ho, gg, g1, sh2, sc2, g2, *[wts[n] for n in names])


def _prep_weights(w_in, q_norm, w_uq, kv_norm, w_ukv, lb_param, hg_norm, w_pa, w_pb, w_out,
                  norm1, norm2, w_gu, w_down, final_norm):
    half = QK_ROPE // 2
    offs = np.cumsum([0, Q_LORA, KV_LORA, QK_ROPE, HG_WIDTH, HG_WIDTH, HG_WIDTH, HG_WIDTH, D_MODEL, D_MODEL])
    w = w_in[0]
    w_kr = w[:, offs[2]:offs[3]]
    zeros = lambda n: jnp.zeros((D_MODEL, n), w.dtype)
    kr_a = jnp.concatenate([zeros(ROPE_LANE0), w_kr, zeros(LANES - ROPE_LANE0 - QK_ROPE)], axis=1)
    kr_b = jnp.concatenate([zeros(ROPE_LANE0), w_kr[:, half:], w_kr[:, :half],
                            zeros(LANES - ROPE_LANE0 - QK_ROPE)], axis=1)
    w_in_p = jnp.concatenate([w[:, offs[0]:offs[2]], w[:, offs[3]:offs[9]], kr_a, kr_b], axis=1).astype(BF16)

    uq = w_uq[0].reshape(Q_LORA, MLA_HEADS, QK_NOPE + QK_ROPE)
    pad = jnp.zeros((Q_LORA, MLA_HEADS, HEAD_PAD - QK_NOPE - QK_ROPE), uq.dtype)
    wqa = jnp.concatenate([uq, pad], axis=2).reshape(Q_LORA, MLA_HEADS * HEAD_PAD).astype(BF16)
    wqb = jnp.concatenate([jnp.zeros((Q_LORA, MLA_HEADS, QK_NOPE), uq.dtype),
                           uq[:, :, QK_NOPE + half:], uq[:, :, QK_NOPE:QK_NOPE + half], pad],
                          axis=2).reshape(Q_LORA, MLA_HEADS * HEAD_PAD).astype(BF16)

    ukv = w_ukv[0].reshape(KV_LORA, MLA_HEADS, QK_NOPE + V_HEAD)
    uk = ukv[:, :, :QK_NOPE]
    uv = ukv[:, :, QK_NOPE:]
    wuk = jnp.concatenate([uk, jnp.zeros((KV_LORA, MLA_HEADS, HEAD_PAD - QK_NOPE), uk.dtype)],
                          axis=2).reshape(KV_LORA, MLA_HEADS * HEAD_PAD).astype(BF16)
    wuvt = uv.reshape(KV_LORA, MLA_WIDTH).T.astype(BF16)
    wuk3 = jnp.transpose(uk, (1, 0, 2)).astype(BF16)
    eye = jnp.eye(MLA_HEADS, dtype=uv.dtype)
    wuvp = (jnp.transpose(uv, (1, 0, 2))[:, :, None, :] * eye[:, None, :, None]
            ).reshape(MLA_HEADS, KV_LORA, MLA_WIDTH).astype(BF16)

    lb = jnp.cumsum(jax.nn.softmax(lb_param.astype(F32), axis=0), axis=0)[0].reshape(1, HG_WIDTH)
    return {
        "w_in": w_in_p, "wqa": wqa, "wqb": wqb, "wqa_t": wqa.T,
        "wuk": wuk, "wuvt": wuvt, "wuk3": wuk3, "wuvp": wuvp, "lb": lb,
        "q_norm": q_norm[0].reshape(1, Q_LORA), "kv_norm": kv_norm[0].reshape(1, KV_LORA),
        "hg_norm": hg_norm[0].reshape(1, HG_VDIM),
        "norm1": norm1[0].reshape(1, D_MODEL), "norm2": norm2[0].reshape(1, D_MODEL),
        "final_norm": final_norm.reshape(1, D_MODEL),
        "wpa": w_pa[0].astype(BF16), "wpb": w_pb[0].astype(BF16), "wout": w_out[0].astype(BF16),
        "wgu": w_gu[0].astype(BF16), "wdown": w_down[0].astype(BF16),
    }


def _rope_tables(pos):
    half = QK_ROPE // 2
    inv = ROPE_THETA ** (-np.arange(half, dtype=np.float64) / half)
    ang = np.asarray(pos, np.float64)[:, None] * inv[None, :]
    cos, sin = np.cos(ang), np.sin(ang)
    t = len(pos)
    ck = np.zeros((t, HEAD_PAD))
    sk = np.zeros((t, HEAD_PAD))
    ck[:, ROPE_LANE0:ROPE_LANE0 + QK_ROPE] = np.concatenate([cos, cos], axis=1)
    sk[:, ROPE_LANE0:ROPE_LANE0 + QK_ROPE] = np.concatenate([-sin, sin], axis=1)
    cq = ck.copy()
    cq[:, :QK_NOPE] = 1.0
    f = lambda a: jnp.asarray(a.astype(np.float32))
    return {"ck": f(ck), "sk": f(sk), "cq": f(cq), "sq": f(sk)}


def _trunk(prompt, x, mod, wts, tabs, s0=None, cache=None):
    sh1, sc1, g1, sh2, sc2, g2 = mod
    if prompt:
        w = dict(wts, wqa=wts["wqa_t"])
        t = dict(tabs, cq=tabs["cq"].T, sq=tabs["sq"].T)
        qt, k, vt, ckv, kr_t, ho, s_new, gg = _pre(True, x, sh1, sc1, w, t)
        kr = jnp.swapaxes(kr_t, 1, 2)
        attn = _attn_prompt(qt, k, vt)
    else:
        q, ckv, kr, hh, gg = _pre(False, x, sh1, sc1, wts, tabs)
        nb, t_new = cache[0].shape[1], x.shape[1] // cache[0].shape[1]
        attn = _attn_sample(q.reshape(nb, t_new, -1), cache[0], cache[1],
                            ckv.reshape(nb, t_new, KV_LORA), kr.reshape(nb, t_new, QK_ROPE), wts)
        attn = attn.reshape(1, nb * t_new, MLA_WIDTH)
        ho, s_new = _hgrn(hh.reshape(nb, t_new, 4 * HG_WIDTH), wts["lb"], wts["hg_norm"], s0, t_new)
        ho = ho.reshape(x.shape[0], x.shape[1], HG_WIDTH)
    y = _post(x, attn, ho, gg, g1, sh2, sc2, g2, wts)
    return y, ckv, kr, s_new


def kernel(x_prompt, x_sample, cache_ckv, cache_krope, state_hgrn, c_prompt, c_sample, w_in, q_norm, w_uq,
           kv_norm, w_ukv, lb_param, hg_norm, w_pa, w_pb, w_out, norm1, norm2, w_ada, b_ada, w_gu, w_down,
           final_norm):
    bsz, t_p, _ = x_prompt.shape
    nb, t_s, _ = x_sample.shape
    past = cache_ckv.shape[2]
    assert past % CHUNK == 0 and t_s <= CHUNK
    wts = _prep_weights(w_in, q_norm, w_uq, kv_norm, w_ukv, lb_param, hg_norm, w_pa, w_pb, w_out,
                        norm1, norm2, w_gu, w_down, final_norm)

    n_c = bsz + nb
    n_pad = -n_c % 8
    c_all = jnp.concatenate([c_prompt, c_sample, jnp.zeros((n_pad, D_MODEL), F32)], axis=0)
    mod_all = _ada(c_all, w_ada[0], b_ada[0])
    mod_p = [m[:, None, :] for m in jnp.split(mod_all[:bsz], 6, axis=-1)]
    mod_s = [jnp.repeat(m, t_s, axis=0)[None] for m in jnp.split(mod_all[bsz:n_c], 6, axis=-1)]

    tabs_p = _rope_tables(np.arange(t_p))
    tabs_s = _rope_tables(np.tile(past + np.arange(t_s), nb))

    y_p, ckv_p, kr_p, s_p = _trunk(True, x_prompt, mod_p, wts, tabs_p)
    y_s, ckv_s, kr_s, s_s = _trunk(False, x_sample.reshape(1, nb * t_s, D_MODEL), mod_s, wts, tabs_s,
                                   s0=state_hgrn[0], cache=(cache_ckv, cache_krope))
    return (y_p, y_s.reshape(nb, t_s, D_MODEL), ckv_p[None], kr_p[None], s_p[None],
            ckv_s.reshape(nb, t_s, KV_LORA)[None], kr_s.reshape(nb, t_s, QK_ROPE)[None], s_s[None])
```

```python
import functools
import math

import numpy as np
import jax
import jax.numpy as jnp
from jax import lax
from jax.experimental import pallas as pl
from jax.experimental.pallas import tpu as pltpu

F32 = jnp.float32
BF16 = jnp.bfloat16

D_MODEL = 1024
CHUNK = 64
EPS = 1e-6
MLA_HEADS = 8
QK_NOPE = 64
QK_ROPE = 32
V_HEAD = 64
Q_LORA = 384
KV_LORA = 256
ROPE_THETA = 10000.0
MLA_WIDTH = MLA_HEADS * V_HEAD
MLA_SCALE = 1.0 / math.sqrt(QK_NOPE + QK_ROPE)
HG_HEADS = 4
HG_EXPAND = 128
HG_VDIM = 128
HG_WIDTH = HG_HEADS * HG_EXPAND
HG_SCALE = 1.0 / math.sqrt(HG_EXPAND)
D_FF = 2816
LOG2E = math.log2(math.e)

LANES = 128
F32_SUBLANES = 8
BF16_SUBLANES = 16
HEAD_PAD = LANES
ROPE_LANE0 = QK_NOPE
VT_ROWS = V_HEAD + BF16_SUBLANES
VMEM_LIMIT = 56 * 1024 * 1024

ROW_TILE = 512
PRE_ROW_TILE = 256
ATT_BLOCK = 1024
ATT_KV_BLOCK = 2048
ATT_SUB = 256
ATT_AHEAD = 1
HG_SUB = F32_SUBLANES
HG_TIME_TILE = 512
FF_CHUNK = 1408

_C_QLAT = 0
_C_KVLAT = _C_QLAT + Q_LORA
_C_HG = _C_KVLAT + KV_LORA
_C_GATE = _C_HG + 4 * HG_WIDTH
_C_KRA = _C_GATE + 2 * D_MODEL
_C_KRB = _C_KRA + LANES
_C_END = _C_KRB + LANES


def _const_spec(shape):
    nd = len(shape)
    return pl.BlockSpec(shape, lambda *_: (0,) * nd, pipeline_mode=pl.Buffered(1))


def _nt(a, b):
    return lax.dot_general(a, b, (((1,), (1,)), ((), ())), preferred_element_type=F32)


def _tn(a, b):
    return lax.dot_general(a, b, (((0,), (0,)), ((), ())), preferred_element_type=F32)


def _dot(a, b):
    return jnp.dot(a, b, preferred_element_type=F32)


def _rms(x, w):
    return x * lax.rsqrt(jnp.mean(x * x, axis=-1, keepdims=True) + EPS) * w


def _sigmoid(x):
    return 1.0 / (1.0 + jnp.exp(-x))


def _ada_kernel(c_ref, w_ref, b_ref, o_ref):
    c = c_ref[...]
    a = (c * _sigmoid(c)).astype(BF16)
    o_ref[...] = _dot(a, w_ref[...].astype(BF16)) + b_ref[...]


def _ada(c, w_ada, b_ada):
    n = c.shape[0]
    cols = w_ada.shape[1]
    blk = D_MODEL
    return pl.pallas_call(
        _ada_kernel,
        grid=(cols // blk,),
        in_specs=[pl.BlockSpec((n, D_MODEL), lambda j: (0, 0)),
                  pl.BlockSpec((D_MODEL, blk), lambda j: (0, j)),
                  pl.BlockSpec((1, blk), lambda j: (0, j))],
        out_specs=pl.BlockSpec((n, blk), lambda j: (0, j)),
        out_shape=jax.ShapeDtypeStruct((n, cols), F32),
        compiler_params=pltpu.CompilerParams(dimension_semantics=("arbitrary",)),
        name="ada",
    )(c, w_ada, b_ada.reshape(1, cols))


def _pre_kernel(prompt, x_ref, sh_ref, sc_ref, n1_ref, win_ref, qn_ref, wqa_ref, kvn_ref,
                cq_ref, sq_ref, ck_ref, sk_ref, *rest):
    if prompt:
        (wuk_ref, wuvt_ref, lb_ref, gn_ref, qt_ref, k_ref, vt_ref, ckv_ref, kr_ref, ho_ref, sout_ref, gg_ref,
         hh_ref, st_sc, sh_sc) = rest
        tstep = pl.program_id(1)

        @pl.when(tstep == 0)
        def _():
            st_sc[...] = jnp.zeros(st_sc.shape, F32)
    else:
        wqb_ref, q_ref, ckv_ref, kr_ref, hh_ref, gg_ref = rest
    x = x_ref[0]
    h = _rms(x, n1_ref[...]) * (1.0 + sc_ref[0]) + sh_ref[0]
    hb = h.astype(BF16)

    def proj(a, b):
        return _dot(hb, win_ref[:, a:b])

    def latents():
        kr_full = proj(_C_KRA, _C_KRB) * ck_ref[...] + proj(_C_KRB, _C_END) * sk_ref[...]
        if prompt:
            kr_ref[0] = kr_full.T[ROPE_LANE0:ROPE_LANE0 + QK_ROPE]
        else:
            kr_ref[0] = kr_full[:, ROPE_LANE0:ROPE_LANE0 + QK_ROPE]
        c_kv = _rms(proj(_C_KVLAT, _C_HG), kvn_ref[...])
        ckv_ref[0] = c_kv
        return kr_full, c_kv, _rms(proj(_C_QLAT, _C_KVLAT), qn_ref[...]).astype(BF16)

    if prompt:
        lat = {}

        def gate_cols(lo, hi):
            gg_ref[0, :, lo:hi] = proj(_C_GATE + lo, _C_GATE + hi)

        def do_latents():
            lat["kr"], c_kv, lat["cq"] = latents()
            lat["ckv"] = c_kv.astype(BF16)

        def do_q():
            qa = _nt(wqa_ref[...], lat["cq"])
            cq = cq_ref[...]
            sq = sq_ref[...]
            half = QK_ROPE // 2
            r0, r1, r2 = ROPE_LANE0, ROPE_LANE0 + half, ROPE_LANE0 + QK_ROPE
            for hd in range(MLA_HEADS):
                qh = qa[hd * HEAD_PAD:(hd + 1) * HEAD_PAD]
                qsw = jnp.concatenate([qh[:r0], qh[r1:r2], qh[r0:r1], qh[r2:]], axis=0)
                qt_ref[0, hd] = ((qh * cq + qsw * sq) * (MLA_SCALE * LOG2E)).astype(BF16)

        def do_kv():
            kall = _dot(lat["ckv"], wuk_ref[...])
            for hd in range(MLA_HEADS):
                sl = slice(hd * HEAD_PAD, (hd + 1) * HEAD_PAD)
                k_ref[0, hd] = (kall[:, sl] + lat["kr"]).astype(BF16)
            vt = _nt(wuvt_ref[...], lat["ckv"])
            ones_rows = (lax.broadcasted_iota(jnp.int32, (VT_ROWS - V_HEAD, vt.shape[1]), 0) == 0).astype(BF16)
            for hd in range(MLA_HEADS):
                vt_ref[0, hd, :V_HEAD] = vt[hd * V_HEAD:(hd + 1) * V_HEAD].astype(BF16)
                vt_ref[0, hd, V_HEAD:] = ones_rows

        gstep = D_MODEL // 2
        mla_work = [functools.partial(gate_cols, 0, gstep), functools.partial(gate_cols, gstep, 2 * gstep),
                    do_latents, functools.partial(gate_cols, 2 * gstep, 3 * gstep), do_q,
                    functools.partial(gate_cols, 3 * gstep, 4 * gstep), do_kv]
        hh_ref[...] = proj(_C_HG, _C_GATE)

        def load(part, j):
            return hh_ref[j * CHUNK:(j + 1) * CHUNK, part * HG_WIDTH:(part + 1) * HG_WIDTH]

        def store(j, o):
            ho_ref[0, j * CHUNK:(j + 1) * CHUNK, :] = o

        _hgrn_blocks(load, store, lb_ref, gn_ref, st_sc, sh_sc, CHUNK, hh_ref.shape[0] // CHUNK, mla_work)

        @pl.when(tstep == pl.num_programs(1) - 1)
        def _():
            for hd in range(HG_HEADS):
                sout_ref[0, hd] = st_sc[hd].T
    else:
        hh_ref[0] = proj(_C_HG, _C_GATE)
        gg_ref[0] = proj(_C_GATE, _C_KRA)
        _, _, cqb = latents()
        qa = _dot(cqb, wqa_ref[...])
        qb = _dot(cqb, wqb_ref[...])
        cq = cq_ref[...]
        sq = sq_ref[...]
        for hd in range(MLA_HEADS):
            sl = slice(hd * HEAD_PAD, (hd + 1) * HEAD_PAD)
            q_ref[0, :, sl] = ((qa[:, sl] * cq + qb[:, sl] * sq) * MLA_SCALE).astype(BF16)


def _pre(prompt, x, sh1, sc1, wts, tabs):
    bp, tp, _ = x.shape
    r = min(PRE_ROW_TILE if prompt else ROW_TILE, tp)
    nt = tp // r
    mod_rows = sh1.shape[1]
    mod_blk = (1, 1, D_MODEL) if mod_rows == 1 else (1, r, D_MODEL)
    mod_map = (lambda b, t: (b, 0, 0)) if mod_rows == 1 else (lambda b, t: (b, t, 0))
    row3 = lambda w: pl.BlockSpec((1, r, w), lambda b, t: (b, t, 0))
    in_specs = [row3(D_MODEL), pl.BlockSpec(mod_blk, mod_map), pl.BlockSpec(mod_blk, mod_map),
                _const_spec((1, D_MODEL)), _const_spec(wts["w_in"].shape), _const_spec((1, Q_LORA)),
                _const_spec(wts["wqa"].shape), _const_spec((1, KV_LORA))]
    if prompt:
        in_specs += [pl.BlockSpec((HEAD_PAD, r), lambda b, t: (0, t))] * 2
    else:
        in_specs += [pl.BlockSpec((r, HEAD_PAD), lambda b, t: (t, 0))] * 2
    in_specs += [pl.BlockSpec((r, HEAD_PAD), lambda b, t: (t, 0))] * 2
    args = [x, sh1, sc1, wts["norm1"], wts["w_in"], wts["q_norm"], wts["wqa"], wts["kv_norm"],
            tabs["cq"], tabs["sq"], tabs["ck"], tabs["sk"]]
    f32_rows = lambda w: jax.ShapeDtypeStruct((bp, tp, w), F32)
    state_shape = (bp, HG_HEADS, HG_EXPAND, HG_VDIM)
    scratch = []
    if prompt:
        in_specs += [_const_spec(wts["wuk"].shape), _const_spec(wts["wuvt"].shape),
                     _const_spec((1, HG_WIDTH)), _const_spec((1, HG_VDIM))]
        args += [wts["wuk"], wts["wuvt"], wts["lb"], wts["hg_norm"]]
        out_shape = [jax.ShapeDtypeStruct((bp, MLA_HEADS, HEAD_PAD, tp), BF16),
                     jax.ShapeDtypeStruct((bp, MLA_HEADS, tp, HEAD_PAD), BF16),
                     jax.ShapeDtypeStruct((bp, MLA_HEADS, VT_ROWS, tp), BF16),
                     f32_rows(KV_LORA), jax.ShapeDtypeStruct((bp, QK_ROPE, tp), F32), f32_rows(HG_WIDTH),
                     jax.ShapeDtypeStruct(state_shape, F32), f32_rows(2 * D_MODEL)]
        out_specs = [pl.BlockSpec((1, MLA_HEADS, HEAD_PAD, r), lambda b, t: (b, 0, 0, t)),
                     pl.BlockSpec((1, MLA_HEADS, r, HEAD_PAD), lambda b, t: (b, 0, t, 0)),
                     pl.BlockSpec((1, MLA_HEADS, VT_ROWS, r), lambda b, t: (b, 0, 0, t)),
                     row3(KV_LORA), pl.BlockSpec((1, QK_ROPE, r), lambda b, t: (b, 0, t)), row3(HG_WIDTH),
                     pl.BlockSpec((1,) + state_shape[1:], lambda b, t: (b, 0, 0, 0)), row3(2 * D_MODEL)]
        scratch = [pltpu.VMEM((r, 4 * HG_WIDTH), F32),
                   pltpu.VMEM((HG_HEADS, HG_VDIM, HG_EXPAND), F32),
                   pltpu.VMEM((r // CHUNK, HG_HEADS, 3, HG_SUB + CHUNK, HG_EXPAND), F32)]
    else:
        in_specs += [_const_spec(wts["wqb"].shape)]
        args += [wts["wqb"]]
        out_shape = [jax.ShapeDtypeStruct((bp, tp, MLA_HEADS * HEAD_PAD), BF16),
                     f32_rows(KV_LORA), f32_rows(QK_ROPE), f32_rows(4 * HG_WIDTH), f32_rows(2 * D_MODEL)]
        out_specs = [row3(MLA_HEADS * HEAD_PAD), row3(KV_LORA), row3(QK_ROPE), row3(4 * HG_WIDTH),
                     row3(2 * D_MODEL)]
    return pl.pallas_call(
        functools.partial(_pre_kernel, prompt),
        grid=(bp, nt),
        in_specs=in_specs,
        out_specs=out_specs,
        out_shape=out_shape,
        scratch_shapes=scratch,
        compiler_params=pltpu.CompilerParams(dimension_semantics=("arbitrary", "arbitrary"),
                                             vmem_limit_bytes=VMEM_LIMIT),
        name="pre_prompt" if prompt else "pre_sample",
    )(*args)


def _attn_kernel(qi_ref, kj_ref, kind_ref, qt_ref, k_ref, vt_ref, o_ref, m_sc, acc_sc, *slabs, blk, kblk, sub):
    p = pl.program_id(1)
    kj = kj_ref[p]
    kind = kind_ref[p]
    ns = blk // sub
    nslab = len(slabs) // 2
    s_bufs, st_bufs = slabs[:nslab], slabs[nslab:]

    @pl.when(kj == 0)
    def _():
        m_sc[...] = jnp.full(m_sc.shape, -jnp.inf, F32)
        acc_sc[...] = jnp.zeros(acc_sc.shape, F32)

    def run(part):
        diag = part is not None
        if diag:
            kc = lax.broadcasted_iota(jnp.int32, (sub, sub), 0) // CHUNK
            qc = lax.broadcasted_iota(jnp.int32, (sub, sub), 1) // CHUNK
            visible = kc <= qc

        def rows(qs):
            return part * blk + (qs + 1) * sub if diag else kblk

        def scores(hd, qs):
            ql = slice(qs * sub, (qs + 1) * sub)
            nrow = rows(qs)
            s_buf, st_buf = s_bufs[qs % nslab], st_bufs[qs % nslab]
            q_t = qt_ref[0, hd, :, ql]
            half = max(sub, (nrow // 2) // sub * sub)
            mx = None
            for lo, hi in ((0, half), (half, nrow)):
                if lo == hi:
                    continue
                s = _dot(k_ref[0, hd, lo:hi, :], q_t)
                if diag and hi == nrow:
                    top = nrow - sub - lo
                    dg = jnp.where(visible, s[top:], -jnp.inf)
                    s_buf[nrow - sub:nrow, :] = dg
                    part = jnp.max(dg, axis=0, keepdims=True)
                    if top > 0:
                        s_buf[lo:nrow - sub, :] = s[:top]
                        part = jnp.maximum(part, jnp.max(s[:top], axis=0, keepdims=True))
                else:
                    s_buf[lo:hi, :] = s
                    part = jnp.max(s, axis=0, keepdims=True)
                mx = part if mx is None else jnp.maximum(mx, part)
            m_old = m_sc[hd, :, ql]
            m_new = jnp.maximum(m_old, mx)
            m_sc[hd, :, ql] = m_new
            st_buf[0] = m_new
            st_buf[1] = jnp.exp2(m_old - m_new)

        def weighted(hd, qs):
            ql = slice(qs * sub, (qs + 1) * sub)
            nrow = rows(qs)
            s_buf, st_buf = s_bufs[qs % nslab], st_bufs[qs % nslab]
            pm = jnp.exp2(s_buf[:nrow, :] - st_buf[0])
            acc_sc[hd, :, ql] = st_buf[1] * acc_sc[hd, :, ql] + _dot(vt_ref[0, hd, :, :nrow], pm.astype(BF16))

        units = [(hd, qs) for hd in range(MLA_HEADS) for qs in range(ns)]
        for u in units[:ATT_AHEAD]:
            scores(*u)
        for i, u in enumerate(units):
            if i + ATT_AHEAD < len(units):
                scores(*units[i + ATT_AHEAD])
            weighted(*u)

    @pl.when(kind == 0)
    def _():
        run(None)

    for part in range(kblk // blk):
        @pl.when(kind == 1 + part)
        def _(part=part):
            run(part)
            o = acc_sc[:, :V_HEAD, :] / acc_sc[:, V_HEAD:V_HEAD + 1, :]
            o_ref[0] = o.reshape(MLA_WIDTH, blk).T.astype(BF16)


def _attn_prompt(qt, k, vt):
    bp, _, _, tp = qt.shape
    blk = min(ATT_BLOCK, tp)
    kblk = min(ATT_KV_BLOCK, tp)
    sub = min(ATT_SUB, blk)
    ns = blk // sub
    ratio = kblk // blk
    pairs = [(q, j) for q in range(tp // blk) for j in range(q // ratio + 1)]
    qi_of = np.array([q for q, _ in pairs], np.int32)
    kj_of = np.array([j for _, j in pairs], np.int32)
    kind_of = np.array([0 if j < q // ratio else 1 + q % ratio for q, j in pairs], np.int32)
    grid_spec = pltpu.PrefetchScalarGridSpec(
        num_scalar_prefetch=3,
        grid=(bp, len(pairs)),
        in_specs=[pl.BlockSpec((1, MLA_HEADS, HEAD_PAD, blk), lambda b, p, qi, kj, kd: (b, 0, 0, qi[p])),
                  pl.BlockSpec((1, MLA_HEADS, kblk, HEAD_PAD), lambda b, p, qi, kj, kd: (b, 0, kj[p], 0)),
                  pl.BlockSpec((1, MLA_HEADS, VT_ROWS, kblk), lambda b, p, qi, kj, kd: (b, 0, 0, kj[p]))],
        out_specs=pl.BlockSpec((1, blk, MLA_WIDTH), lambda b, p, qi, kj, kd: (b, qi[p], 0)),
        scratch_shapes=[pltpu.VMEM((MLA_HEADS, 1, blk), F32),
                        pltpu.VMEM((MLA_HEADS, VT_ROWS, blk), F32)]
                       + [pltpu.VMEM((kblk, sub), F32)] * (ATT_AHEAD + 1)
                       + [pltpu.VMEM((2, 1, sub), F32)] * (ATT_AHEAD + 1),
    )
    return pl.pallas_call(
        functools.partial(_attn_kernel, blk=blk, kblk=kblk, sub=sub),
        grid_spec=grid_spec,
        out_shape=jax.ShapeDtypeStruct((bp, tp, MLA_WIDTH), BF16),
        compiler_params=pltpu.CompilerParams(dimension_semantics=("arbitrary", "arbitrary"),
                                             vmem_limit_bytes=VMEM_LIMIT),
        name="attn_prompt",
    )(jnp.asarray(qi_of), jnp.asarray(kj_of), jnp.asarray(kind_of), qt, k, vt)


def _attn_sample_kernel(q_ref, cc_ref, ckrt_ref, nc_ref, nkr_ref, wuk_ref, wuvp_ref, o_ref):
    t = q_ref.shape[1]
    q = q_ref[0]
    qh = [q[:, hd * HEAD_PAD:(hd + 1) * HEAD_PAD] for hd in range(MLA_HEADS)]
    q_rope = jnp.concatenate([h[:, ROPE_LANE0:ROPE_LANE0 + QK_ROPE] for h in qh], axis=0)
    q_abs = jnp.concatenate(
        [_nt(qh[hd][:, :QK_NOPE], wuk_ref[hd]) for hd in range(MLA_HEADS)], axis=0).astype(BF16)
    cc = cc_ref[0, 0].astype(BF16)
    nc = nc_ref[0].astype(BF16)
    s_c = _nt(q_abs, cc) + _dot(q_rope, ckrt_ref[0].astype(BF16))
    s_n = _nt(q_abs, nc) + _nt(q_rope, nkr_ref[0].astype(BF16))
    m = jnp.maximum(jnp.max(s_c, axis=-1, keepdims=True), jnp.max(s_n, axis=-1, keepdims=True))
    p_c = jnp.exp(s_c - m)
    p_n = jnp.exp(s_n - m)
    l = jnp.sum(p_c, axis=-1, keepdims=True) + jnp.sum(p_n, axis=-1, keepdims=True)
    o_lat = (_dot(p_c.astype(BF16), cc) + _dot(p_n.astype(BF16), nc)) / l
    o_lat = o_lat.astype(BF16)
    out = jnp.zeros((t, MLA_WIDTH), F32)
    for hd in range(MLA_HEADS):
        out = out + _dot(o_lat[hd * t:(hd + 1) * t], wuvp_ref[hd])
    o_ref[0] = out.astype(BF16)


def _attn_sample(q, cache_ckv, cache_kr, new_ckv, new_kr, wts):
    nb, t, _ = q.shape
    past = cache_ckv.shape[2]
    cache_kr_t = jnp.swapaxes(cache_kr[0], 1, 2)
    return pl.pallas_call(
        _attn_sample_kernel,
        grid=(nb,),
        in_specs=[pl.BlockSpec((1, t, MLA_HEADS * HEAD_PAD), lambda b: (b, 0, 0)),
                  pl.BlockSpec((1, 1, past, KV_LORA), lambda b: (0, b, 0, 0)),
                  pl.BlockSpec((1, QK_ROPE, past), lambda b: (b, 0, 0)),
                  pl.BlockSpec((1, t, KV_LORA), lambda b: (b, 0, 0)),
                  pl.BlockSpec((1, t, QK_ROPE), lambda b: (b, 0, 0)),
                  _const_spec(wts["wuk3"].shape), _const_spec(wts["wuvp"].shape)],
        out_specs=pl.BlockSpec((1, t, MLA_WIDTH), lambda b: (b, 0, 0)),
        out_shape=jax.ShapeDtypeStruct((nb, t, MLA_WIDTH), BF16),
        compiler_params=pltpu.CompilerParams(dimension_semantics=("arbitrary",),
                                             vmem_limit_bytes=VMEM_LIMIT),
        name="attn_sample",
    )(q, cache_ckv, cache_kr_t, new_ckv, new_kr, wts["wuk3"], wts["wuvp"])


def _split3(x):
    hi = x.astype(BF16)
    r1 = x - hi.astype(F32)
    mid = r1.astype(BF16)
    lo = (r1 - mid.astype(F32)).astype(BF16)
    return hi, mid, lo


def _hgrn_blocks(load, store, lb_ref, gn_ref, st_sc, sh_sc, L, nb, fillers=()):
    fillers = list(fillers)

    def fill(count):
        for _ in range(min(count, len(fillers))):
            fillers.pop(0)()

    heads = range(HG_HEADS)
    rmod = lax.broadcasted_iota(jnp.int32, (L, 1), 0) % HG_SUB
    tril = (lax.broadcasted_iota(jnp.int32, (L, L), 0) >= lax.broadcasted_iota(jnp.int32, (L, L), 1)
            ).astype(BF16)
    nsub = L // HG_SUB
    hsl = lambda a, hd: a[:, hd * HG_EXPAND:(hd + 1) * HG_EXPAND]
    tile = lambda a, i: a[HG_SUB * i:HG_SUB * (i + 1)]
    zero_tile = jnp.zeros((HG_SUB, HG_EXPAND), F32)
    lb = lb_ref[...]
    gain = gn_ref[...]
    sh_sc[:, :, :, :HG_SUB, :] = jnp.zeros((nb, HG_HEADS, 3, HG_SUB, HG_EXPAND), F32)

    blocks = []
    for j in range(nb):
        hq, hf, v, hg = (load(part, j) for part in range(4))
        sg = _sigmoid(hf)
        g = jnp.log2(lb + (1.0 - lb) * sg)
        blocks.append(dict(v=v, hg=hg,
                           kk=(1.0 - lb) * (1.0 - sg),
                           qq=hq * _sigmoid(hq) * HG_SCALE,
                           g3=_split3(g)))
    for blk in blocks:
        g_hi, g_mid, g_lo = blk.pop("g3")
        blk["bc"] = _dot(tril, g_hi) + _dot(tril, g_mid) + _dot(tril, g_lo)
    fill(2)
    for blk in blocks:
        bc, qq, kk = blk["bc"], blk["qq"], blk["kk"]
        last = bc[L - 1:L, :]
        blk["qdec"] = (qq * jnp.exp2(bc)).astype(BF16)
        blk["kdec"] = (kk * jnp.exp2(last - bc)).astype(BF16)
        blk["dec"] = jnp.exp2(last)
        blk["vb"] = blk["v"].astype(BF16)
        if nsub > 1:
            blk["q2"], blk["k2"] = [], []
            for hd in heads:
                qh, kh, bh = hsl(qq, hd), hsl(kk, hd), hsl(bc, hd)
                q_rows = [jnp.concatenate([zero_tile] * (nsub - 1), axis=1)]
                k_cols = []
                for i in range(1, nsub):
                    b_i = bh[HG_SUB * i - 1:HG_SUB * i, :]
                    q_i = tile(qh, i) * jnp.exp2(tile(bh, i) - b_i)
                    q_rows.append(jnp.concatenate([zero_tile] * (i - 1) + [q_i] + [zero_tile] * (nsub - 1 - i),
                                                  axis=1))
                    k_i = kh[:HG_SUB * i] * jnp.exp2(b_i - bh[:HG_SUB * i])
                    k_cols.append(jnp.concatenate([k_i] + [zero_tile] * (nsub - i), axis=0))
                blk["q2"].append(jnp.concatenate(q_rows, axis=0).astype(BF16))
                blk["k2"].append(jnp.concatenate(k_cols, axis=1).astype(BF16))
    for blk in blocks:
        if nsub > 1:
            blk["att"] = [_nt(q2, k2).astype(BF16) for q2, k2 in zip(blk.pop("q2"), blk.pop("k2"))]
        blk["upd"] = [_tn(hsl(blk["vb"], hd), hsl(blk["kdec"], hd)) for hd in heads]
    fill(2)
    for j, blk in enumerate(blocks):
        blk["od"] = []
        for hd in heads:
            bc, qq, kk, v = (hsl(blk[name], hd) for name in ("bc", "qq", "kk", "v"))
            sh_sc[j, hd, 0, HG_SUB:, :] = kk
            sh_sc[j, hd, 1, HG_SUB:, :] = bc
            sh_sc[j, hd, 2, HG_SUB:, :] = v
            od = jnp.sum(qq * kk, axis=1, keepdims=True) * v
            for d in range(1, HG_SUB):
                shifted = pl.ds(HG_SUB - d, L)
                e = jnp.exp2(jnp.where(rmod >= d, bc - sh_sc[j, hd, 1, shifted, :], -jnp.inf))
                od = od + (jnp.sum(qq * sh_sc[j, hd, 0, shifted, :] * e, axis=1, keepdims=True)
                           * sh_sc[j, hd, 2, shifted, :])
            blk["od"].append(od)
    for blk in blocks:
        blk["st"] = []
    for hd in heads:
        st = st_sc[hd]
        for blk in blocks:
            blk["st"].append(st.astype(BF16))
            st = st * hsl(blk["dec"], hd) + blk["upd"][hd]
        st_sc[hd] = st
    for blk in blocks:
        o = [_nt(hsl(blk["qdec"], hd), blk["st"][hd]) for hd in heads]
        if nsub > 1:
            o = [o[hd] + _dot(blk["att"][hd], hsl(blk["vb"], hd)) for hd in heads]
        blk["o"] = o
    fill(len(fillers))
    for j, blk in enumerate(blocks):
        hg = blk["hg"]
        gate = hg * _sigmoid(hg)
        store(j, jnp.concatenate([_rms(blk["o"][hd] + blk["od"][hd], gain) * hsl(gate, hd) for hd in heads],
                                 axis=1))


def _hgrn_kernel(hh_ref, lb_ref, gn_ref, s0_ref, o_ref, sout_ref, st_sc, sh_sc, *, L, nb):
    tstep = pl.program_id(1)

    @pl.when(tstep == 0)
    def _():
        for hd in range(HG_HEADS):
            st_sc[hd] = s0_ref[0, hd].T

    def load(part, j):
        return hh_ref[0, j * L:(j + 1) * L, part * HG_WIDTH:(part + 1) * HG_WIDTH]

    def store(j, o):
        o_ref[0, j * L:(j + 1) * L, :] = o

    _hgrn_blocks(load, store, lb_ref, gn_ref, st_sc, sh_sc, L, nb)

    @pl.when(tstep == pl.num_programs(1) - 1)
    def _():
        for hd in range(HG_HEADS):
            sout_ref[0, hd] = st_sc[hd].T


def _hgrn(hh, lb, hg_norm, s0, L):
    bp, tp, _ = hh.shape
    tt = min(HG_TIME_TILE, tp)
    nb = tt // L
    return pl.pallas_call(
        functools.partial(_hgrn_kernel, L=L, nb=nb),
        grid=(bp, tp // tt),
        in_specs=[pl.BlockSpec((1, tt, 4 * HG_WIDTH), lambda b, t: (b, t, 0)),
                  _const_spec((1, HG_WIDTH)), _const_spec((1, HG_VDIM)),
                  pl.BlockSpec((1, HG_HEADS, HG_EXPAND, HG_VDIM), lambda b, t: (b, 0, 0, 0))],
        out_specs=[pl.BlockSpec((1, tt, HG_WIDTH), lambda b, t: (b, t, 0)),
                   pl.BlockSpec((1, HG_HEADS, HG_EXPAND, HG_VDIM), lambda b, t: (b, 0, 0, 0))],
        out_shape=[jax.ShapeDtypeStruct((bp, tp, HG_WIDTH), F32),
                   jax.ShapeDtypeStruct((bp, HG_HEADS, HG_EXPAND, HG_VDIM), F32)],
        scratch_shapes=[pltpu.VMEM((HG_HEADS, HG_VDIM, HG_EXPAND), F32),
                        pltpu.VMEM((nb, HG_HEADS, 3, HG_SUB + L, HG_EXPAND), F32)],
        compiler_params=pltpu.CompilerParams(dimension_semantics=("arbitrary", "arbitrary"),
                                             vmem_limit_bytes=VMEM_LIMIT),
        name="hgrn",
    )(hh, lb, hg_norm, s0)


def _post_kernel(x_ref, at_ref, ho_ref, gg_ref, g1_ref, sh2_ref, sc2_ref, g2_ref,
                 wpa_ref, wpb_ref, wout_ref, n2_ref, wgu_ref, wdown_ref, fn_ref, y_ref):
    x = x_ref[0]
    ya = _dot(at_ref[0], wpa_ref[...])
    yb = _dot(ho_ref[0].astype(BF16), wpb_ref[...])
    mixed = _sigmoid(gg_ref[0, :, :D_MODEL]) * ya + _sigmoid(gg_ref[0, :, D_MODEL:]) * yb
    x1 = x + g1_ref[0] * _dot(mixed.astype(BF16), wout_ref[...])
    hb = (_rms(x1, n2_ref[...]) * (1.0 + sc2_ref[0]) + sh2_ref[0]).astype(BF16)
    ff = jnp.zeros(x.shape, F32)
    for c in range(D_FF // FF_CHUNK):
        gate = _dot(hb, wgu_ref[:, c * FF_CHUNK:(c + 1) * FF_CHUNK])
        up = _dot(hb, wgu_ref[:, D_FF + c * FF_CHUNK:D_FF + (c + 1) * FF_CHUNK])
        act = (gate * _sigmoid(gate) * up).astype(BF16)
        ff = ff + _dot(act, wdown_ref[c * FF_CHUNK:(c + 1) * FF_CHUNK, :])
    x2 = x1 + g2_ref[0] * ff
    y_ref[0] = _rms(x2, fn_ref[...])


def _post(x, attn, ho, gg, g1, sh2, sc2, g2, wts):
    bp, tp, _ = x.shape
    r = min(ROW_TILE, tp)
    mod_rows = g1.shape[1]
    mod_blk = (1, 1, D_MODEL) if mod_rows == 1 else (1, r, D_MODEL)
    mod_map = (lambda b, t: (b, 0, 0)) if mod_rows == 1 else (lambda b, t: (b, t, 0))
    row3 = lambda w: pl.BlockSpec((1, r, w), lambda b, t: (b, t, 0))
    mod = pl.BlockSpec(mod_blk, mod_map)
    names = ["wpa", "wpb", "wout", "norm2", "wgu", "wdown", "final_norm"]
    return pl.pallas_call(
        _post_kernel,
        grid=(bp, tp // r),
        in_specs=[row3(D_MODEL), row3(MLA_WIDTH), row3(HG_WIDTH), row3(2 * D_MODEL), mod, mod, mod, mod]
                 + [_const_spec(wts[n].shape) for n in names],
        out_specs=row3(D_MODEL),
        out_shape=jax.ShapeDtypeStruct((bp, tp, D_MODEL), F32),
        compiler_params=pltpu.CompilerParams(dimension_semantics=("arbitrary", "arbitrary"),
                                             vmem_limit_bytes=VMEM_LIMIT),
        name="post",
    )(x, attn, ho, gg, g1, sh2, sc2, g2, *[wts[n] for n in names])


def _prep_weights(w_in, q_norm, w_uq, kv_norm, w_ukv, lb_param, hg_norm, w_pa, w_pb, w_out,
                  norm1, norm2, w_gu, w_down, final_norm):
    half = QK_ROPE // 2
    offs = np.cumsum([0, Q_LORA, KV_LORA, QK_ROPE, HG_WIDTH, HG_WIDTH, HG_WIDTH, HG_WIDTH, D_MODEL, D_MODEL])
    w = w_in[0]
    w_kr = w[:, offs[2]:offs[3]]
    zeros = lambda n: jnp.zeros((D_MODEL, n), w.dtype)
    kr_a = jnp.concatenate([zeros(ROPE_LANE0), w_kr, zeros(LANES - ROPE_LANE0 - QK_ROPE)], axis=1)
    kr_b = jnp.concatenate([zeros(ROPE_LANE0), w_kr[:, half:], w_kr[:, :half],
                            zeros(LANES - ROPE_LANE0 - QK_ROPE)], axis=1)
    w_in_p = jnp.concatenate([w[:, offs[0]:offs[2]], w[:, offs[3]:offs[9]], kr_a, kr_b], axis=1).astype(BF16)

    uq = w_uq[0].reshape(Q_LORA, MLA_HEADS, QK_NOPE + QK_ROPE)
    pad = jnp.zeros((Q_LORA, MLA_HEADS, HEAD_PAD - QK_NOPE - QK_ROPE), uq.dtype)
    wqa = jnp.concatenate([uq, pad], axis=2).reshape(Q_LORA, MLA_HEADS * HEAD_PAD).astype(BF16)
    wqb = jnp.concatenate([jnp.zeros((Q_LORA, MLA_HEADS, QK_NOPE), uq.dtype),
                           uq[:, :, QK_NOPE + half:], uq[:, :, QK_NOPE:QK_NOPE + half], pad],
                          axis=2).reshape(Q_LORA, MLA_HEADS * HEAD_PAD).astype(BF16)

    ukv = w_ukv[0].reshape(KV_LORA, MLA_HEADS, QK_NOPE + V_HEAD)
    uk = ukv[:, :, :QK_NOPE]
    uv = ukv[:, :, QK_NOPE:]
    wuk = jnp.concatenate([uk, jnp.zeros((KV_LORA, MLA_HEADS, HEAD_PAD - QK_NOPE), uk.dtype)],
                          axis=2).reshape(KV_LORA, MLA_HEADS * HEAD_PAD).astype(BF16)
    wuvt = uv.reshape(KV_LORA, MLA_WIDTH).T.astype(BF16)
    wuk3 = jnp.transpose(uk, (1, 0, 2)).astype(BF16)
    eye = jnp.eye(MLA_HEADS, dtype=uv.dtype)
    wuvp = (jnp.transpose(uv, (1, 0, 2))[:, :, None, :] * eye[:, None, :, None]
            ).reshape(MLA_HEADS, KV_LORA, MLA_WIDTH).astype(BF16)

    lb = jnp.cumsum(jax.nn.softmax(lb_param.astype(F32), axis=0), axis=0)[0].reshape(1, HG_WIDTH)
    return {
        "w_in": w_in_p, "wqa": wqa, "wqb": wqb, "wqa_t": wqa.T,
        "wuk": wuk, "wuvt": wuvt, "wuk3": wuk3, "wuvp": wuvp, "lb": lb,
        "q_norm": q_norm[0].reshape(1, Q_LORA), "kv_norm": kv_norm[0].reshape(1, KV_LORA),
        "hg_norm": hg_norm[0].reshape(1, HG_VDIM),
        "norm1": norm1[0].reshape(1, D_MODEL), "norm2": norm2[0].reshape(1, D_MODEL),
        "final_norm": final_norm.reshape(1, D_MODEL),
        "wpa": w_pa[0].astype(BF16), "wpb": w_pb[0].astype(BF16), "wout": w_out[0].astype(BF16),
        "wgu": w_gu[0].astype(BF16), "wdown": w_down[0].astype(BF16),
    }


def _rope_tables(pos):
    half = QK_ROPE // 2
    inv = ROPE_THETA ** (-np.arange(half, dtype=np.float64) / half)
    ang = np.asarray(pos, np.float64)[:, None] * inv[None, :]
    cos, sin = np.cos(ang), np.sin(ang)
    t = len(pos)
    ck = np.zeros((t, HEAD_PAD))
    sk = np.zeros((t, HEAD_PAD))
    ck[:, ROPE_LANE0:ROPE_LANE0 + QK_ROPE] = np.concatenate([cos, cos], axis=1)
    sk[:, ROPE_LANE0:ROPE_LANE0 + QK_ROPE] = np.concatenate([-sin, sin], axis=1)
    cq = ck.copy()
    cq[:, :QK_NOPE] = 1.0
    f = lambda a: jnp.asarray(a.astype(np.float32))
    return {"ck": f(ck), "sk": f(sk), "cq": f(cq), "sq": f(sk)}


def _trunk(prompt, x, mod, wts, tabs, s0=None, cache=None):
    sh1, sc1, g1, sh2, sc2, g2 = mod
    if prompt:
        w = dict(wts, wqa=wts["wqa_t"])
        t = dict(tabs, cq=tabs["cq"].T, sq=tabs["sq"].T)
        qt, k, vt, ckv, kr_t, ho, s_new, gg = _pre(True, x, sh1, sc1, w, t)
        kr = jnp.swapaxes(kr_t, 1, 2)
        attn = _attn_prompt(qt, k, vt)
    else:
        q, ckv, kr, hh, gg = _pre(False, x, sh1, sc1, wts, tabs)
        nb, t_new = cache[0].shape[1], x.shape[1] // cache[0].shape[1]
        attn = _attn_sample(q.reshape(nb, t_new, -1), cache[0], cache[1],
                            ckv.reshape(nb, t_new, KV_LORA), kr.reshape(nb, t_new, QK_ROPE), wts)
        attn = attn.reshape(1, nb * t_new, MLA_WIDTH)
        ho, s_new = _hgrn(hh.reshape(nb, t_new, 4 * HG_WIDTH), wts["lb"], wts["hg_norm"], s0, t_new)
        ho = ho.reshape(x.shape[0], x.shape[1], HG_WIDTH)
    y = _post(x, attn, ho, gg, g1, sh2, sc2, g2, wts)
    return y, ckv, kr, s_new


def kernel(x_prompt, x_sample, cache_ckv, cache_krope, state_hgrn, c_prompt, c_sample, w_in, q_norm, w_uq,
           kv_norm, w_ukv, lb_param, hg_norm, w_pa, w_pb, w_out, norm1, norm2, w_ada, b_ada, w_gu, w_down,
           final_norm):
    bsz, t_p, _ = x_prompt.shape
    nb, t_s, _ = x_sample.shape
    past = cache_ckv.shape[2]
    assert past % CHUNK == 0 and t_s <= CHUNK
    wts = _prep_weights(w_in, q_norm, w_uq, kv_norm, w_ukv, lb_param, hg_norm, w_pa, w_pb, w_out,
                        norm1, norm2, w_gu, w_down, final_norm)

    n_c = bsz + nb
    n_pad = -n_c % 8
    c_all = jnp.concatenate([c_prompt, c_sample, jnp.zeros((n_pad, D_MODEL), F32)], axis=0)
    mod_all = _ada(c_all, w_ada[0], b_ada[0])
    mod_p = [m[:, None, :] for m in jnp.split(mod_all[:bsz], 6, axis=-1)]
    mod_s = [jnp.repeat(m, t_s, axis=0)[None] for m in jnp.split(mod_all[bsz:n_c], 6, axis=-1)]

    tabs_p = _rope_tables(np.arange(t_p))
    tabs_s = _rope_tables(np.tile(past + np.arange(t_s), nb))

    y_p, ckv_p, kr_p, s_p = _trunk(True, x_prompt, mod_p, wts, tabs_p)
    y_s, ckv_s, kr_s, s_s = _trunk(False, x_sample.reshape(1, nb * t_s, D_MODEL), mod_s, wts, tabs_s,
                                   s0=state_hgrn[0], cache=(cache_ckv, cache_krope))
    return (y_p, y_s.reshape(nb, t_s, D_MODEL), ckv_p[None], kr_p[None], s_p[None],
            ckv_s.reshape(nb, t_s, KV_LORA)[None], kr_s.reshape(nb, t_s, QK_ROPE)[None], s_s[None])
```
